```python
import math
import jax, jax.numpy as jnp
from jax import lax
import numpy as np

D_MODEL = 1024
BATCH = 8
SEQ = 4096
DEPTH = 1

D_MIX = D_MODEL
A_HEADS = 8
A_HEAD_DIM = 64
D_A = A_HEADS * A_HEAD_DIM
B_GROUPS = 8
B_GROUP_DIM = 64
D_B = B_GROUPS * B_GROUP_DIM
D_IN = 2 * D_A + D_B
CHUNK = 128

N_EXPERTS = 64
TOP_K = 8
N_EXPERT_GROUPS = 8
TOPK_GROUPS = 4
D_EXPERT = 256
D_SHARED = 256
ROUTED_SCALE = 2.5
EXPERT_BLOCK = 128
EPS = 1e-6

kernel_name = "hybrid_sgu_fnet_moe_encoder"


def rms_norm(x, g):
    xf = x.astype(jnp.float32)
    y = xf * lax.rsqrt(jnp.mean(xf * xf, axis=-1, keepdims=True) + EPS)
    return (y * g.astype(jnp.float32)).astype(x.dtype)


def layer_norm(x, g, b):
    xf = x.astype(jnp.float32)
    mu = jnp.mean(xf, axis=-1, keepdims=True)
    xc = xf - mu
    y = xc * lax.rsqrt(jnp.mean(xc * xc, axis=-1, keepdims=True) + EPS)
    return (y * g.astype(jnp.float32) + b.astype(jnp.float32)).astype(x.dtype)


def spatial_gating(z, ln_g, ln_b, w_s, b_s):
    u, v = jnp.split(z, 2, axis=-1)
    v = layer_norm(v, ln_g, ln_b)
    bsz, s, _ = v.shape
    v = v.reshape(bsz, s // CHUNK, CHUNK, A_HEADS, A_HEAD_DIM)
    mixed = jnp.einsum('hpq,bcqhd->bcphd', w_s, v) + b_s.T[None, None, :, :, None]
    return u * mixed.reshape(bsz, s, D_A)


def fourier_mix(z):
    bsz, s, _ = z.shape
    zf = z.astype(jnp.float32).reshape(bsz, s, B_GROUPS, B_GROUP_DIM)
    f = jnp.fft.fft2(zf, axes=(1, 3), norm='ortho')
    return jnp.real(f).reshape(bsz, s, D_B).astype(z.dtype)


def route(xf, w_router, router_bias):
    t = xf.shape[0]
    scores = jax.nn.sigmoid(xf.astype(jnp.float32) @ w_router.astype(jnp.float32))
    biased = scores + router_bias.astype(jnp.float32)
    grp = biased.reshape(t, N_EXPERT_GROUPS, N_EXPERTS // N_EXPERT_GROUPS)
    group_score = jnp.sum(lax.top_k(grp, 2)[0], axis=-1)
    _, gidx = lax.top_k(group_score, TOPK_GROUPS)
    gmask = jnp.sum(jax.nn.one_hot(gidx, N_EXPERT_GROUPS, dtype=jnp.float32), axis=1) > 0
    emask = jnp.repeat(gmask, N_EXPERTS // N_EXPERT_GROUPS, axis=1)
    masked = jnp.where(emask, biased, -jnp.inf)
    _, eidx = lax.top_k(masked, TOP_K)
    w = jnp.take_along_axis(scores, eidx, axis=1)
    w = w / jnp.sum(w, axis=-1, keepdims=True) * ROUTED_SCALE
    return eidx, w


def routed_experts(xf, eidx, gates, w_gate, w_up, w_down):
    t, d = xf.shape
    n_assign = t * TOP_K
    e_flat = eidx.reshape(n_assign)
    tok_flat = jnp.repeat(jnp.arange(t, dtype=jnp.int32), TOP_K)
    g_flat = gates.reshape(n_assign).astype(xf.dtype)
    order = jnp.argsort(e_flat)
    e_s = e_flat[order]
    tok_s = tok_flat[order]
    g_s = g_flat[order]
    counts = jnp.bincount(e_flat, length=N_EXPERTS)
    starts = jnp.cumsum(counts) - counts
    padded = (counts + EXPERT_BLOCK - 1) // EXPERT_BLOCK * EXPERT_BLOCK
    pad_ends = jnp.cumsum(padded)
    pad_starts = pad_ends - padded
    dest = pad_starts[e_s] + (jnp.arange(n_assign, dtype=jnp.int32) - starts[e_s])
    total = n_assign + N_EXPERTS * EXPERT_BLOCK
    n_blocks = total // EXPERT_BLOCK
    buf_tok = jnp.full((total,), t, jnp.int32).at[dest].set(tok_s)
    buf_gate = jnp.zeros((total,), xf.dtype).at[dest].set(g_s)
    block_e = jnp.minimum(
        jnp.searchsorted(pad_ends, jnp.arange(n_blocks, dtype=jnp.int32) * EXPERT_BLOCK, side='right'),
        N_EXPERTS - 1)
    x_pad = jnp.concatenate([xf, jnp.zeros((1, d), xf.dtype)], axis=0)

    def step(y, blk):
        tok, gate, e = blk
        xb = x_pad[tok]
        h = jax.nn.silu(xb @ w_gate[e]) * (xb @ w_up[e])
        out = (h @ w_down[e]) * gate[:, None]
        return y.at[tok].add(out), None

    y0 = jnp.zeros((t + 1, d), xf.dtype)
    y, _ = lax.scan(step, y0, (buf_tok.reshape(n_blocks, EXPERT_BLOCK),
                               buf_gate.reshape(n_blocks, EXPERT_BLOCK), block_e))
    return y[:t]


def setup_inputs(seed: int = 0) -> dict:
    key = jax.random.key(seed)
    ks = jax.random.split(key, 22)
    f32 = jnp.float32
    L = DEPTH

    def nrm(k, shape, scale):
        return jax.random.normal(k, shape, f32) * scale

    return {
        "x": jax.random.normal(ks[0], (BATCH, SEQ, D_MODEL), f32),
        "norm1_g": 1.0 + nrm(ks[1], (L, D_MODEL), 0.01),
        "w_in": nrm(ks[2], (L, D_MODEL, D_IN), D_MODEL ** -0.5),
        "sgu_ln_g": 1.0 + nrm(ks[3], (L, D_A), 0.01),
        "sgu_ln_b": nrm(ks[4], (L, D_A), 0.02),
        "w_spatial": nrm(ks[5], (L, A_HEADS, CHUNK, CHUNK), CHUNK ** -0.5),
        "b_spatial": 1.0 + nrm(ks[6], (L, A_HEADS, CHUNK), 0.01),
        "out_norm_a_g": 1.0 + nrm(ks[7], (L, D_A), 0.01),
        "out_norm_b_g": 1.0 + nrm(ks[8], (L, D_B), 0.01),
        "w_out": nrm(ks[9], (L, D_MIX, D_MODEL), D_MIX ** -0.5),
        "norm2_g": 1.0 + nrm(ks[10], (L, D_MODEL), 0.01),
        "w_router": nrm(ks[11], (L, D_MODEL, N_EXPERTS), D_MODEL ** -0.5),
        "router_bias": nrm(ks[12], (L, N_EXPERTS), 0.01),
        "w_gate": nrm(ks[13], (L, N_EXPERTS, D_MODEL, D_EXPERT), D_MODEL ** -0.5),
        "w_up": nrm(ks[14], (L, N_EXPERTS, D_MODEL, D_EXPERT), D_MODEL ** -0.5),
        "w_down": nrm(ks[15], (L, N_EXPERTS, D_EXPERT, D_MODEL), D_EXPERT ** -0.5),
        "ws_gate": nrm(ks[16], (L, D_MODEL, D_SHARED), D_MODEL ** -0.5),
        "ws_up": nrm(ks[17], (L, D_MODEL, D_SHARED), D_MODEL ** -0.5),
        "ws_down": nrm(ks[18], (L, D_SHARED, D_MODEL), D_SHARED ** -0.5),
        "final_norm_g": 1.0 + nrm(ks[19], (D_MODEL,), 0.01),
    }


def reference(x, norm1_g, w_in, sgu_ln_g, sgu_ln_b, w_spatial, b_spatial,
              out_norm_a_g, out_norm_b_g, w_out, norm2_g, w_router, router_bias,
              w_gate, w_up, w_down, ws_gate, ws_up, ws_down, final_norm_g):
    bsz, s, d = x.shape
    h = x
    for l in range(DEPTH):
        xn = rms_norm(h, norm1_g[l])
        z = jnp.einsum('bsd,de->bse', xn, w_in[l])
        z_a = jax.nn.gelu(z[..., :2 * D_A], approximate=False)
        z_b = z[..., 2 * D_A:]
        y_a = spatial_gating(z_a, sgu_ln_g[l], sgu_ln_b[l], w_spatial[l], b_spatial[l])
        y_b = fourier_mix(z_b)
        mix = jnp.concatenate([rms_norm(y_a, out_norm_a_g[l]),
                               rms_norm(y_b, out_norm_b_g[l])], axis=-1)
        h = h + jnp.einsum('bse,ed->bsd', mix, w_out[l])
        hn = rms_norm(h, norm2_g[l]).reshape(bsz * s, d)
        eidx, gates = route(hn, w_router[l], router_bias[l])
        routed = routed_experts(hn, eidx, gates, w_gate[l], w_up[l], w_down[l])
        shared = (jax.nn.silu(hn @ ws_gate[l]) * (hn @ ws_up[l])) @ ws_down[l]
        h = h + (routed + shared).reshape(bsz, s, d)
    return rms_norm(h, final_norm_g)
```

```python
import functools
import math

import numpy as np
import jax
import jax.numpy as jnp
from jax import lax
from jax.experimental import pallas as pl
from jax.experimental.pallas import tpu as pltpu

A_HEAD_DIM = 64
B_GROUP_DIM = 64
TOP_K = 8
N_EXPERT_GROUPS = 8
TOPK_GROUPS = 4
ROUTED_SCALE = 2.5
EPS = 1e-6

LANES = 128
FRONT_ROWS = 512
EXPERT_ROWS = 256
DISPATCH_TOKENS = 512
COMBINE_TOKENS = 128
FFT_RADIX = 64
VMEM_LIMIT = 56 * 1024 * 1024

F32 = jnp.float32
BF16 = jnp.bfloat16


def _rms(x, g):
    return x * lax.rsqrt(jnp.mean(x * x, axis=-1, keepdims=True) + EPS) * g


def _gelu(x):
    return 0.5 * x * (1.0 + lax.erf(x * (1.0 / math.sqrt(2.0))))


def _silu(x):
    return x / (1.0 + jnp.exp(-x))


def _front_kernel(x_ref, g1_ref, win_ref, lng_ref, lnb_ref, ws_ref, bs_ref, ga_ref,
                  an_ref, zb_ref, ya_scr, *, d_a, chunk):
    x = x_ref[...]
    xn = _rms(x, g1_ref[...]).astype(BF16)
    z = jnp.dot(xn, win_ref[...], preferred_element_type=F32)
    for q in range(zb_ref.shape[0]):
        zb_ref[q] = z[:, 2 * d_a + q * LANES:2 * d_a + (q + 1) * LANES]
    u = _gelu(z[:, :d_a])
    v = _gelu(z[:, d_a:2 * d_a])
    mu = jnp.mean(v, axis=-1, keepdims=True)
    vc = v - mu
    v = vc * lax.rsqrt(jnp.mean(vc * vc, axis=-1, keepdims=True) + EPS) * lng_ref[...] + lnb_ref[...]
    rows = x.shape[0]
    lane = lax.broadcasted_iota(jnp.int32, (chunk, LANES), 1)
    low = lane < A_HEAD_DIM
    for c in range(rows // chunk):
        r0 = c * chunk
        for hp in range(d_a // LANES):
            c0 = hp * LANES
            vp = v[r0:r0 + chunk, c0:c0 + LANES]
            rhs = jnp.concatenate([jnp.where(low, vp, 0.0), jnp.where(low, 0.0, vp)], axis=0).astype(BF16)
            mixed = jnp.dot(ws_ref[hp], rhs, preferred_element_type=F32) + bs_ref[:, c0:c0 + LANES]
            ya_scr[r0:r0 + chunk, c0:c0 + LANES] = u[r0:r0 + chunk, c0:c0 + LANES] * mixed
    an_ref[...] = _rms(ya_scr[...], ga_ref[...]).astype(BF16)


def _front(x2, g1, win, lng, lnb, ws_pair, bs_full, ga, *, d_a, d_b, chunk):
    t, d = x2.shape
    d_in = win.shape[1]
    rows = FRONT_ROWS
    const = lambda shape: pl.BlockSpec(shape, lambda i: (0,) * len(shape))
    return pl.pallas_call(
        functools.partial(_front_kernel, d_a=d_a, chunk=chunk),
        grid=(t // rows,),
        in_specs=[
            pl.BlockSpec((rows, d), lambda i: (i, 0)),
            const((1, d)), const((d, d_in)), const((1, d_a)), const((1, d_a)),
            const(ws_pair.shape), const(bs_full.shape), const((1, d_a)),
        ],
        out_specs=[pl.BlockSpec((rows, d_a), lambda i: (i, 0)),
                   pl.BlockSpec((d_b // LANES, rows, LANES), lambda i: (0, i, 0))],
        out_shape=[jax.ShapeDtypeStruct((t, d_a), BF16), jax.ShapeDtypeStruct((d_b // LANES, t, LANES), F32)],
        scratch_shapes=[pltpu.VMEM((rows, d_a), F32)],
        compiler_params=pltpu.CompilerParams(dimension_semantics=("parallel",),
                                             vmem_limit_bytes=VMEM_LIMIT),
        name="front",
    )(x2, g1, win, lng, lnb, ws_pair, bs_full, ga)


def _fnet_tables(seq, half_w):
    r = FFT_RADIX
    assert seq == r * r
    j = np.arange(B_GROUP_DIM)
    ang = 2.0 * np.pi * np.outer(j, j) / B_GROUP_DIM
    ng = half_w // B_GROUP_DIM
    eye = np.eye(ng)
    m_ch = np.concatenate([np.kron(eye, np.cos(ang)), np.kron(eye, np.sin(ang))], axis=1)
    k1 = np.arange(r)[:, None]
    n1 = np.arange(r)[None, :]
    d_tw = np.zeros((r, 2 * r, 2 * r))
    for n2 in range(r):
        th = 2.0 * np.pi * (k1 * n1 / r + k1 * n2 / (r * r))
        c, s = np.cos(th), np.sin(th)
        d_tw[n2] = np.block([[c, -s], [s, c]])
    ph = 2.0 * np.pi * np.outer(np.arange(r), np.arange(r)) / r
    scale = 1.0 / math.sqrt(seq * B_GROUP_DIM)
    d2 = np.concatenate([np.cos(ph), -np.sin(ph)], axis=1) * scale
    return (jnp.asarray(m_ch, F32), jnp.asarray(d_tw, F32), jnp.asarray(d2, F32))


def _fnet_kernel(z_ref, mch32_ref, dtw_ref, d232_ref, gb_ref, out_ref, a2_scr, y_scr, mch_ref, d2_ref,
                 *, n_col, unroll):
    r = FFT_RADIX
    hf = pl.program_id(1)
    half_w = n_col * LANES
    mch_ref[...] = mch32_ref[...].astype(BF16)
    d2_ref[...] = d232_ref[...].astype(BF16)

    def stage1(it, carry):
        for u in range(unroll):
            n2 = it * unroll + u
            zs = jnp.concatenate([z_ref[q, pl.ds(n2, r, stride=r), :] for q in range(n_col)],
                                 axis=1).astype(BF16)
            w = jnp.dot(zs, mch_ref[...], preferred_element_type=F32)
            rhs = jnp.concatenate([w[:, :half_w], w[:, half_w:]], axis=0).astype(BF16)
            a = jnp.dot(dtw_ref[n2].astype(BF16), rhs, preferred_element_type=F32)
            for q in range(n_col):
                a2_scr[q, pl.ds(n2, r, stride=2 * r), :] = a[:r, q * LANES:(q + 1) * LANES]
                a2_scr[q, pl.ds(r + n2, r, stride=2 * r), :] = a[r:, q * LANES:(q + 1) * LANES]
        return carry

    lax.fori_loop(0, r // unroll, stage1, 0)

    def stage2(it, carry):
        for u in range(unroll):
            k1 = it * unroll + u
            row0 = pl.multiple_of(k1 * 2 * r, 2 * r)
            blk = jnp.concatenate([a2_scr[q, pl.ds(row0, 2 * r), :] for q in range(n_col)],
                                  axis=1).astype(BF16)
            y = jnp.dot(d2_ref[...], blk, preferred_element_type=F32)
            for q in range(n_col):
                y_scr[hf * n_col + q, pl.ds(k1, r, stride=r), :] = y[:, q * LANES:(q + 1) * LANES]
        return carry

    lax.fori_loop(0, r // unroll, stage2, 0)

    @pl.when(hf == pl.num_programs(1) - 1)
    def _():
        rows = 512
        n_all = y_scr.shape[0]
        for c in range(y_scr.shape[1] // rows):
            ys = [y_scr[q, c * rows:(c + 1) * rows, :] for q in range(n_all)]
            ss = sum(jnp.sum(y * y, axis=-1, keepdims=True) for y in ys)
            inv = lax.rsqrt(ss * (1.0 / (n_all * LANES)) + EPS)
            for q in range(n_all):
                out_ref[0, c * rows:(c + 1) * rows, q * LANES:(q + 1) * LANES] = (
                    ys[q] * inv * gb_ref[:, q * LANES:(q + 1) * LANES]).astype(BF16)


def _fnet(zb_cols, gb, bsz):
    n_all, t, _ = zb_cols.shape
    s = t // bsz
    d_b = n_all * LANES
    n_col = 2
    r = FFT_RADIX
    m_ch, d_tw, d2 = _fnet_tables(s, n_col * LANES)
    return pl.pallas_call(
        functools.partial(_fnet_kernel, n_col=n_col, unroll=4),
        grid=(bsz, n_all // n_col),
        in_specs=[
            pl.BlockSpec((n_col, s, LANES), lambda i, h: (h, i, 0)),
            pl.BlockSpec(m_ch.shape, lambda i, h: (0, 0)),
            pl.BlockSpec(d_tw.shape, lambda i, h: (0, 0, 0)),
            pl.BlockSpec(d2.shape, lambda i, h: (0, 0)),
            pl.BlockSpec((1, d_b), lambda i, h: (0, 0)),
        ],
        out_specs=pl.BlockSpec((1, s, d_b), lambda i, h: (i, 0, 0)),
        out_shape=jax.ShapeDtypeStruct((bsz, s, d_b), BF16),
        scratch_shapes=[pltpu.VMEM((n_col, r * 2 * r, LANES), F32), pltpu.VMEM((n_all, s, LANES), F32),
                        pltpu.VMEM(m_ch.shape, BF16), pltpu.VMEM(d2.shape, BF16)],
        compiler_params=pltpu.CompilerParams(dimension_semantics=("parallel", "arbitrary"),
                                             vmem_limit_bytes=VMEM_LIMIT),
        name="fnet",
    )(zb_cols, m_ch, d_tw, d2, gb)


def _post_kernel(x_ref, an_ref, bn_ref, woa_ref, wob_ref, g2_ref, wrt_ref, rb_ref, wsgu_ref, wsd_ref,
                 tri_ref, hs_ref, hn_ref, eidx_ref, gate_ref, rank_ref, cnt_ref, carry_scr,
                 *, n_experts, d_shared):
    i = pl.program_id(0)

    @pl.when(i == 0)
    def _():
        carry_scr[...] = jnp.zeros_like(carry_scr)

    h = (x_ref[...]
         + jnp.dot(an_ref[...], woa_ref[...], preferred_element_type=F32)
         + jnp.dot(bn_ref[...], wob_ref[...], preferred_element_type=F32))
    hn = _rms(h, g2_ref[...])
    hn_ref[...] = hn
    hnb = hn.astype(BF16)
    gu = jnp.dot(hnb, wsgu_ref[...], preferred_element_type=F32)
    act = (_silu(gu[:, :d_shared]) * gu[:, d_shared:]).astype(BF16)
    hs_ref[...] = h + jnp.dot(act, wsd_ref[...], preferred_element_type=F32)

    rows = hn.shape[0]
    eg = n_experts // N_EXPERT_GROUPS
    logits_t = lax.dot_general(wrt_ref[...], hn, (((1,), (1,)), ((), ())),
                               precision=lax.Precision.HIGHEST, preferred_element_type=F32)
    scores_t = 1.0 / (1.0 + jnp.exp(-logits_t))
    biased_t = scores_t + rb_ref[...]
    sub = lax.broadcasted_iota(jnp.int32, (eg, rows), 0)
    neg = -jnp.inf
    grp = [biased_t[g * eg:(g + 1) * eg, :] for g in range(N_EXPERT_GROUPS)]
    sco = [scores_t[g * eg:(g + 1) * eg, :] for g in range(N_EXPERT_GROUPS)]
    gid = [sub + g * eg for g in range(N_EXPERT_GROUPS)]

    gscore = []
    for g in range(N_EXPERT_GROUPS):
        m1 = jnp.max(grp[g], axis=0, keepdims=True)
        first = jnp.min(jnp.where(grp[g] == m1, sub, eg), axis=0, keepdims=True)
        m2 = jnp.max(jnp.where(sub == first, neg, grp[g]), axis=0, keepdims=True)
        gscore.append(m1 + m2)
    masked = []
    for g in range(N_EXPERT_GROUPS):
        beaten = jnp.zeros((1, rows), jnp.int32)
        for j in range(N_EXPERT_GROUPS):
            if j == g:
                continue
            wins = (gscore[j] > gscore[g]) if j > g else (gscore[j] >= gscore[g])
            beaten = beaten + wins.astype(jnp.int32)
        masked.append(jnp.where(beaten < TOPK_GROUPS, grp[g], neg))

    sel = [jnp.zeros((eg, rows), F32) for _ in range(N_EXPERT_GROUPS)]
    e_rows, w_rows = [], []
    for _ in range(TOP_K):
        m = masked[0]
        for g in range(1, N_EXPERT_GROUPS):
            m = jnp.maximum(m, masked[g])
        mk = jnp.max(m, axis=0, keepdims=True)
        cand = jnp.where(masked[0] == mk, gid[0], n_experts)
        for g in range(1, N_EXPERT_GROUPS):
            cand = jnp.minimum(cand, jnp.where(masked[g] == mk, gid[g], n_experts))
        ik = jnp.min(cand, axis=0, keepdims=True)
        wk = jnp.zeros((eg, rows), F32)
        for g in range(N_EXPERT_GROUPS):
            hit = gid[g] == ik
            masked[g] = jnp.where(hit, neg, masked[g])
            sel[g] = jnp.where(hit, 1.0, sel[g])
            wk = wk + jnp.where(hit, sco[g], 0.0)
        e_rows.append(ik)
        w_rows.append(jnp.sum(wk, axis=0, keepdims=True))
    wsum = w_rows[0]
    for k in range(1, TOP_K):
        wsum = wsum + w_rows[k]
    gates = [w_rows[k] / wsum * ROUTED_SCALE for k in range(TOP_K)]

    sel_t = jnp.concatenate(sel, axis=0)
    within = jnp.dot(sel_t.astype(BF16), tri_ref[...], preferred_element_type=F32)
    posn = within + carry_scr[:, 0:1]
    r_rows = []
    for k in range(TOP_K):
        acc = jnp.zeros((eg, rows), F32)
        for g in range(N_EXPERT_GROUPS):
            acc = acc + jnp.where(gid[g] == e_rows[k], posn[g * eg:(g + 1) * eg, :], 0.0)
        r_rows.append(jnp.sum(acc, axis=0, keepdims=True))
    new_carry = carry_scr[...] + jnp.sum(sel_t, axis=1, keepdims=True)
    carry_scr[...] = new_carry
    cnt_ref[...] = new_carry.astype(jnp.int32)
    eidx_ref[...] = jnp.concatenate(e_rows, axis=0)
    gate_ref[...] = jnp.concatenate(gates, axis=0)
    rank_ref[...] = jnp.concatenate(r_rows, axis=0).astype(jnp.int32)


def _post(x2, an, bn, woa, wob, g2, wrt, rb, wsgu, wsd):
    t, d = x2.shape
    rows = FRONT_ROWS
    n_experts = wrt.shape[0]
    d_shared = wsd.shape[0]
    tri = jnp.asarray(np.triu(np.ones((rows, rows), np.float32), k=1), BF16)
    const = lambda shape: pl.BlockSpec(shape, lambda i: (0,) * len(shape))
    row_blk = lambda w: pl.BlockSpec((rows, w), lambda i: (i, 0))
    tok_blk = pl.BlockSpec((TOP_K, rows), lambda i: (0, i))
    return pl.pallas_call(
        functools.partial(_post_kernel, n_experts=n_experts, d_shared=d_shared),
        grid=(t // rows,),
        in_specs=[row_blk(d), row_blk(an.shape[1]), row_blk(bn.shape[1]),
                  const(woa.shape), const(wob.shape), const((1, d)), const(wrt.shape), const(rb.shape),
                  const(wsgu.shape), const(wsd.shape), const(tri.shape)],
        out_specs=[row_blk(d), row_blk(d), tok_blk, tok_blk, tok_blk, const((n_experts, LANES))],
        out_shape=[jax.ShapeDtypeStruct((t, d), F32), jax.ShapeDtypeStruct((t, d), F32),
                   jax.ShapeDtypeStruct((TOP_K, t), jnp.int32), jax.ShapeDtypeStruct((TOP_K, t), F32),
                   jax.ShapeDtypeStruct((TOP_K, t), jnp.int32),
                   jax.ShapeDtypeStruct((n_experts, LANES), jnp.int32)],
        scratch_shapes=[pltpu.VMEM((n_experts, LANES), F32)],
        compiler_params=pltpu.CompilerParams(dimension_semantics=("arbitrary",),
                                             vmem_limit_bytes=VMEM_LIMIT),
        name="post",
    )(x2, an, bn, woa, wob, g2, wrt, rb, wsgu, wsd, tri)


def _dispatch_kernel(pend_ref, dest_ref, hn_hbm, xs_hbm, zero_scr, zsem, sem, *, n_experts):
    i = pl.program_id(0)
    blk = EXPERT_ROWS
    tokens = DISPATCH_TOKENS

    def zero_copy(e):
        start = pl.multiple_of(jnp.maximum(pend_ref[e] - blk, 0), blk)
        return pltpu.make_async_copy(zero_scr, xs_hbm.at[pl.ds(start, blk)], zsem)

    @pl.when(i == 0)
    def _():
        zero_scr[...] = jnp.zeros_like(zero_scr)

        def start_body(e, c):
            zero_copy(e).start()
            return c

        def wait_body(e, c):
            zero_copy(e).wait()
            return c

        lax.fori_loop(0, n_experts, start_body, 0)
        lax.fori_loop(0, n_experts, wait_body, 0)

        def tail_copy(b):
            return pltpu.make_async_copy(zero_scr, xs_hbm.at[pl.ds(pl.multiple_of(b * blk, blk), blk)], zsem)

        def tail_start(b, c):
            tail_copy(b).start()
            return c

        def tail_wait(b, c):
            tail_copy(b).wait()
            return c

        first_free = pend_ref[n_experts - 1] // blk
        lax.fori_loop(first_free, xs_hbm.shape[0] // blk, tail_start, 0)
        lax.fori_loop(first_free, xs_hbm.shape[0] // blk, tail_wait, 0)

    base = i * tokens

    def row_copy(t, d):
        return pltpu.make_async_copy(hn_hbm.at[pl.ds(t, 1)], xs_hbm.at[pl.ds(d, 1)], sem)

    def issue(j, c):
        for k in range(TOP_K):
            row_copy(base + j, dest_ref[j * TOP_K + k]).start()
        return c

    lax.fori_loop(0, tokens, issue, 0)
    for k in range(TOP_K):
        pltpu.make_async_copy(hn_hbm.at[pl.ds(0, tokens)], xs_hbm.at[pl.ds(0, tokens)], sem).wait()


def _dispatch(pad_end, dest_flat, hn, n_rows):
    t, d = hn.shape
    n_experts = pad_end.shape[0]
    tokens = DISPATCH_TOKENS
    grid_spec = pltpu.PrefetchScalarGridSpec(
        num_scalar_prefetch=1,
        grid=(t // tokens,),
        in_specs=[pl.BlockSpec((tokens * TOP_K,), lambda i, pe: (i,), memory_space=pltpu.SMEM),
                  pl.BlockSpec(memory_space=pl.ANY)],
        out_specs=pl.BlockSpec(memory_space=pl.ANY),
        scratch_shapes=[pltpu.VMEM((EXPERT_ROWS, d), F32), pltpu.SemaphoreType.DMA, pltpu.SemaphoreType.DMA],
    )
    return pl.pallas_call(
        functools.partial(_dispatch_kernel, n_experts=n_experts),
        grid_spec=grid_spec,
        out_shape=jax.ShapeDtypeStruct((n_rows, d), F32),
        compiler_params=pltpu.CompilerParams(dimension_semantics=("arbitrary",), has_side_effects=True),
        name="dispatch",
    )(pad_end, dest_flat, hn)


def _expert_kernel(be_ref, nu_ref, xs_ref, wgu_ref, wd_ref, ys_ref, *, d_expert):
    i = pl.program_id(0)

    @pl.when(i < nu_ref[0])
    def _():
        xb = xs_ref[...].astype(BF16)
        gu = jnp.dot(xb, wgu_ref[0], preferred_element_type=F32)
        act = (_silu(gu[:, :d_expert]) * gu[:, d_expert:]).astype(BF16)
        ys_ref[...] = jnp.dot(act, wd_ref[0], preferred_element_type=F32)

    @pl.when(i >= nu_ref[0])
    def _():
        ys_ref[...] = jnp.zeros_like(ys_ref)


def _experts(block_e, n_used, xs, wgu, wd):
    n_rows, d = xs.shape
    blk = EXPERT_ROWS
    d_expert = wd.shape[1]
    grid_spec = pltpu.PrefetchScalarGridSpec(
        num_scalar_prefetch=2,
        grid=(n_rows // blk,),
        in_specs=[pl.BlockSpec((blk, d), lambda i, be, nu: (jnp.minimum(i, nu[0] - 1), 0)),
                  pl.BlockSpec((1, d, 2 * d_expert), lambda i, be, nu: (be[i], 0, 0)),
                  pl.BlockSpec((1, d_expert, d), lambda i, be, nu: (be[i], 0, 0))],
        out_specs=pl.BlockSpec((blk, d), lambda i, be, nu: (i, 0)),
    )
    return pl.pallas_call(
        functools.partial(_expert_kernel, d_expert=d_expert),
        grid_spec=grid_spec,
        out_shape=jax.ShapeDtypeStruct((n_rows, d), F32),
        compiler_params=pltpu.CompilerParams(dimension_semantics=("arbitrary",),
                                             vmem_limit_bytes=VMEM_LIMIT),
        name="experts",
    )(block_e, n_used, xs, wgu, wd)


def _combine_kernel(dcur_ref, dnext_ref, gate_ref, hs_ref, gf_ref, ys_hbm, out_ref, buf, sem):
    i = pl.program_id(0)
    n = pl.num_programs(0)
    tokens = COMBINE_TOKENS
    slot = i % 2

    def row_copy(dref, s, j, k):
        return pltpu.make_async_copy(ys_hbm.at[pl.ds(dref[j * TOP_K + k], 1)],
                                     buf.at[s, k, pl.ds(j, 1)], sem.at[s])

    def issue(dref, s):
        def body(j, c):
            for k in range(TOP_K):
                row_copy(dref, s, j, k).start()
            return c
        lax.fori_loop(0, tokens, body, 0)

    @pl.when(i == 0)
    def _():
        issue(dcur_ref, 0)

    @pl.when(i + 1 < n)
    def _():
        issue(dnext_ref, 1 - slot)

    for k in range(TOP_K):
        pltpu.make_async_copy(ys_hbm.at[pl.ds(0, tokens)], buf.at[slot, k], sem.at[slot]).wait()

    acc = hs_ref[...]
    g = gate_ref[...]
    for k in range(TOP_K):
        acc = acc + g[:, k:k + 1] * buf[slot, k]
    out_ref[...] = _rms(acc, gf_ref[...])


def _combine(dest_flat, gate_tk, hs, gf, ys):
    t, d = hs.shape
    tokens = COMBINE_TOKENS
    n = t // tokens
    dspec = lambda f: pl.BlockSpec((tokens * TOP_K,), f, memory_space=pltpu.SMEM)
    return pl.pallas_call(
        _combine_kernel,
        grid=(n,),
        in_specs=[dspec(lambda i: (i,)), dspec(lambda i: (jnp.minimum(i + 1, n - 1),)),
                  pl.BlockSpec((tokens, TOP_K), lambda i: (i, 0)),
                  pl.BlockSpec((tokens, d), lambda i: (i, 0)),
                  pl.BlockSpec((1, d), lambda i: (0, 0)),
                  pl.BlockSpec(memory_space=pl.ANY)],
        out_specs=pl.BlockSpec((tokens, d), lambda i: (i, 0)),
        out_shape=jax.ShapeDtypeStruct((t, d), F32),
        scratch_shapes=[pltpu.VMEM((2, TOP_K, tokens, d), F32), pltpu.SemaphoreType.DMA((2,))],
        compiler_params=pltpu.CompilerParams(dimension_semantics=("arbitrary",),
                                             vmem_limit_bytes=VMEM_LIMIT),
        name="combine",
    )(dest_flat, dest_flat, gate_tk, hs, gf, ys)


def kernel(x, norm1_g, w_in, sgu_ln_g, sgu_ln_b, w_spatial, b_spatial, out_norm_a_g, out_norm_b_g, w_out,
           norm2_g, w_router, router_bias, w_gate, w_up, w_down, ws_gate, ws_up, ws_down, final_norm_g):
    bsz, s, d = x.shape
    depth = w_in.shape[0]
    assert depth == 1, "the fused final norm assumes a single trunk layer"
    l = 0
    t = bsz * s
    d_a = sgu_ln_g.shape[-1]
    d_b = out_norm_b_g.shape[-1]
    n_heads, chunk = w_spatial.shape[1], w_spatial.shape[2]
    n_experts = w_router.shape[-1]
    assert d_a == n_heads * A_HEAD_DIM and chunk == LANES and 2 * A_HEAD_DIM == LANES
    assert t % FRONT_ROWS == 0 and FRONT_ROWS % chunk == 0 and s % FRONT_ROWS == 0
    assert t % DISPATCH_TOKENS == 0 and t % COMBINE_TOKENS == 0
    row = lambda v: v.reshape(1, -1).astype(F32)

    x2 = x.reshape(t, d)

    win = w_in[l].astype(BF16)
    ws = w_spatial[l]
    ws_pair = jnp.concatenate([ws[0::2], ws[1::2]], axis=2).astype(BF16)
    bs_full = jnp.repeat(b_spatial[l].T, A_HEAD_DIM, axis=1).astype(F32)
    woa = w_out[l][:d_a].astype(BF16)
    wob = w_out[l][d_a:].astype(BF16)
    wrt = w_router[l].T.astype(F32)
    rb = jnp.broadcast_to(router_bias[l].astype(F32)[:, None], (n_experts, 1))
    wsgu = jnp.concatenate([ws_gate[l], ws_up[l]], axis=1).astype(BF16)
    wsd = ws_down[l].astype(BF16)
    wgu = jnp.concatenate([w_gate[l], w_up[l]], axis=2).astype(BF16)
    wd = w_down[l].astype(BF16)

    an, zb = _front(x2, row(norm1_g[l]), win, row(sgu_ln_g[l]), row(sgu_ln_b[l]), ws_pair, bs_full,
                    row(out_norm_a_g[l]), d_a=d_a, d_b=d_b, chunk=chunk)
    bn = _fnet(zb, row(out_norm_b_g[l]), bsz).reshape(t, d_b)

    hs, hn, eidx, gate, rank, cnt = _post(x2, an, bn, woa, wob, row(norm2_g[l]), wrt, rb, wsgu, wsd)

    blk = EXPERT_ROWS
    counts = cnt[:, 0]
    padded = (counts + blk - 1) // blk * blk
    pad_end = jnp.cumsum(padded).astype(jnp.int32)
    pad_start = pad_end - padded
    n_rows = t * TOP_K + n_experts * blk
    n_blocks = n_rows // blk
    dest = pad_start[eidx] + rank
    dest_flat = dest.T.reshape(-1).astype(jnp.int32)
    gate_tk = gate.T
    block_row0 = jnp.arange(n_blocks, dtype=jnp.int32) * blk
    block_e = jnp.minimum(jnp.sum((pad_end[None, :] <= block_row0[:, None]).astype(jnp.int32), axis=1),
                          n_experts - 1)
    n_used = (pad_end[-1:] // blk).astype(jnp.int32)

    xs = _dispatch(pad_end, dest_flat, hn, n_rows)
    ys = _experts(block_e, n_used, xs, wgu, wd)
    out = _combine(dest_flat, gate_tk, hs, row(final_norm_g), ys)
    return out.reshape(bsz, s, d)
```

```python
import functools
import math

import numpy as np
import jax
import jax.numpy as jnp
from jax import lax
from jax.experimental import pallas as pl
from jax.experimental.pallas import tpu as pltpu

A_HEAD_DIM = 64
B_GROUP_DIM = 64
TOP_K = 8
N_EXPERT_GROUPS = 8
TOPK_GROUPS = 4
ROUTED_SCALE = 2.5
EPS = 1e-6

LANES = 128
FRONT_ROWS = 512
EXPERT_ROWS = 256
DISPATCH_TOKENS = 512
COMBINE_TOKENS = 128
FFT_RADIX = 64
VMEM_LIMIT = 56 * 1024 * 1024

F32 = jnp.float32
BF16 = jnp.bfloat16


def _rms(x, g):
    return x * lax.rsqrt(jnp.mean(x * x, axis=-1, keepdims=True) + EPS) * g


def _gelu(x):
    return 0.5 * x * (1.0 + lax.erf(x * (1.0 / math.sqrt(2.0))))


def _silu(x):
    return x / (1.0 + jnp.exp(-x))


def _front_kernel(x_ref, g1_ref, win_ref, lng_ref, lnb_ref, ws_ref, bs_ref, ga_ref,
                  an_ref, zb_ref, ya_scr, *, d_a, chunk):
    x = x_ref[...]
    xn = _rms(x, g1_ref[...]).astype(BF16)
    z = jnp.dot(xn, win_ref[...], preferred_element_type=F32)
    for q in range(zb_ref.shape[0]):
        zb_ref[q] = z[:, 2 * d_a + q * LANES:2 * d_a + (q + 1) * LANES]
    u = _gelu(z[:, :d_a])
    v = _gelu(z[:, d_a:2 * d_a])
    mu = jnp.mean(v, axis=-1, keepdims=True)
    vc = v - mu
    v = vc * lax.rsqrt(jnp.mean(vc * vc, axis=-1, keepdims=True) + EPS) * lng_ref[...] + lnb_ref[...]
    rows = x.shape[0]
    lane = lax.broadcasted_iota(jnp.int32, (chunk, LANES), 1)
    low = lane < A_HEAD_DIM
    for c in range(rows // chunk):
        r0 = c * chunk
        for hp in range(d_a // LANES):
            c0 = hp * LANES
            vp = v[r0:r0 + chunk, c0:c0 + LANES]
            rhs = jnp.concatenate([jnp.where(low, vp, 0.0), jnp.where(low, 0.0, vp)], axis=0).astype(BF16)
            mixed = jnp.dot(ws_ref[hp], rhs, preferred_element_type=F32) + bs_ref[:, c0:c0 + LANES]
            ya_scr[r0:r0 + chunk, c0:c0 + LANES] = u[r0:r0 + chunk, c0:c0 + LANES] * mixed
    an_ref[...] = _rms(ya_scr[...], ga_ref[...]).astype(BF16)


def _front(x2, g1, win, lng, lnb, ws_pair, bs_full, ga, *, d_a, d_b, chunk):
    t, d = x2.shape
    d_in = win.shape[1]
    rows = FRONT_ROWS
    const = lambda shape: pl.BlockSpec(shape, lambda i: (0,) * len(shape))
    return pl.pallas_call(
        functools.partial(_front_kernel, d_a=d_a, chunk=chunk),
        grid=(t // rows,),
        in_specs=[
            pl.BlockSpec((rows, d), lambda i: (i, 0)),
            const((1, d)), const((d, d_in)), const((1, d_a)), const((1, d_a)),
            const(ws_pair.shape), const(bs_full.shape), const((1, d_a)),
        ],
        out_specs=[pl.BlockSpec((rows, d_a), lambda i: (i, 0)),
                   pl.BlockSpec((d_b // LANES, rows, LANES), lambda i: (0, i, 0))],
        out_shape=[jax.ShapeDtypeStruct((t, d_a), BF16), jax.ShapeDtypeStruct((d_b // LANES, t, LANES), F32)],
        scratch_shapes=[pltpu.VMEM((rows, d_a), F32)],
        compiler_params=pltpu.CompilerParams(dimension_semantics=("parallel",),
                                             vmem_limit_bytes=VMEM_LIMIT),
        name="front",
    )(x2, g1, win, lng, lnb, ws_pair, bs_full, ga)


def _fnet_tables(seq, half_w):
    r = FFT_RADIX
    assert seq == r * r
    j = np.arange(B_GROUP_DIM)
    ang = 2.0 * np.pi * np.outer(j, j) / B_GROUP_DIM
    ng = half_w // B_GROUP_DIM
    eye = np.eye(ng)
    m_ch = np.concatenate([np.kron(eye, np.cos(ang)), np.kron(eye, np.sin(ang))], axis=1)
    k1 = np.arange(r)[:, None]
    n1 = np.arange(r)[None, :]
    d_tw = np.zeros((r, 2 * r, 2 * r))
    for n2 in range(r):
        th = 2.0 * np.pi * (k1 * n1 / r + k1 * n2 / (r * r))
        c, s = np.cos(th), np.sin(th)
        d_tw[n2] = np.block([[c, -s], [s, c]])
    ph = 2.0 * np.pi * np.outer(np.arange(r), np.arange(r)) / r
    scale = 1.0 / math.sqrt(seq * B_GROUP_DIM)
    d2 = np.concatenate([np.cos(ph), -np.sin(ph)], axis=1) * scale
    return (jnp.asarray(m_ch, F32), jnp.asarray(d_tw, F32), jnp.asarray(d2, F32))


def _fnet_kernel(z_ref, mch32_ref, dtw_ref, d232_ref, gb_ref, out_ref, a2_scr, y_scr, mch_ref, d2_ref,
                 *, n_col, unroll):
    r = FFT_RADIX
    hf = pl.program_id(1)
    half_w = n_col * LANES
    mch_ref[...] = mch32_ref[...].astype(BF16)
    d2_ref[...] = d232_ref[...].astype(BF16)

    def stage1(it, carry):
        for u in range(unroll):
            n2 = it * unroll + u
            zs = jnp.concatenate([z_ref[q, pl.ds(n2, r, stride=r), :] for q in range(n_col)],
                                 axis=1).astype(BF16)
            w = jnp.dot(zs, mch_ref[...], preferred_element_type=F32)
            rhs = jnp.concatenate([w[:, :half_w], w[:, half_w:]], axis=0).astype(BF16)
            a = jnp.dot(dtw_ref[n2].astype(BF16), rhs, preferred_element_type=F32)
            for q in range(n_col):
                a2_scr[q, pl.ds(n2, r, stride=2 * r), :] = a[:r, q * LANES:(q + 1) * LANES]
                a2_scr[q, pl.ds(r + n2, r, stride=2 * r), :] = a[r:, q * LANES:(q + 1) * LANES]
        return carry

    lax.fori_loop(0, r // unroll, stage1, 0)

    def stage2(it, carry):
        for u in range(unroll):
            k1 = it * unroll + u
            row0 = pl.multiple_of(k1 * 2 * r, 2 * r)
            blk = jnp.concatenate([a2_scr[q, pl.ds(row0, 2 * r), :] for q in range(n_col)],
                                  axis=1).astype(BF16)
            y = jnp.dot(d2_ref[...], blk, preferred_element_type=F32)
            for q in range(n_col):
                y_scr[hf * n_col + q, pl.ds(k1, r, stride=r), :] = y[:, q * LANES:(q + 1) * LANES]
        return carry

    lax.fori_loop(0, r // unroll, stage2, 0)

    @pl.when(hf == pl.num_programs(1) - 1)
    def _():
        rows = 512
        n_all = y_scr.shape[0]
        for c in range(y_scr.shape[1] // rows):
            ys = [y_scr[q, c * rows:(c + 1) * rows, :] for q in range(n_all)]
            ss = sum(jnp.sum(y * y, axis=-1, keepdims=True) for y in ys)
            inv = lax.rsqrt(ss * (1.0 / (n_all * LANES)) + EPS)
            for q in range(n_all):
                out_ref[0, c * rows:(c + 1) * rows, q * LANES:(q + 1) * LANES] = (
                    ys[q] * inv * gb_ref[:, q * LANES:(q + 1) * LANES]).astype(BF16)


def _fnet(zb_cols, gb, bsz):
    n_all, t, _ = zb_cols.shape
    s = t // bsz
    d_b = n_all * LANES
    n_col = 2
    r = FFT_RADIX
    m_ch, d_tw, d2 = _fnet_tables(s, n_col * LANES)
    return pl.pallas_call(
        functools.partial(_fnet_kernel, n_col=n_col, unroll=4),
        grid=(bsz, n_all // n_col),
        in_specs=[
            pl.BlockSpec((n_col, s, LANES), lambda i, h: (h, i, 0)),
            pl.BlockSpec(m_ch.shape, lambda i, h: (0, 0)),
            pl.BlockSpec(d_tw.shape, lambda i, h: (0, 0, 0)),
            pl.BlockSpec(d2.shape, lambda i, h: (0, 0)),
            pl.BlockSpec((1, d_b), lambda i, h: (0, 0)),
        ],
        out_specs=pl.BlockSpec((1, s, d_b), lambda i, h: (i, 0, 0)),
        out_shape=jax.ShapeDtypeStruct((bsz, s, d_b), BF16),
        scratch_shapes=[pltpu.VMEM((n_col, r * 2 * r, LANES), F32), pltpu.VMEM((n_all, s, LANES), F32),
                        pltpu.VMEM(m_ch.shape, BF16), pltpu.VMEM(d2.shape, BF16)],
        compiler_params=pltpu.CompilerParams(dimension_semantics=("parallel", "arbitrary"),
                                             vmem_limit_bytes=VMEM_LIMIT),
        name="fnet",
    )(zb_cols, m_ch, d_tw, d2, gb)


def _post_kernel(x_ref, an_ref, bn_ref, woa_ref, wob_ref, g2_ref, wrt_ref, rb_ref, wsgu_ref, wsd_ref,
                 tri_ref, hs_ref, hn_ref, eidx_ref, gate_ref, rank_ref, cnt_ref, carry_scr,
                 *, n_experts, d_shared):
    i = pl.program_id(0)

    @pl.when(i == 0)
    def _():
        carry_scr[...] = jnp.zeros_like(carry_scr)

    h = (x_ref[...]
         + jnp.dot(an_ref[...], woa_ref[...], preferred_element_type=F32)
         + jnp.dot(bn_ref[...], wob_ref[...], preferred_element_type=F32))
    hn = _rms(h, g2_ref[...])
    hn_ref[...] = hn
    hnb = hn.astype(BF16)
    gu = jnp.dot(hnb, wsgu_ref[...], preferred_element_type=F32)
    act = (_silu(gu[:, :d_shared]) * gu[:, d_shared:]).astype(BF16)
    hs_ref[...] = h + jnp.dot(act, wsd_ref[...], preferred_element_type=F32)

    rows = hn.shape[0]
    eg = n_experts // N_EXPERT_GROUPS
    logits_t = lax.dot_general(wrt_ref[...], hn, (((1,), (1,)), ((), ())),
                               precision=lax.Precision.HIGHEST, preferred_element_type=F32)
    scores_t = 1.0 / (1.0 + jnp.exp(-logits_t))
    biased_t = scores_t + rb_ref[...]
    sub = lax.broadcasted_iota(jnp.int32, (eg, rows), 0)
    neg = -jnp.inf
    grp = [biased_t[g * eg:(g + 1) * eg, :] for g in range(N_EXPERT_GROUPS)]
    sco = [scores_t[g * eg:(g + 1) * eg, :] for g in range(N_EXPERT_GROUPS)]
    gid = [sub + g * eg for g in range(N_EXPERT_GROUPS)]

    gscore = []
    for g in range(N_EXPERT_GROUPS):
        m1 = jnp.max(grp[g], axis=0, keepdims=True)
        first = jnp.min(jnp.where(grp[g] == m1, sub, eg), axis=0, keepdims=True)
        m2 = jnp.max(jnp.where(sub == first, neg, grp[g]), axis=0, keepdims=True)
        gscore.append(m1 + m2)
    masked = []
    for g in range(N_EXPERT_GROUPS):
        beaten = jnp.zeros((1, rows), jnp.int32)
        for j in range(N_EXPERT_GROUPS):
            if j == g:
                continue
            wins = (gscore[j] > gscore[g]) if j > g else (gscore[j] >= gscore[g])
            beaten = beaten + wins.astype(jnp.int32)
        masked.append(jnp.where(beaten < TOPK_GROUPS, grp[g], neg))

    sel = [jnp.zeros((eg, rows), F32) for _ in range(N_EXPERT_GROUPS)]
    e_rows, w_rows = [], []
    for _ in range(TOP_K):
        m = masked[0]
        for g in range(1, N_EXPERT_GROUPS):
            m = jnp.maximum(m, masked[g])
        mk = jnp.max(m, axis=0, keepdims=True)
        cand = jnp.where(masked[0] == mk, gid[0], n_experts)
        for g in range(1, N_EXPERT_GROUPS):
            cand = jnp.minimum(cand, jnp.where(masked[g] == mk, gid[g], n_experts))
        ik = jnp.min(cand, axis=0, keepdims=True)
        wk = jnp.zeros((eg, rows), F32)
        for g in range(N_EXPERT_GROUPS):
            hit = gid[g] == ik
            masked[g] = jnp.where(hit, neg, masked[g])
            sel[g] = jnp.where(hit, 1.0, sel[g])
            wk = wk + jnp.where(hit, sco[g], 0.0)
        e_rows.append(ik)
        w_rows.append(jnp.sum(wk, axis=0, keepdims=True))
    wsum = w_rows[0]
    for k in range(1, TOP_K):
        wsum = wsum + w_rows[k]
    gates = [w_rows[k] / wsum * ROUTED_SCALE for k in range(TOP_K)]

    sel_t = jnp.concatenate(sel, axis=0)
    within = jnp.dot(sel_t.astype(BF16), tri_ref[...], preferred_element_type=F32)
    posn = within + carry_scr[:, 0:1]
    r_rows = []
    for k in range(TOP_K):
        acc = jnp.zeros((eg, rows), F32)
        for g in range(N_EXPERT_GROUPS):
            acc = acc + jnp.where(gid[g] == e_rows[k], posn[g * eg:(g + 1) * eg, :], 0.0)
        r_rows.append(jnp.sum(acc, axis=0, keepdims=True))
    new_carry = carry_scr[...] + jnp.sum(sel_t, axis=1, keepdims=True)
    carry_scr[...] = new_carry
    cnt_ref[...] = new_carry.astype(jnp.int32)
    eidx_ref[...] = jnp.concatenate(e_rows, axis=0)
    gate_ref[...] = jnp.concatenate(gates, axis=0)
    rank_ref[...] = jnp.concatenate(r_rows, axis=0).astype(jnp.int32)


def _post(x2, an, bn, woa, wob, g2, wrt, rb, wsgu, wsd):
    t, d = x2.shape
    rows = FRONT_ROWS
    n_experts = wrt.shape[0]
    d_shared = wsd.shape[0]
    tri = jnp.asarray(np.triu(np.ones((rows, rows), np.float32), k=1), BF16)
    const = lambda shape: pl.BlockSpec(shape, lambda i: (0,) * len(shape))
    row_blk = lambda w: pl.BlockSpec((rows, w), lambda i: (i, 0))
    tok_blk = pl.BlockSpec((TOP_K, rows), lambda i: (0, i))
    return pl.pallas_call(
        functools.partial(_post_kernel, n_experts=n_experts, d_shared=d_shared),
        grid=(t // rows,),
        in_specs=[row_blk(d), row_blk(an.shape[1]), row_blk(bn.shape[1]),
                  const(woa.shape), const(wob.shape), const((1, d)), const(wrt.shape), const(rb.shape),
                  const(wsgu.shape), const(wsd.shape), const(tri.shape)],
        out_specs=[row_blk(d), row_blk(d), tok_blk, tok_blk, tok_blk, const((n_experts, LANES))],
        out_shape=[jax.ShapeDtypeStruct((t, d), F32), jax.ShapeDtypeStruct((t, d), F32),
                   jax.ShapeDtypeStruct((TOP_K, t), jnp.int32), jax.ShapeDtypeStruct((TOP_K, t), F32),
                   jax.ShapeDtypeStruct((TOP_K, t), jnp.int32),
                   jax.ShapeDtypeStruct((n_experts, LANES), jnp.int32)],
        scratch_shapes=[pltpu.VMEM((n_experts, LANES), F32)],
        compiler_params=pltpu.CompilerParams(dimension_semantics=("arbitrary",),
                                             vmem_limit_bytes=VMEM_LIMIT),
        name="post",
    )(x2, an, bn, woa, wob, g2, wrt, rb, wsgu, wsd, tri)


def _dispatch_kernel(pend_ref, dest_ref, hn_ref, xs_hbm, zero_scr, zsem, sem, *, n_experts):
    i = pl.program_id(0)
    blk = EXPERT_ROWS
    tokens = DISPATCH_TOKENS

    def zero_copy(e):
        start = pl.multiple_of(jnp.maximum(pend_ref[e] - blk, 0), blk)
        return pltpu.make_async_copy(zero_scr, xs_hbm.at[pl.ds(start, blk)], zsem)

    @pl.when(i == 0)
    def _():
        zero_scr[...] = jnp.zeros_like(zero_scr)

        def start_body(e, c):
            zero_copy(e).start()
            return c

        def wait_body(e, c):
            zero_copy(e).wait()
            return c

        lax.fori_loop(0, n_experts, start_body, 0)
        lax.fori_loop(0, n_experts, wait_body, 0)

        def tail_copy(b):
            return pltpu.make_async_copy(zero_scr, xs_hbm.at[pl.ds(pl.multiple_of(b * blk, blk), blk)], zsem)

        def tail_start(b, c):
            tail_copy(b).start()
            return c

        def tail_wait(b, c):
            tail_copy(b).wait()
            return c

        first_free = pend_ref[n_experts - 1] // blk
        lax.fori_loop(first_free, xs_hbm.shape[0] // blk, tail_start, 0)
        lax.fori_loop(first_free, xs_hbm.shape[0] // blk, tail_wait, 0)

    def issue(j, c):
        for k in range(TOP_K):
            pltpu.make_async_copy(hn_ref.at[pl.ds(j, 1)], xs_hbm.at[pl.ds(dest_ref[j * TOP_K + k], 1)],
                                  sem).start(priority=k % 2)
        return c

    lax.fori_loop(0, tokens, issue, 0)
    for k in range(TOP_K):
        pltpu.make_async_copy(hn_ref, xs_hbm.at[pl.ds(0, tokens)], sem).wait()


def _dispatch(pad_end, dest_flat, hn, n_rows):
    t, d = hn.shape
    n_experts = pad_end.shape[0]
    tokens = DISPATCH_TOKENS
    grid_spec = pltpu.PrefetchScalarGridSpec(
        num_scalar_prefetch=1,
        grid=(t // tokens,),
        in_specs=[pl.BlockSpec((tokens * TOP_K,), lambda i, pe: (i,), memory_space=pltpu.SMEM),
                  pl.BlockSpec((tokens, d), lambda i, pe: (i, 0))],
        out_specs=pl.BlockSpec(memory_space=pl.ANY),
        scratch_shapes=[pltpu.VMEM((EXPERT_ROWS, d), F32), pltpu.SemaphoreType.DMA, pltpu.SemaphoreType.DMA],
    )
    return pl.pallas_call(
        functools.partial(_dispatch_kernel, n_experts=n_experts),
        grid_spec=grid_spec,
        out_shape=jax.ShapeDtypeStruct((n_rows, d), F32),
        compiler_params=pltpu.CompilerParams(dimension_semantics=("arbitrary",), has_side_effects=True),
        name="dispatch",
    )(pad_end, dest_flat, hn)


def _expert_kernel(be_ref, nu_ref, xs_ref, wgu_ref, wd_ref, ys_ref, *, d_expert):
    i = pl.program_id(0)

    @pl.when(i < nu_ref[0])
    def _():
        xb = xs_ref[...].astype(BF16)
        gu = jnp.dot(xb, wgu_ref[0], preferred_element_type=F32)
        act = (_silu(gu[:, :d_expert]) * gu[:, d_expert:]).astype(BF16)
        ys_ref[...] = jnp.dot(act, wd_ref[0], preferred_element_type=F32)

    @pl.when(i >= nu_ref[0])
    def _():
        ys_ref[...] = jnp.zeros_like(ys_ref)


def _experts(block_e, n_used, xs, wgu, wd):
    n_rows, d = xs.shape
    blk = EXPERT_ROWS
    d_expert = wd.shape[1]
    grid_spec = pltpu.PrefetchScalarGridSpec(
        num_scalar_prefetch=2,
        grid=(n_rows // blk,),
        in_specs=[pl.BlockSpec((blk, d), lambda i, be, nu: (jnp.minimum(i, nu[0] - 1), 0)),
                  pl.BlockSpec((1, d, 2 * d_expert), lambda i, be, nu: (be[i], 0, 0)),
                  pl.BlockSpec((1, d_expert, d), lambda i, be, nu: (be[i], 0, 0))],
        out_specs=pl.BlockSpec((blk, d), lambda i, be, nu: (i, 0)),
    )
    return pl.pallas_call(
        functools.partial(_expert_kernel, d_expert=d_expert),
        grid_spec=grid_spec,
        out_shape=jax.ShapeDtypeStruct((n_rows, d), F32),
        compiler_params=pltpu.CompilerParams(dimension_semantics=("arbitrary",),
                                             vmem_limit_bytes=VMEM_LIMIT),
        name="experts",
    )(block_e, n_used, xs, wgu, wd)


def _combine_kernel(dcur_ref, dnext_ref, gate_ref, hs_ref, gf_ref, ys_hbm, out_ref, buf, sem):
    i = pl.program_id(0)
    n = pl.num_programs(0)
    tokens = COMBINE_TOKENS
    slot = i % 2

    def row_copy(dref, s, j, k):
        return pltpu.make_async_copy(ys_hbm.at[pl.ds(dref[j * TOP_K + k], 1)],
                                     buf.at[s, k, pl.ds(j, 1)], sem.at[s])

    def issue(dref, s):
        def body(j, c):
            for k in range(TOP_K):
                row_copy(dref, s, j, k).start(priority=k % 2)
            return c
        lax.fori_loop(0, tokens, body, 0)

    @pl.when(i == 0)
    def _():
        issue(dcur_ref, 0)

    @pl.when(i + 1 < n)
    def _():
        issue(dnext_ref, 1 - slot)

    for k in range(TOP_K):
        pltpu.make_async_copy(ys_hbm.at[pl.ds(0, tokens)], buf.at[slot, k], sem.at[slot]).wait()

    acc = hs_ref[...]
    g = gate_ref[...]
    for k in range(TOP_K):
        acc = acc + g[:, k:k + 1] * buf[slot, k]
    out_ref[...] = _rms(acc, gf_ref[...])


def _combine(dest_flat, gate_tk, hs, gf, ys):
    t, d = hs.shape
    tokens = COMBINE_TOKENS
    n = t // tokens
    dspec = lambda f: pl.BlockSpec((tokens * TOP_K,), f, memory_space=pltpu.SMEM)
    return pl.pallas_call(
        _combine_kernel,
        grid=(n,),
        in_specs=[dspec(lambda i: (i,)), dspec(lambda i: (jnp.minimum(i + 1, n - 1),)),
                  pl.BlockSpec((tokens, TOP_K), lambda i: (i, 0)),
                  pl.BlockSpec((tokens, d), lambda i: (i, 0)),
                  pl.BlockSpec((1, d), lambda i: (0, 0)),
                  pl.BlockSpec(memory_space=pl.ANY)],
        out_specs=pl.BlockSpec((tokens, d), lambda i: (i, 0)),
        out_shape=jax.ShapeDtypeStruct((t, d), F32),
        scratch_shapes=[pltpu.VMEM((2, TOP_K, tokens, d), F32), pltpu.SemaphoreType.DMA((2,))],
        compiler_params=pltpu.CompilerParams(dimension_semantics=("arbitrary",),
                                             vmem_limit_bytes=VMEM_LIMIT),
        name="combine",
    )(dest_flat, dest_flat, gate_tk, hs, gf, ys)


def kernel(x, norm1_g, w_in, sgu_ln_g, sgu_ln_b, w_spatial, b_spatial, out_norm_a_g, out_norm_b_g, w_out,
           norm2_g, w_router, router_bias, w_gate, w_up, w_down, ws_gate, ws_up, ws_down, final_norm_g):
    bsz, s, d = x.shape
    depth = w_in.shape[0]
    assert depth == 1, "the fused final norm assumes a single trunk layer"
    l = 0
    t = bsz * s
    d_a = sgu_ln_g.shape[-1]
    d_b = out_norm_b_g.shape[-1]
    n_heads, chunk = w_spatial.shape[1], w_spatial.shape[2]
    n_experts = w_router.shape[-1]
    assert d_a == n_heads * A_HEAD_DIM and chunk == LANES and 2 * A_HEAD_DIM == LANES
    assert t % FRONT_ROWS == 0 and FRONT_ROWS % chunk == 0 and s % FRONT_ROWS == 0
    assert t % DISPATCH_TOKENS == 0 and t % COMBINE_TOKENS == 0
    row = lambda v: v.reshape(1, -1).astype(F32)

    x2 = x.reshape(t, d)

    win = w_in[l].astype(BF16)
    ws = w_spatial[l]
    ws_pair = jnp.concatenate([ws[0::2], ws[1::2]], axis=2).astype(BF16)
    bs_full = jnp.repeat(b_spatial[l].T, A_HEAD_DIM, axis=1).astype(F32)
    woa = w_out[l][:d_a].astype(BF16)
    wob = w_out[l][d_a:].astype(BF16)
    wrt = w_router[l].T.astype(F32)
    rb = jnp.broadcast_to(router_bias[l].astype(F32)[:, None], (n_experts, 1))
    wsgu = jnp.concatenate([ws_gate[l], ws_up[l]], axis=1).astype(BF16)
    wsd = ws_down[l].astype(BF16)
    wgu = jnp.concatenate([w_gate[l], w_up[l]], axis=2).astype(BF16)
    wd = w_down[l].astype(BF16)

    an, zb = _front(x2, row(norm1_g[l]), win, row(sgu_ln_g[l]), row(sgu_ln_b[l]), ws_pair, bs_full,
                    row(out_norm_a_g[l]), d_a=d_a, d_b=d_b, chunk=chunk)
    bn = _fnet(zb, row(out_norm_b_g[l]), bsz).reshape(t, d_b)

    hs, hn, eidx, gate, rank, cnt = _post(x2, an, bn, woa, wob, row(norm2_g[l]), wrt, rb, wsgu, wsd)

    blk = EXPERT_ROWS
    counts = cnt[:, 0]
    padded = (counts + blk - 1) // blk * blk
    pad_end = jnp.cumsum(padded).astype(jnp.int32)
    pad_start = pad_end - padded
    n_rows = t * TOP_K + n_experts * blk
    n_blocks = n_rows // blk
    e_ids = jnp.arange(n_experts, dtype=jnp.int32)[:, None, None]
    dest = jnp.sum(jnp.where(eidx[None] == e_ids, pad_start[:, None, None], 0), axis=0) + rank
    dest_flat = dest.T.reshape(-1).astype(jnp.int32)
    gate_tk = gate.T
    block_row0 = jnp.arange(n_blocks, dtype=jnp.int32) * blk
    block_e = jnp.minimum(jnp.sum((pad_end[None, :] <= block_row0[:, None]).astype(jnp.int32), axis=1),
                          n_experts - 1)
    n_used = (pad_end[-1:] // blk).astype(jnp.int32)

    xs = _dispatch(pad_end, dest_flat, hn, n_rows)
    ys = _experts(block_e, n_used, xs, wgu, wd)
    out = _combine(dest_flat, gate_tk, hs, row(final_norm_g), ys)
    return out.reshape(bsz, s, d)
```

```python
import functools
import math

import numpy as np
import jax
import jax.numpy as jnp
from jax import lax
from jax.experimental import pallas as pl
from jax.experimental.pallas import tpu as pltpu

A_HEAD_DIM = 64
B_GROUP_DIM = 64
TOP_K = 8
N_EXPERT_GROUPS = 8
TOPK_GROUPS = 4
ROUTED_SCALE = 2.5
EPS = 1e-6

LANES = 128
FRONT_ROWS = 512
EXPERT_ROWS = 256
DISPATCH_TOKENS = 512
COMBINE_TOKENS = 128
GROUP_TOKENS = 2048
FFT_RADIX = 64
VMEM_LIMIT = 56 * 1024 * 1024

F32 = jnp.float32
BF16 = jnp.bfloat16


def _rms(x, g):
    return x * lax.rsqrt(jnp.mean(x * x, axis=-1, keepdims=True) + EPS) * g


def _gelu(x):
    return 0.5 * x * (1.0 + lax.erf(x * (1.0 / math.sqrt(2.0))))


def _silu(x):
    return x / (1.0 + jnp.exp(-x))


def _front_kernel(x_ref, g1_ref, win_ref, lng_ref, lnb_ref, ws_ref, bs_ref, ga_ref,
                  an_ref, zb_ref, ya_scr, *, d_a, chunk):
    x = x_ref[...]
    xn = _rms(x, g1_ref[...]).astype(BF16)
    z = jnp.dot(xn, win_ref[...], preferred_element_type=F32)
    for q in range(zb_ref.shape[0]):
        zb_ref[q] = z[:, 2 * d_a + q * LANES:2 * d_a + (q + 1) * LANES]
    u = _gelu(z[:, :d_a])
    v = _gelu(z[:, d_a:2 * d_a])
    mu = jnp.mean(v, axis=-1, keepdims=True)
    vc = v - mu
    v = vc * lax.rsqrt(jnp.mean(vc * vc, axis=-1, keepdims=True) + EPS) * lng_ref[...] + lnb_ref[...]
    rows = x.shape[0]
    lane = lax.broadcasted_iota(jnp.int32, (chunk, LANES), 1)
    low = lane < A_HEAD_DIM
    for c in range(rows // chunk):
        r0 = c * chunk
        for hp in range(d_a // LANES):
            c0 = hp * LANES
            vp = v[r0:r0 + chunk, c0:c0 + LANES]
            rhs = jnp.concatenate([jnp.where(low, vp, 0.0), jnp.where(low, 0.0, vp)], axis=0).astype(BF16)
            mixed = jnp.dot(ws_ref[hp], rhs, preferred_element_type=F32) + bs_ref[:, c0:c0 + LANES]
            ya_scr[r0:r0 + chunk, c0:c0 + LANES] = u[r0:r0 + chunk, c0:c0 + LANES] * mixed
    an_ref[...] = _rms(ya_scr[...], ga_ref[...]).astype(BF16)


def _front(x2, g1, win, lng, lnb, ws_pair, bs_full, ga, *, d_a, d_b, chunk):
    t, d = x2.shape
    d_in = win.shape[1]
    rows = FRONT_ROWS
    const = lambda shape: pl.BlockSpec(shape, lambda i: (0,) * len(shape))
    return pl.pallas_call(
        functools.partial(_front_kernel, d_a=d_a, chunk=chunk),
        grid=(t // rows,),
        in_specs=[
            pl.BlockSpec((rows, d), lambda i: (i, 0)),
            const((1, d)), const((d, d_in)), const((1, d_a)), const((1, d_a)),
            const(ws_pair.shape), const(bs_full.shape), const((1, d_a)),
        ],
        out_specs=[pl.BlockSpec((rows, d_a), lambda i: (i, 0)),
                   pl.BlockSpec((d_b // LANES, rows, LANES), lambda i: (0, i, 0))],
        out_shape=[jax.ShapeDtypeStruct((t, d_a), BF16), jax.ShapeDtypeStruct((d_b // LANES, t, LANES), F32)],
        scratch_shapes=[pltpu.VMEM((rows, d_a), F32)],
        compiler_params=pltpu.CompilerParams(dimension_semantics=("parallel",),
                                             vmem_limit_bytes=VMEM_LIMIT),
        name="front",
    )(x2, g1, win, lng, lnb, ws_pair, bs_full, ga)


def _fnet_tables(seq, half_w):
    r = FFT_RADIX
    assert seq == r * r
    j = np.arange(B_GROUP_DIM)
    ang = 2.0 * np.pi * np.outer(j, j) / B_GROUP_DIM
    ng = half_w // B_GROUP_DIM
    eye = np.eye(ng)
    m_ch = np.concatenate([np.kron(eye, np.cos(ang)), np.kron(eye, np.sin(ang))], axis=1)
    k1 = np.arange(r)[:, None]
    n1 = np.arange(r)[None, :]
    d_tw = np.zeros((r, 2 * r, 2 * r))
    for n2 in range(r):
        th = 2.0 * np.pi * (k1 * n1 / r + k1 * n2 / (r * r))
        c, s = np.cos(th), np.sin(th)
        d_tw[n2] = np.block([[c, -s], [s, c]])
    ph = 2.0 * np.pi * np.outer(np.arange(r), np.arange(r)) / r
    scale = 1.0 / math.sqrt(seq * B_GROUP_DIM)
    d2 = np.concatenate([np.cos(ph), -np.sin(ph)], axis=1) * scale
    return (jnp.asarray(m_ch, F32), jnp.asarray(d_tw, F32), jnp.asarray(d2, F32))


def _fnet_kernel(z_ref, mch32_ref, dtw_ref, d232_ref, gb_ref, out_ref, a2_scr, y_scr, mch_ref, d2_ref,
                 *, n_col, unroll):
    r = FFT_RADIX
    hf = pl.program_id(1)
    half_w = n_col * LANES
    mch_ref[...] = mch32_ref[...].astype(BF16)
    d2_ref[...] = d232_ref[...].astype(BF16)

    def stage1(it, carry):
        for u in range(unroll):
            n2 = it * unroll + u
            zs = jnp.concatenate([z_ref[q, pl.ds(n2, r, stride=r), :] for q in range(n_col)],
                                 axis=1).astype(BF16)
            w = jnp.dot(zs, mch_ref[...], preferred_element_type=F32)
            rhs = jnp.concatenate([w[:, :half_w], w[:, half_w:]], axis=0).astype(BF16)
            a = jnp.dot(dtw_ref[n2].astype(BF16), rhs, preferred_element_type=F32)
            for q in range(n_col):
                a2_scr[q, pl.ds(n2, r, stride=2 * r), :] = a[:r, q * LANES:(q + 1) * LANES]
                a2_scr[q, pl.ds(r + n2, r, stride=2 * r), :] = a[r:, q * LANES:(q + 1) * LANES]
        return carry

    lax.fori_loop(0, r // unroll, stage1, 0)

    def stage2(it, carry):
        for u in range(unroll):
            k1 = it * unroll + u
            row0 = pl.multiple_of(k1 * 2 * r, 2 * r)
            blk = jnp.concatenate([a2_scr[q, pl.ds(row0, 2 * r), :] for q in range(n_col)],
                                  axis=1).astype(BF16)
            y = jnp.dot(d2_ref[...], blk, preferred_element_type=F32)
            for q in range(n_col):
                y_scr[hf * n_col + q, pl.ds(k1, r, stride=r), :] = y[:, q * LANES:(q + 1) * LANES]
        return carry

    lax.fori_loop(0, r // unroll, stage2, 0)

    @pl.when(hf == pl.num_programs(1) - 1)
    def _():
        rows = 512
        n_all = y_scr.shape[0]
        for c in range(y_scr.shape[1] // rows):
            ys = [y_scr[q, c * rows:(c + 1) * rows, :] for q in range(n_all)]
            ss = sum(jnp.sum(y * y, axis=-1, keepdims=True) for y in ys)
            inv = lax.rsqrt(ss * (1.0 / (n_all * LANES)) + EPS)
            for q in range(n_all):
                out_ref[0, c * rows:(c + 1) * rows, q * LANES:(q + 1) * LANES] = (
                    ys[q] * inv * gb_ref[:, q * LANES:(q + 1) * LANES]).astype(BF16)


def _fnet(zb_cols, gb, bsz):
    n_all, t, _ = zb_cols.shape
    s = t // bsz
    d_b = n_all * LANES
    n_col = 2
    r = FFT_RADIX
    m_ch, d_tw, d2 = _fnet_tables(s, n_col * LANES)
    return pl.pallas_call(
        functools.partial(_fnet_kernel, n_col=n_col, unroll=4),
        grid=(bsz, n_all // n_col),
        in_specs=[
            pl.BlockSpec((n_col, s, LANES), lambda i, h: (h, i, 0)),
            pl.BlockSpec(m_ch.shape, lambda i, h: (0, 0)),
            pl.BlockSpec(d_tw.shape, lambda i, h: (0, 0, 0)),
            pl.BlockSpec(d2.shape, lambda i, h: (0, 0)),
            pl.BlockSpec((1, d_b), lambda i, h: (0, 0)),
        ],
        out_specs=pl.BlockSpec((1, s, d_b), lambda i, h: (i, 0, 0)),
        out_shape=jax.ShapeDtypeStruct((bsz, s, d_b), BF16),
        scratch_shapes=[pltpu.VMEM((n_col, r * 2 * r, LANES), F32), pltpu.VMEM((n_all, s, LANES), F32),
                        pltpu.VMEM(m_ch.shape, BF16), pltpu.VMEM(d2.shape, BF16)],
        compiler_params=pltpu.CompilerParams(dimension_semantics=("parallel", "arbitrary"),
                                             vmem_limit_bytes=VMEM_LIMIT),
        name="fnet",
    )(zb_cols, m_ch, d_tw, d2, gb)


def _post_kernel(x_ref, an_ref, bn_ref, woa_ref, wob_ref, g2_ref, wrt_ref, rb_ref, wsgu_ref, wsd_ref,
                 tri_ref, hs_ref, hn_ref, eidx_ref, gate_ref, rank_ref, cnt_ref, carry_scr,
                 *, n_experts, d_shared, steps_per_group):
    i = pl.program_id(0)

    @pl.when(i % steps_per_group == 0)
    def _():
        carry_scr[...] = jnp.zeros_like(carry_scr)

    h = (x_ref[...]
         + jnp.dot(an_ref[...], woa_ref[...], preferred_element_type=F32)
         + jnp.dot(bn_ref[...], wob_ref[...], preferred_element_type=F32))
    hn = _rms(h, g2_ref[...])
    hnb = hn.astype(BF16)
    gu = jnp.dot(hnb, wsgu_ref[...], preferred_element_type=F32)
    act = (_silu(gu[:, :d_shared]) * gu[:, d_shared:]).astype(BF16)
    hs = h + jnp.dot(act, wsd_ref[...], preferred_element_type=F32)
    n_slab = hn.shape[1] // LANES
    for s in range(n_slab):
        hn_ref[pl.ds(s, hn.shape[0], stride=n_slab), :] = hn[:, s * LANES:(s + 1) * LANES]
        hs_ref[pl.ds(s, hn.shape[0], stride=n_slab), :] = hs[:, s * LANES:(s + 1) * LANES]

    rows = hn.shape[0]
    eg = n_experts // N_EXPERT_GROUPS
    logits_t = lax.dot_general(wrt_ref[...], hn, (((1,), (1,)), ((), ())),
                               precision=lax.Precision.HIGHEST, preferred_element_type=F32)
    scores_t = 1.0 / (1.0 + jnp.exp(-logits_t))
    biased_t = scores_t + rb_ref[...]
    sub = lax.broadcasted_iota(jnp.int32, (eg, rows), 0)
    neg = -jnp.inf
    grp = [biased_t[g * eg:(g + 1) * eg, :] for g in range(N_EXPERT_GROUPS)]
    sco = [scores_t[g * eg:(g + 1) * eg, :] for g in range(N_EXPERT_GROUPS)]
    gid = [sub + g * eg for g in range(N_EXPERT_GROUPS)]

    gscore = []
    for g in range(N_EXPERT_GROUPS):
        m1 = jnp.max(grp[g], axis=0, keepdims=True)
        first = jnp.min(jnp.where(grp[g] == m1, sub, eg), axis=0, keepdims=True)
        m2 = jnp.max(jnp.where(sub == first, neg, grp[g]), axis=0, keepdims=True)
        gscore.append(m1 + m2)
    masked = []
    for g in range(N_EXPERT_GROUPS):
        beaten = jnp.zeros((1, rows), jnp.int32)
        for j in range(N_EXPERT_GROUPS):
            if j == g:
                continue
            wins = (gscore[j] > gscore[g]) if j > g else (gscore[j] >= gscore[g])
            beaten = beaten + wins.astype(jnp.int32)
        masked.append(jnp.where(beaten < TOPK_GROUPS, grp[g], neg))

    sel = [jnp.zeros((eg, rows), F32) for _ in range(N_EXPERT_GROUPS)]
    e_rows, w_rows = [], []
    for _ in range(TOP_K):
        m = masked[0]
        for g in range(1, N_EXPERT_GROUPS):
            m = jnp.maximum(m, masked[g])
        mk = jnp.max(m, axis=0, keepdims=True)
        cand = jnp.where(masked[0] == mk, gid[0], n_experts)
        for g in range(1, N_EXPERT_GROUPS):
            cand = jnp.minimum(cand, jnp.where(masked[g] == mk, gid[g], n_experts))
        ik = jnp.min(cand, axis=0, keepdims=True)
        wk = jnp.zeros((eg, rows), F32)
        for g in range(N_EXPERT_GROUPS):
            hit = gid[g] == ik
            masked[g] = jnp.where(hit, neg, masked[g])
            sel[g] = jnp.where(hit, 1.0, sel[g])
            wk = wk + jnp.where(hit, sco[g], 0.0)
        e_rows.append(ik)
        w_rows.append(jnp.sum(wk, axis=0, keepdims=True))
    wsum = w_rows[0]
    for k in range(1, TOP_K):
        wsum = wsum + w_rows[k]
    gates = [w_rows[k] / wsum * ROUTED_SCALE for k in range(TOP_K)]

    sel_t = jnp.concatenate(sel, axis=0)
    within = jnp.dot(sel_t.astype(BF16), tri_ref[...], preferred_element_type=F32)
    posn = within + carry_scr[:, 0:1]
    r_rows = []
    for k in range(TOP_K):
        acc = jnp.zeros((eg, rows), F32)
        for g in range(N_EXPERT_GROUPS):
            acc = acc + jnp.where(gid[g] == e_rows[k], posn[g * eg:(g + 1) * eg, :], 0.0)
        r_rows.append(jnp.sum(acc, axis=0, keepdims=True))
    new_carry = carry_scr[...] + jnp.sum(sel_t, axis=1, keepdims=True)
    carry_scr[...] = new_carry
    cnt_ref[0] = new_carry.astype(jnp.int32)
    eidx_ref[...] = jnp.concatenate(e_rows, axis=0)
    gate_ref[...] = jnp.concatenate(gates, axis=0)
    rank_ref[...] = jnp.concatenate(r_rows, axis=0).astype(jnp.int32)


def _post(x2, an, bn, woa, wob, g2, wrt, rb, wsgu, wsd):
    t, d = x2.shape
    rows = FRONT_ROWS
    n_experts = wrt.shape[0]
    d_shared = wsd.shape[0]
    n_slab = d // LANES
    steps_per_group = GROUP_TOKENS // rows
    n_groups = t // GROUP_TOKENS
    tri = jnp.asarray(np.triu(np.ones((rows, rows), np.float32), k=1), BF16)
    const = lambda shape: pl.BlockSpec(shape, lambda i: (0,) * len(shape))
    row_blk = lambda w: pl.BlockSpec((rows, w), lambda i: (i, 0))
    slab_blk = pl.BlockSpec((rows * n_slab, LANES), lambda i: (i, 0))
    tok_blk = pl.BlockSpec((TOP_K, rows), lambda i: (0, i))
    return pl.pallas_call(
        functools.partial(_post_kernel, n_experts=n_experts, d_shared=d_shared,
                          steps_per_group=steps_per_group),
        grid=(t // rows,),
        in_specs=[row_blk(d), row_blk(an.shape[1]), row_blk(bn.shape[1]),
                  const(woa.shape), const(wob.shape), const((1, d)), const(wrt.shape), const(rb.shape),
                  const(wsgu.shape), const(wsd.shape), const(tri.shape)],
        out_specs=[slab_blk, slab_blk, tok_blk, tok_blk, tok_blk,
                   pl.BlockSpec((1, n_experts, LANES), lambda i: (i // steps_per_group, 0, 0))],
        out_shape=[jax.ShapeDtypeStruct((t * n_slab, LANES), F32), jax.ShapeDtypeStruct((t * n_slab, LANES), F32),
                   jax.ShapeDtypeStruct((TOP_K, t), jnp.int32), jax.ShapeDtypeStruct((TOP_K, t), F32),
                   jax.ShapeDtypeStruct((TOP_K, t), jnp.int32),
                   jax.ShapeDtypeStruct((n_groups, n_experts, LANES), jnp.int32)],
        scratch_shapes=[pltpu.VMEM((n_experts, LANES), F32)],
        compiler_params=pltpu.CompilerParams(dimension_semantics=("arbitrary",),
                                             vmem_limit_bytes=VMEM_LIMIT),
        name="post",
    )(x2, an, bn, woa, wob, g2, wrt, rb, wsgu, wsd, tri)


def _dispatch_kernel(pend_ref, dest_ref, hn_ref, xs_hbm, zero_scr, zsem, sem, *, n_experts):
    i = pl.program_id(0)
    blk = EXPERT_ROWS
    tokens = DISPATCH_TOKENS

    def zero_copy(e):
        start = pl.multiple_of(jnp.maximum(pend_ref[e] - blk, 0), blk)
        return pltpu.make_async_copy(zero_scr, xs_hbm.at[pl.ds(start, blk)], zsem)

    @pl.when(i == 0)
    def _():
        zero_scr[...] = jnp.zeros_like(zero_scr)

        def start_body(e, c):
            zero_copy(e).start()
            return c

        def wait_body(e, c):
            zero_copy(e).wait()
            return c

        lax.fori_loop(0, n_experts, start_body, 0)
        lax.fori_loop(0, n_experts, wait_body, 0)

        def tail_copy(b):
            return pltpu.make_async_copy(zero_scr, xs_hbm.at[pl.ds(pl.multiple_of(b * blk, blk), blk)], zsem)

        def tail_start(b, c):
            tail_copy(b).start()
            return c

        def tail_wait(b, c):
            tail_copy(b).wait()
            return c

        first_free = pend_ref[n_experts - 1] // blk
        lax.fori_loop(first_free, xs_hbm.shape[0] // blk, tail_start, 0)
        lax.fori_loop(first_free, xs_hbm.shape[0] // blk, tail_wait, 0)

    def issue(j, c):
        for k in range(TOP_K):
            pltpu.make_async_copy(hn_ref.at[pl.ds(j, 1)], xs_hbm.at[pl.ds(dest_ref[j * TOP_K + k], 1)],
                                  sem).start(priority=k % 2)
        return c

    lax.fori_loop(0, tokens, issue, 0)
    for k in range(TOP_K):
        pltpu.make_async_copy(hn_ref, xs_hbm.at[pl.ds(0, tokens)], sem).wait()


def _dispatch(pad_end, dest_flat, hn, n_rows):
    t, d = hn.shape
    n_experts = pad_end.shape[0]
    tokens = DISPATCH_TOKENS
    grid_spec = pltpu.PrefetchScalarGridSpec(
        num_scalar_prefetch=1,
        grid=(t // tokens,),
        in_specs=[pl.BlockSpec((tokens * TOP_K,), lambda i, pe: (i,), memory_space=pltpu.SMEM),
                  pl.BlockSpec((tokens, d), lambda i, pe: (i, 0))],
        out_specs=pl.BlockSpec(memory_space=pl.ANY),
        scratch_shapes=[pltpu.VMEM((EXPERT_ROWS, d), F32), pltpu.SemaphoreType.DMA, pltpu.SemaphoreType.DMA],
    )
    return pl.pallas_call(
        functools.partial(_dispatch_kernel, n_experts=n_experts),
        grid_spec=grid_spec,
        out_shape=jax.ShapeDtypeStruct((n_rows, d), F32),
        compiler_params=pltpu.CompilerParams(dimension_semantics=("arbitrary",), has_side_effects=True,
                                             disable_bounds_checks=True),
        name="dispatch",
    )(pad_end, dest_flat, hn)


def _expert_kernel(be_ref, nu_ref, xs_ref, wgu_ref, wd_ref, ys_ref, *, d_expert):
    i = pl.program_id(0)

    @pl.when(i < nu_ref[0])
    def _():
        xb = xs_ref[...].astype(BF16)
        gu = jnp.dot(xb, wgu_ref[0], preferred_element_type=F32)
        act = (_silu(gu[:, :d_expert]) * gu[:, d_expert:]).astype(BF16)
        ys_ref[...] = jnp.dot(act, wd_ref[0], preferred_element_type=F32)

    @pl.when(i >= nu_ref[0])
    def _():
        ys_ref[...] = jnp.zeros_like(ys_ref)


def _experts(block_e, n_used, xs, wgu, wd):
    n_rows, d = xs.shape
    blk = EXPERT_ROWS
    d_expert = wd.shape[1]
    grid_spec = pltpu.PrefetchScalarGridSpec(
        num_scalar_prefetch=2,
        grid=(n_rows // blk,),
        in_specs=[pl.BlockSpec((blk, d), lambda i, be, nu: (jnp.minimum(i, nu[0] - 1), 0)),
                  pl.BlockSpec((1, d, 2 * d_expert), lambda i, be, nu: (be[i], 0, 0)),
                  pl.BlockSpec((1, d_expert, d), lambda i, be, nu: (be[i], 0, 0))],
        out_specs=pl.BlockSpec((blk, d), lambda i, be, nu: (i, 0)),
    )
    return pl.pallas_call(
        functools.partial(_expert_kernel, d_expert=d_expert),
        grid_spec=grid_spec,
        out_shape=jax.ShapeDtypeStruct((n_rows, d), F32),
        compiler_params=pltpu.CompilerParams(dimension_semantics=("arbitrary",),
                                             vmem_limit_bytes=VMEM_LIMIT),
        name="experts",
    )(block_e, n_used, xs, wgu, wd)


def _combine_kernel(dcur_ref, dnext_ref, gate_ref, hs_ref, gf_ref, ys_hbm, out_ref, buf, sem):
    i = pl.program_id(0)
    n = pl.num_programs(0)
    tokens = COMBINE_TOKENS
    slot = i % 2

    def row_copy(dref, s, j, k):
        return pltpu.make_async_copy(ys_hbm.at[pl.ds(dref[j * TOP_K + k], 1)],
                                     buf.at[s, k, pl.ds(j, 1)], sem.at[s])

    def issue(dref, s):
        def body(j, c):
            for k in range(TOP_K):
                row_copy(dref, s, j, k).start(priority=k % 2)
            return c
        lax.fori_loop(0, tokens, body, 0)

    @pl.when(i == 0)
    def _():
        issue(dcur_ref, 0)

    @pl.when(i + 1 < n)
    def _():
        issue(dnext_ref, 1 - slot)

    for k in range(TOP_K):
        pltpu.make_async_copy(ys_hbm.at[pl.ds(0, tokens)], buf.at[slot, k], sem.at[slot]).wait()

    acc = hs_ref[...]
    g = gate_ref[...]
    for k in range(TOP_K):
        acc = acc + g[:, k:k + 1] * buf[slot, k]
    out_ref[...] = _rms(acc, gf_ref[...])


def _combine(dest_flat, gate_tk, hs, gf, ys):
    t, d = hs.shape
    tokens = COMBINE_TOKENS
    n = t // tokens
    dspec = lambda f: pl.BlockSpec((tokens * TOP_K,), f, memory_space=pltpu.SMEM)
    return pl.pallas_call(
        _combine_kernel,
        grid=(n,),
        in_specs=[dspec(lambda i: (i,)), dspec(lambda i: (jnp.minimum(i + 1, n - 1),)),
                  pl.BlockSpec((tokens, TOP_K), lambda i: (i, 0)),
                  pl.BlockSpec((tokens, d), lambda i: (i, 0)),
                  pl.BlockSpec((1, d), lambda i: (0, 0)),
                  pl.BlockSpec(memory_space=pl.ANY)],
        out_specs=pl.BlockSpec((tokens, d), lambda i: (i, 0)),
        out_shape=jax.ShapeDtypeStruct((t, d), F32),
        scratch_shapes=[pltpu.VMEM((2, TOP_K, tokens, d), F32), pltpu.SemaphoreType.DMA((2,))],
        compiler_params=pltpu.CompilerParams(dimension_semantics=("arbitrary",),
                                             vmem_limit_bytes=VMEM_LIMIT, disable_bounds_checks=True),
        name="combine",
    )(dest_flat, dest_flat, gate_tk, hs, gf, ys)


def _moe_kernel(cnt_ref, off_ref, dest_ref, gate_ref, hn_ref, hs_hbm, wgu_ref, wd_ref, gf_ref,
                out_ref, acc_scr, x_scr, y_scr, list_scr, sem, *, d_expert, n_slab):
    g = pl.program_id(0)
    e = pl.program_id(1)
    n_e = pl.num_programs(1)
    tokens = GROUP_TOKENS
    blk = EXPERT_ROWS
    unroll = 8
    dummy = tokens

    def slab(ref, idx):
        return ref.at[pl.ds(pl.multiple_of(idx * n_slab, n_slab), n_slab), :]

    def acc_init():
        return pltpu.make_async_copy(hs_hbm.at[pl.ds(pl.multiple_of(g * tokens * n_slab, n_slab), tokens * n_slab), :],
                                     acc_scr.at[pl.ds(0, tokens * n_slab), :], sem)

    @pl.when(e == 0)
    def _():
        acc_init().start()
        slab(acc_scr, dummy)[...] = jnp.zeros((n_slab, LANES), F32)

        @pl.when(g == 0)
        def _():
            x_scr[...] = jnp.zeros_like(x_scr)

        def build(j, c):
            for u in range(unroll):
                a = j * unroll + u
                list_scr[dest_ref[a]] = a
            return c

        lax.fori_loop(0, tokens * TOP_K // unroll, build, 0)
        acc_init().wait()

    n = cnt_ref[g * n_e + e]
    off = off_ref[g * n_e + e]

    def block(b, carry):
        base = off + b * blk
        n_valid = jnp.minimum(n - b * blk, blk)
        last = base + n_valid - 1

        def gather(j, c):
            for u in range(unroll):
                r = j * unroll + u
                a = list_scr[jnp.minimum(base + r, last)]
                tok = lax.shift_right_logical(a, 3)
                slab(x_scr, r)[...] = slab(hn_ref, tok)[...]
            return c

        lax.fori_loop(0, blk // unroll, gather, 0)

        xb = jnp.concatenate([x_scr[pl.ds(s, blk, stride=n_slab), :] for s in range(n_slab)], axis=1).astype(BF16)
        gu = jnp.dot(xb, wgu_ref[0], preferred_element_type=F32)
        act = (_silu(gu[:, :d_expert]) * gu[:, d_expert:]).astype(BF16)
        y = jnp.dot(act, wd_ref[0], preferred_element_type=F32)
        for s in range(n_slab):
            y_scr[pl.ds(s, blk, stride=n_slab), :] = y[:, s * LANES:(s + 1) * LANES]

        def scatter(j, c):
            toks, vals = [], []
            for u in range(unroll):
                r = j * unroll + u
                valid = r < n_valid
                a = list_scr[jnp.minimum(base + r, last)]
                tok = jnp.where(valid, lax.shift_right_logical(a, 3), dummy)
                gt = jnp.where(valid, gate_ref[a], 0.0)
                toks.append(tok)
                vals.append(slab(acc_scr, tok)[...] + gt * slab(y_scr, r)[...])
            for u in range(unroll):
                slab(acc_scr, toks[u])[...] = vals[u]
            return c

        lax.fori_loop(0, blk // unroll, scatter, 0)
        return carry

    lax.fori_loop(0, (n + blk - 1) // blk, block, 0)

    @pl.when(e == n_e - 1)
    def _():
        rows = 256
        for c in range(tokens // rows):
            acc = jnp.concatenate([acc_scr[pl.ds(c * rows * n_slab + s, rows, stride=n_slab), :]
                                   for s in range(n_slab)], axis=1)
            out_ref[c * rows:(c + 1) * rows, :] = _rms(acc, gf_ref[...])


def _moe(cnt_flat, off_flat, dest_flat, gate_flat, hn3, hs3, wgu, wd, gf):
    n_slab = gf.shape[1] // LANES
    t = hn3.shape[0] // n_slab
    d = gf.shape[1]
    n_experts, d_expert = wd.shape[0], wd.shape[1]
    tokens = GROUP_TOKENS
    n_groups = t // tokens
    smem_blk = pl.BlockSpec((tokens * TOP_K,), lambda g, e, cnt, off: (g,), memory_space=pltpu.SMEM)
    grid_spec = pltpu.PrefetchScalarGridSpec(
        num_scalar_prefetch=2,
        grid=(n_groups, n_experts),
        in_specs=[smem_blk, smem_blk,
                  pl.BlockSpec((tokens * n_slab, LANES), lambda g, e, cnt, off: (g, 0)),
                  pl.BlockSpec(memory_space=pl.ANY),
                  pl.BlockSpec((1, d, 2 * d_expert), lambda g, e, cnt, off: (e, 0, 0)),
                  pl.BlockSpec((1, d_expert, d), lambda g, e, cnt, off: (e, 0, 0)),
                  pl.BlockSpec((1, d), lambda g, e, cnt, off: (0, 0))],
        out_specs=pl.BlockSpec((tokens, d), lambda g, e, cnt, off: (g, 0)),
        scratch_shapes=[pltpu.VMEM(((tokens + 1) * n_slab, LANES), F32),
                        pltpu.VMEM((EXPERT_ROWS * n_slab, LANES), F32),
                        pltpu.VMEM((EXPERT_ROWS * n_slab, LANES), F32),
                        pltpu.SMEM((tokens * TOP_K,), jnp.int32),
                        pltpu.SemaphoreType.DMA],
    )
    return pl.pallas_call(
        functools.partial(_moe_kernel, d_expert=d_expert, n_slab=n_slab),
        grid_spec=grid_spec,
        out_shape=jax.ShapeDtypeStruct((t, d), F32),
        compiler_params=pltpu.CompilerParams(dimension_semantics=("arbitrary", "arbitrary"),
                                             vmem_limit_bytes=VMEM_LIMIT),
        name="moe",
    )(cnt_flat, off_flat, dest_flat, gate_flat, hn3, hs3, wgu, wd, gf)


def kernel(x, norm1_g, w_in, sgu_ln_g, sgu_ln_b, w_spatial, b_spatial, out_norm_a_g, out_norm_b_g, w_out,
           norm2_g, w_router, router_bias, w_gate, w_up, w_down, ws_gate, ws_up, ws_down, final_norm_g):
    bsz, s, d = x.shape
    depth = w_in.shape[0]
    assert depth == 1, "the fused final norm assumes a single trunk layer"
    l = 0
    t = bsz * s
    d_a = sgu_ln_g.shape[-1]
    d_b = out_norm_b_g.shape[-1]
    n_heads, chunk = w_spatial.shape[1], w_spatial.shape[2]
    n_experts = w_router.shape[-1]
    assert d_a == n_heads * A_HEAD_DIM and chunk == LANES and 2 * A_HEAD_DIM == LANES
    assert t % FRONT_ROWS == 0 and FRONT_ROWS % chunk == 0 and s % FRONT_ROWS == 0
    assert t % DISPATCH_TOKENS == 0 and t % COMBINE_TOKENS == 0
    row = lambda v: v.reshape(1, -1).astype(F32)

    x2 = x.reshape(t, d)

    win = w_in[l].astype(BF16)
    ws = w_spatial[l]
    ws_pair = jnp.concatenate([ws[0::2], ws[1::2]], axis=2).astype(BF16)
    bs_full = jnp.repeat(b_spatial[l].T, A_HEAD_DIM, axis=1).astype(F32)
    woa = w_out[l][:d_a].astype(BF16)
    wob = w_out[l][d_a:].astype(BF16)
    wrt = w_router[l].T.astype(F32)
    rb = jnp.broadcast_to(router_bias[l].astype(F32)[:, None], (n_experts, 1))
    wsgu = jnp.concatenate([ws_gate[l], ws_up[l]], axis=1).astype(BF16)
    wsd = ws_down[l].astype(BF16)
    wgu = jnp.concatenate([w_gate[l], w_up[l]], axis=2).astype(BF16)
    wd = w_down[l].astype(BF16)

    an, zb = _front(x2, row(norm1_g[l]), win, row(sgu_ln_g[l]), row(sgu_ln_b[l]), ws_pair, bs_full,
                    row(out_norm_a_g[l]), d_a=d_a, d_b=d_b, chunk=chunk)
    bn = _fnet(zb, row(out_norm_b_g[l]), bsz).reshape(t, d_b)

    hs3, hn3, eidx, gate, rank, cnt = _post(x2, an, bn, woa, wob, row(norm2_g[l]), wrt, rb, wsgu, wsd)

    n_groups = t // GROUP_TOKENS
    counts = cnt[:, :, 0]
    offs = jnp.cumsum(counts, axis=1) - counts
    eidx3 = eidx.reshape(TOP_K, n_groups, GROUP_TOKENS)
    e_ids = jnp.arange(n_experts, dtype=jnp.int32)[:, None, None, None]
    dest = jnp.sum(jnp.where(eidx3[None] == e_ids, offs.T[:, None, :, None], 0), axis=0) \
        + rank.reshape(TOP_K, n_groups, GROUP_TOKENS)
    dest_flat = dest.transpose(1, 2, 0).reshape(-1).astype(jnp.int32)
    gate_flat = gate.T.reshape(-1)

    out = _moe(counts.reshape(-1).astype(jnp.int32), offs.reshape(-1).astype(jnp.int32), dest_flat, gate_flat,
               hn3, hs3, wgu, wd, row(final_norm_g))
    return out.reshape(bsz, s, d)
```

```python
import functools
import math

import numpy as np
import jax
import jax.numpy as jnp
from jax import lax
from jax.experimental import pallas as pl
from jax.experimental.pallas import tpu as pltpu

A_HEAD_DIM = 64
B_GROUP_DIM = 64
TOP_K = 8
N_EXPERT_GROUPS = 8
TOPK_GROUPS = 4
ROUTED_SCALE = 2.5
EPS = 1e-6

LANES = 128
FRONT_ROWS = 512
EXPERT_ROWS = 128
DISPATCH_TOKENS = 512
COMBINE_TOKENS = 128
GROUP_TOKENS = 2048
FFT_RADIX = 64
VMEM_LIMIT = 56 * 1024 * 1024

F32 = jnp.float32
BF16 = jnp.bfloat16


def _rms(x, g):
    return x * lax.rsqrt(jnp.mean(x * x, axis=-1, keepdims=True) + EPS) * g


def _gelu(x):
    return 0.5 * x * (1.0 + lax.erf(x * (1.0 / math.sqrt(2.0))))


def _silu(x):
    return x / (1.0 + jnp.exp(-x))


def _front_kernel(x_ref, g1_ref, win_ref, lng_ref, lnb_ref, ws_ref, bs_ref, ga_ref,
                  an_ref, zb_ref, ya_scr, *, d_a, chunk):
    x = x_ref[...]
    xn = _rms(x, g1_ref[...]).astype(BF16)
    z = jnp.dot(xn, win_ref[...], preferred_element_type=F32)
    for q in range(zb_ref.shape[0]):
        zb_ref[q] = z[:, 2 * d_a + q * LANES:2 * d_a + (q + 1) * LANES]
    u = _gelu(z[:, :d_a])
    v = _gelu(z[:, d_a:2 * d_a])
    mu = jnp.mean(v, axis=-1, keepdims=True)
    vc = v - mu
    v = vc * lax.rsqrt(jnp.mean(vc * vc, axis=-1, keepdims=True) + EPS) * lng_ref[...] + lnb_ref[...]
    rows = x.shape[0]
    lane = lax.broadcasted_iota(jnp.int32, (chunk, LANES), 1)
    low = lane < A_HEAD_DIM
    for c in range(rows // chunk):
        r0 = c * chunk
        for hp in range(d_a // LANES):
            c0 = hp * LANES
            vp = v[r0:r0 + chunk, c0:c0 + LANES]
            rhs = jnp.concatenate([jnp.where(low, vp, 0.0), jnp.where(low, 0.0, vp)], axis=0).astype(BF16)
            mixed = jnp.dot(ws_ref[hp], rhs, preferred_element_type=F32) + bs_ref[:, c0:c0 + LANES]
            ya_scr[r0:r0 + chunk, c0:c0 + LANES] = u[r0:r0 + chunk, c0:c0 + LANES] * mixed
    an_ref[...] = _rms(ya_scr[...], ga_ref[...]).astype(BF16)


def _front(x2, g1, win, lng, lnb, ws_pair, bs_full, ga, *, d_a, d_b, chunk):
    t, d = x2.shape
    d_in = win.shape[1]
    rows = FRONT_ROWS
    const = lambda shape: pl.BlockSpec(shape, lambda i: (0,) * len(shape))
    return pl.pallas_call(
        functools.partial(_front_kernel, d_a=d_a, chunk=chunk),
        grid=(t // rows,),
        in_specs=[
            pl.BlockSpec((rows, d), lambda i: (i, 0)),
            const((1, d)), const((d, d_in)), const((1, d_a)), const((1, d_a)),
            const(ws_pair.shape), const(bs_full.shape), const((1, d_a)),
        ],
        out_specs=[pl.BlockSpec((rows, d_a), lambda i: (i, 0)),
                   pl.BlockSpec((d_b // LANES, rows, LANES), lambda i: (0, i, 0))],
        out_shape=[jax.ShapeDtypeStruct((t, d_a), BF16), jax.ShapeDtypeStruct((d_b // LANES, t, LANES), F32)],
        scratch_shapes=[pltpu.VMEM((rows, d_a), F32)],
        compiler_params=pltpu.CompilerParams(dimension_semantics=("parallel",),
                                             vmem_limit_bytes=VMEM_LIMIT),
        name="front",
    )(x2, g1, win, lng, lnb, ws_pair, bs_full, ga)


def _fnet_tables(seq, half_w):
    r = FFT_RADIX
    assert seq == r * r
    j = np.arange(B_GROUP_DIM)
    ang = 2.0 * np.pi * np.outer(j, j) / B_GROUP_DIM
    ng = half_w // B_GROUP_DIM
    eye = np.eye(ng)
    m_ch = np.concatenate([np.kron(eye, np.cos(ang)), np.kron(eye, np.sin(ang))], axis=1)
    k1 = np.arange(r)[:, None]
    n1 = np.arange(r)[None, :]
    d_tw = np.zeros((r, 2 * r, 2 * r))
    for n2 in range(r):
        th = 2.0 * np.pi * (k1 * n1 / r + k1 * n2 / (r * r))
        c, s = np.cos(th), np.sin(th)
        d_tw[n2] = np.block([[c, -s], [s, c]])
    ph = 2.0 * np.pi * np.outer(np.arange(r), np.arange(r)) / r
    scale = 1.0 / math.sqrt(seq * B_GROUP_DIM)
    d2 = np.concatenate([np.cos(ph), -np.sin(ph)], axis=1) * scale
    return (jnp.asarray(m_ch, F32), jnp.asarray(d_tw, F32), jnp.asarray(d2, F32))


def _fnet_kernel(z_ref, mch32_ref, dtw_ref, d232_ref, gb_ref, out_ref, a2_scr, y_scr, mch_ref, d2_ref,
                 *, n_col, unroll):
    r = FFT_RADIX
    hf = pl.program_id(1)
    half_w = n_col * LANES
    mch_ref[...] = mch32_ref[...].astype(BF16)
    d2_ref[...] = d232_ref[...].astype(BF16)

    def stage1(it, carry):
        for u in range(unroll):
            n2 = it * unroll + u
            zs = jnp.concatenate([z_ref[q, pl.ds(n2, r, stride=r), :] for q in range(n_col)],
                                 axis=1).astype(BF16)
            w = jnp.dot(zs, mch_ref[...], preferred_element_type=F32)
            rhs = jnp.concatenate([w[:, :half_w], w[:, half_w:]], axis=0).astype(BF16)
            a = jnp.dot(dtw_ref[n2].astype(BF16), rhs, preferred_element_type=F32)
            for q in range(n_col):
                a2_scr[q, pl.ds(n2, r, stride=2 * r), :] = a[:r, q * LANES:(q + 1) * LANES]
                a2_scr[q, pl.ds(r + n2, r, stride=2 * r), :] = a[r:, q * LANES:(q + 1) * LANES]
        return carry

    lax.fori_loop(0, r // unroll, stage1, 0)

    def stage2(it, carry):
        for u in range(unroll):
            k1 = it * unroll + u
            row0 = pl.multiple_of(k1 * 2 * r, 2 * r)
            blk = jnp.concatenate([a2_scr[q, pl.ds(row0, 2 * r), :] for q in range(n_col)],
                                  axis=1).astype(BF16)
            y = jnp.dot(d2_ref[...], blk, preferred_element_type=F32)
            for q in range(n_col):
                y_scr[hf * n_col + q, pl.ds(k1, r, stride=r), :] = y[:, q * LANES:(q + 1) * LANES]
        return carry

    lax.fori_loop(0, r // unroll, stage2, 0)

    @pl.when(hf == pl.num_programs(1) - 1)
    def _():
        rows = 512
        n_all = y_scr.shape[0]
        for c in range(y_scr.shape[1] // rows):
            ys = [y_scr[q, c * rows:(c + 1) * rows, :] for q in range(n_all)]
            ss = sum(jnp.sum(y * y, axis=-1, keepdims=True) for y in ys)
            inv = lax.rsqrt(ss * (1.0 / (n_all * LANES)) + EPS)
            for q in range(n_all):
                out_ref[0, c * rows:(c + 1) * rows, q * LANES:(q + 1) * LANES] = (
                    ys[q] * inv * gb_ref[:, q * LANES:(q + 1) * LANES]).astype(BF16)


def _fnet(zb_cols, gb, bsz):
    n_all, t, _ = zb_cols.shape
    s = t // bsz
    d_b = n_all * LANES
    n_col = 2
    r = FFT_RADIX
    m_ch, d_tw, d2 = _fnet_tables(s, n_col * LANES)
    return pl.pallas_call(
        functools.partial(_fnet_kernel, n_col=n_col, unroll=4),
        grid=(bsz, n_all // n_col),
        in_specs=[
            pl.BlockSpec((n_col, s, LANES), lambda i, h: (h, i, 0)),
            pl.BlockSpec(m_ch.shape, lambda i, h: (0, 0)),
            pl.BlockSpec(d_tw.shape, lambda i, h: (0, 0, 0)),
            pl.BlockSpec(d2.shape, lambda i, h: (0, 0)),
            pl.BlockSpec((1, d_b), lambda i, h: (0, 0)),
        ],
        out_specs=pl.BlockSpec((1, s, d_b), lambda i, h: (i, 0, 0)),
        out_shape=jax.ShapeDtypeStruct((bsz, s, d_b), BF16),
        scratch_shapes=[pltpu.VMEM((n_col, r * 2 * r, LANES), F32), pltpu.VMEM((n_all, s, LANES), F32),
                        pltpu.VMEM(m_ch.shape, BF16), pltpu.VMEM(d2.shape, BF16)],
        compiler_params=pltpu.CompilerParams(dimension_semantics=("parallel", "arbitrary"),
                                             vmem_limit_bytes=VMEM_LIMIT),
        name="fnet",
    )(zb_cols, m_ch, d_tw, d2, gb)


def _post_kernel(x_ref, an_ref, bn_ref, woa_ref, wob_ref, g2_ref, wrt_ref, rb_ref, wsgu_ref, wsd_ref,
                 tri_ref, hs_ref, hn_ref, eidx_ref, gate_ref, rank_ref, cnt_ref, carry_scr,
                 *, n_experts, d_shared, steps_per_group):
    i = pl.program_id(0)

    @pl.when(i % steps_per_group == 0)
    def _():
        carry_scr[...] = jnp.zeros_like(carry_scr)

    h = (x_ref[...]
         + jnp.dot(an_ref[...], woa_ref[...], preferred_element_type=F32)
         + jnp.dot(bn_ref[...], wob_ref[...], preferred_element_type=F32))
    hn = _rms(h, g2_ref[...])
    hnb = hn.astype(BF16)
    gu = jnp.dot(hnb, wsgu_ref[...], preferred_element_type=F32)
    act = (_silu(gu[:, :d_shared]) * gu[:, d_shared:]).astype(BF16)
    hs = h + jnp.dot(act, wsd_ref[...], preferred_element_type=F32)
    n_slab = hn.shape[1] // LANES
    for s in range(n_slab):
        hn_ref[pl.ds(s, hn.shape[0], stride=n_slab), :] = hn[:, s * LANES:(s + 1) * LANES]
        hs_ref[pl.ds(s, hn.shape[0], stride=n_slab), :] = hs[:, s * LANES:(s + 1) * LANES]

    rows = hn.shape[0]
    eg = n_experts // N_EXPERT_GROUPS
    logits_t = lax.dot_general(wrt_ref[...], hn, (((1,), (1,)), ((), ())),
                               precision=lax.Precision.HIGHEST, preferred_element_type=F32)
    scores_t = 1.0 / (1.0 + jnp.exp(-logits_t))
    biased_t = scores_t + rb_ref[...]
    sub = lax.broadcasted_iota(jnp.int32, (eg, rows), 0)
    neg = -jnp.inf
    grp = [biased_t[g * eg:(g + 1) * eg, :] for g in range(N_EXPERT_GROUPS)]
    sco = [scores_t[g * eg:(g + 1) * eg, :] for g in range(N_EXPERT_GROUPS)]
    gid = [sub + g * eg for g in range(N_EXPERT_GROUPS)]

    gscore = []
    for g in range(N_EXPERT_GROUPS):
        m1 = jnp.max(grp[g], axis=0, keepdims=True)
        first = jnp.min(jnp.where(grp[g] == m1, sub, eg), axis=0, keepdims=True)
        m2 = jnp.max(jnp.where(sub == first, neg, grp[g]), axis=0, keepdims=True)
        gscore.append(m1 + m2)
    masked = []
    for g in range(N_EXPERT_GROUPS):
        beaten = jnp.zeros((1, rows), jnp.int32)
        for j in range(N_EXPERT_GROUPS):
            if j == g:
                continue
            wins = (gscore[j] > gscore[g]) if j > g else (gscore[j] >= gscore[g])
            beaten = beaten + wins.astype(jnp.int32)
        masked.append(jnp.where(beaten < TOPK_GROUPS, grp[g], neg))

    sel = [jnp.zeros((eg, rows), F32) for _ in range(N_EXPERT_GROUPS)]
    e_rows, w_rows = [], []
    for _ in range(TOP_K):
        m = masked[0]
        for g in range(1, N_EXPERT_GROUPS):
            m = jnp.maximum(m, masked[g])
        mk = jnp.max(m, axis=0, keepdims=True)
        cand = jnp.where(masked[0] == mk, gid[0], n_experts)
        for g in range(1, N_EXPERT_GROUPS):
            cand = jnp.minimum(cand, jnp.where(masked[g] == mk, gid[g], n_experts))
        ik = jnp.min(cand, axis=0, keepdims=True)
        wk = jnp.zeros((eg, rows), F32)
        for g in range(N_EXPERT_GROUPS):
            hit = gid[g] == ik
            masked[g] = jnp.where(hit, neg, masked[g])
            sel[g] = jnp.where(hit, 1.0, sel[g])
            wk = wk + jnp.where(hit, sco[g], 0.0)
        e_rows.append(ik)
        w_rows.append(jnp.sum(wk, axis=0, keepdims=True))
    wsum = w_rows[0]
    for k in range(1, TOP_K):
        wsum = wsum + w_rows[k]
    gates = [w_rows[k] / wsum * ROUTED_SCALE for k in range(TOP_K)]

    sel_t = jnp.concatenate(sel, axis=0)
    within = jnp.dot(sel_t.astype(BF16), tri_ref[...], preferred_element_type=F32)
    posn = within + carry_scr[:, 0:1]
    r_rows = []
    for k in range(TOP_K):
        acc = jnp.zeros((eg, rows), F32)
        for g in range(N_EXPERT_GROUPS):
            acc = acc + jnp.where(gid[g] == e_rows[k], posn[g * eg:(g + 1) * eg, :], 0.0)
        r_rows.append(jnp.sum(acc, axis=0, keepdims=True))
    new_carry = carry_scr[...] + jnp.sum(sel_t, axis=1, keepdims=True)
    carry_scr[...] = new_carry
    cnt_ref[0] = new_carry.astype(jnp.int32)
    eidx_ref[...] = jnp.concatenate(e_rows, axis=0)
    gate_ref[...] = jnp.concatenate(gates, axis=0)
    rank_ref[...] = jnp.concatenate(r_rows, axis=0).astype(jnp.int32)


def _post(x2, an, bn, woa, wob, g2, wrt, rb, wsgu, wsd):
    t, d = x2.shape
    rows = FRONT_ROWS
    n_experts = wrt.shape[0]
    d_shared = wsd.shape[0]
    n_slab = d // LANES
    steps_per_group = GROUP_TOKENS // rows
    n_groups = t // GROUP_TOKENS
    tri = jnp.asarray(np.triu(np.ones((rows, rows), np.float32), k=1), BF16)
    const = lambda shape: pl.BlockSpec(shape, lambda i: (0,) * len(shape))
    row_blk = lambda w: pl.BlockSpec((rows, w), lambda i: (i, 0))
    slab_blk = pl.BlockSpec((rows * n_slab, LANES), lambda i: (i, 0))
    tok_blk = pl.BlockSpec((TOP_K, rows), lambda i: (0, i))
    return pl.pallas_call(
        functools.partial(_post_kernel, n_experts=n_experts, d_shared=d_shared,
                          steps_per_group=steps_per_group),
        grid=(t // rows,),
        in_specs=[row_blk(d), row_blk(an.shape[1]), row_blk(bn.shape[1]),
                  const(woa.shape), const(wob.shape), const((1, d)), const(wrt.shape), const(rb.shape),
                  const(wsgu.shape), const(wsd.shape), const(tri.shape)],
        out_specs=[slab_blk, slab_blk, tok_blk, tok_blk, tok_blk,
                   pl.BlockSpec((1, n_experts, LANES), lambda i: (i // steps_per_group, 0, 0))],
        out_shape=[jax.ShapeDtypeStruct((t * n_slab, LANES), F32), jax.ShapeDtypeStruct((t * n_slab, LANES), F32),
                   jax.ShapeDtypeStruct((TOP_K, t), jnp.int32), jax.ShapeDtypeStruct((TOP_K, t), F32),
                   jax.ShapeDtypeStruct((TOP_K, t), jnp.int32),
                   jax.ShapeDtypeStruct((n_groups, n_experts, LANES), jnp.int32)],
        scratch_shapes=[pltpu.VMEM((n_experts, LANES), F32)],
        compiler_params=pltpu.CompilerParams(dimension_semantics=("arbitrary",),
                                             vmem_limit_bytes=VMEM_LIMIT),
        name="post",
    )(x2, an, bn, woa, wob, g2, wrt, rb, wsgu, wsd, tri)


def _dispatch_kernel(pend_ref, dest_ref, hn_ref, xs_hbm, zero_scr, zsem, sem, *, n_experts):
    i = pl.program_id(0)
    blk = EXPERT_ROWS
    tokens = DISPATCH_TOKENS

    def zero_copy(e):
        start = pl.multiple_of(jnp.maximum(pend_ref[e] - blk, 0), blk)
        return pltpu.make_async_copy(zero_scr, xs_hbm.at[pl.ds(start, blk)], zsem)

    @pl.when(i == 0)
    def _():
        zero_scr[...] = jnp.zeros_like(zero_scr)

        def start_body(e, c):
            zero_copy(e).start()
            return c

        def wait_body(e, c):
            zero_copy(e).wait()
            return c

        lax.fori_loop(0, n_experts, start_body, 0)
        lax.fori_loop(0, n_experts, wait_body, 0)

        def tail_copy(b):
            return pltpu.make_async_copy(zero_scr, xs_hbm.at[pl.ds(pl.multiple_of(b * blk, blk), blk)], zsem)

        def tail_start(b, c):
            tail_copy(b).start()
            return c

        def tail_wait(b, c):
            tail_copy(b).wait()
            return c

        first_free = pend_ref[n_experts - 1] // blk
        lax.fori_loop(first_free, xs_hbm.shape[0] // blk, tail_start, 0)
        lax.fori_loop(first_free, xs_hbm.shape[0] // blk, tail_wait, 0)

    def issue(j, c):
        for k in range(TOP_K):
            pltpu.make_async_copy(hn_ref.at[pl.ds(j, 1)], xs_hbm.at[pl.ds(dest_ref[j * TOP_K + k], 1)],
                                  sem).start(priority=k % 2)
        return c

    lax.fori_loop(0, tokens, issue, 0)
    for k in range(TOP_K):
        pltpu.make_async_copy(hn_ref, xs_hbm.at[pl.ds(0, tokens)], sem).wait()


def _dispatch(pad_end, dest_flat, hn, n_rows):
    t, d = hn.shape
    n_experts = pad_end.shape[0]
    tokens = DISPATCH_TOKENS
    grid_spec = pltpu.PrefetchScalarGridSpec(
        num_scalar_prefetch=1,
        grid=(t // tokens,),
        in_specs=[pl.BlockSpec((tokens * TOP_K,), lambda i, pe: (i,), memory_space=pltpu.SMEM),
                  pl.BlockSpec((tokens, d), lambda i, pe: (i, 0))],
        out_specs=pl.BlockSpec(memory_space=pl.ANY),
        scratch_shapes=[pltpu.VMEM((EXPERT_ROWS, d), F32), pltpu.SemaphoreType.DMA, pltpu.SemaphoreType.DMA],
    )
    return pl.pallas_call(
        functools.partial(_dispatch_kernel, n_experts=n_experts),
        grid_spec=grid_spec,
        out_shape=jax.ShapeDtypeStruct((n_rows, d), F32),
        compiler_params=pltpu.CompilerParams(dimension_semantics=("arbitrary",), has_side_effects=True,
                                             disable_bounds_checks=True),
        name="dispatch",
    )(pad_end, dest_flat, hn)


def _expert_kernel(be_ref, nu_ref, xs_ref, wgu_ref, wd_ref, ys_ref, *, d_expert):
    i = pl.program_id(0)

    @pl.when(i < nu_ref[0])
    def _():
        xb = xs_ref[...].astype(BF16)
        gu = jnp.dot(xb, wgu_ref[0], preferred_element_type=F32)
        act = (_silu(gu[:, :d_expert]) * gu[:, d_expert:]).astype(BF16)
        ys_ref[...] = jnp.dot(act, wd_ref[0], preferred_element_type=F32)

    @pl.when(i >= nu_ref[0])
    def _():
        ys_ref[...] = jnp.zeros_like(ys_ref)


def _experts(block_e, n_used, xs, wgu, wd):
    n_rows, d = xs.shape
    blk = EXPERT_ROWS
    d_expert = wd.shape[1]
    grid_spec = pltpu.PrefetchScalarGridSpec(
        num_scalar_prefetch=2,
        grid=(n_rows // blk,),
        in_specs=[pl.BlockSpec((blk, d), lambda i, be, nu: (jnp.minimum(i, nu[0] - 1), 0)),
                  pl.BlockSpec((1, d, 2 * d_expert), lambda i, be, nu: (be[i], 0, 0)),
                  pl.BlockSpec((1, d_expert, d), lambda i, be, nu: (be[i], 0, 0))],
        out_specs=pl.BlockSpec((blk, d), lambda i, be, nu: (i, 0)),
    )
    return pl.pallas_call(
        functools.partial(_expert_kernel, d_expert=d_expert),
        grid_spec=grid_spec,
        out_shape=jax.ShapeDtypeStruct((n_rows, d), F32),
        compiler_params=pltpu.CompilerParams(dimension_semantics=("arbitrary",),
                                             vmem_limit_bytes=VMEM_LIMIT),
        name="experts",
    )(block_e, n_used, xs, wgu, wd)


def _combine_kernel(dcur_ref, dnext_ref, gate_ref, hs_ref, gf_ref, ys_hbm, out_ref, buf, sem):
    i = pl.program_id(0)
    n = pl.num_programs(0)
    tokens = COMBINE_TOKENS
    slot = i % 2

    def row_copy(dref, s, j, k):
        return pltpu.make_async_copy(ys_hbm.at[pl.ds(dref[j * TOP_K + k], 1)],
                                     buf.at[s, k, pl.ds(j, 1)], sem.at[s])

    def issue(dref, s):
        def body(j, c):
            for k in range(TOP_K):
                row_copy(dref, s, j, k).start(priority=k % 2)
            return c
        lax.fori_loop(0, tokens, body, 0)

    @pl.when(i == 0)
    def _():
        issue(dcur_ref, 0)

    @pl.when(i + 1 < n)
    def _():
        issue(dnext_ref, 1 - slot)

    for k in range(TOP_K):
        pltpu.make_async_copy(ys_hbm.at[pl.ds(0, tokens)], buf.at[slot, k], sem.at[slot]).wait()

    acc = hs_ref[...]
    g = gate_ref[...]
    for k in range(TOP_K):
        acc = acc + g[:, k:k + 1] * buf[slot, k]
    out_ref[...] = _rms(acc, gf_ref[...])


def _combine(dest_flat, gate_tk, hs, gf, ys):
    t, d = hs.shape
    tokens = COMBINE_TOKENS
    n = t // tokens
    dspec = lambda f: pl.BlockSpec((tokens * TOP_K,), f, memory_space=pltpu.SMEM)
    return pl.pallas_call(
        _combine_kernel,
        grid=(n,),
        in_specs=[dspec(lambda i: (i,)), dspec(lambda i: (jnp.minimum(i + 1, n - 1),)),
                  pl.BlockSpec((tokens, TOP_K), lambda i: (i, 0)),
                  pl.BlockSpec((tokens, d), lambda i: (i, 0)),
                  pl.BlockSpec((1, d), lambda i: (0, 0)),
                  pl.BlockSpec(memory_space=pl.ANY)],
        out_specs=pl.BlockSpec((tokens, d), lambda i: (i, 0)),
        out_shape=jax.ShapeDtypeStruct((t, d), F32),
        scratch_shapes=[pltpu.VMEM((2, TOP_K, tokens, d), F32), pltpu.SemaphoreType.DMA((2,))],
        compiler_params=pltpu.CompilerParams(dimension_semantics=("arbitrary",),
                                             vmem_limit_bytes=VMEM_LIMIT, disable_bounds_checks=True),
        name="combine",
    )(dest_flat, dest_flat, gate_tk, hs, gf, ys)


def _moe_kernel(cnt_ref, off_ref, next_ref, dest_ref, gate_ref, hn_ref, hs_hbm, wgu_ref, wd_ref, gf_ref,
                out_ref, acc_scr, x_scr, y_scr, list_scr, state_scr, sem, *, d_expert, n_slab):
    g = pl.program_id(0)
    e = pl.program_id(1)
    n_e = pl.num_programs(1)
    tokens = GROUP_TOKENS
    blk = EXPERT_ROWS
    batch = 8

    def slab(ref, row0):
        return ref.at[pl.ds(pl.multiple_of(row0, n_slab), n_slab), :]

    def acc_init():
        return pltpu.make_async_copy(hs_hbm.at[pl.ds(pl.multiple_of(g * tokens * n_slab, n_slab), tokens * n_slab), :],
                                     acc_scr.at[pl.ds(0, tokens * n_slab), :], sem)

    def gather(base):
        for r in range(blk):
            slab(x_scr, r * n_slab)[...] = slab(hn_ref, list_scr[base + r] & -n_slab)[...]

    def scatter(base):
        for j in range(blk // batch):
            rows, vals = [], []
            for u in range(batch):
                r = j * batch + u
                a = list_scr[base + r]
                row0 = a & -n_slab
                rows.append(row0)
                vals.append(slab(acc_scr, row0)[...] + gate_ref[a] * slab(y_scr, r * n_slab)[...])
            for u in reversed(range(batch)):
                slab(acc_scr, rows[u])[...] = vals[u]

    n = cnt_ref[g * n_e + e]
    off = off_ref[g * n_e + e]

    @pl.when(e == 0)
    def _():
        acc_init().start()
        y_scr[...] = jnp.zeros_like(y_scr)
        state_scr[0] = 0

        @pl.when(g == 0)
        def _():
            def clear(j, c):
                list_scr[j] = 0
                return c

            lax.fori_loop(tokens * TOP_K, list_scr.shape[0], clear, 0)

        def build(j, c):
            for u in range(batch):
                a = j * batch + u
                list_scr[dest_ref[a]] = a
            return c

        lax.fori_loop(0, tokens * TOP_K // batch, build, 0)
        gather(jnp.where(n > 0, off, next_ref[g * n_e + e]))
        acc_init().wait()

    n_blocks = (n + blk - 1) // blk
    nxt = next_ref[g * n_e + e]
    row_id = lax.broadcasted_iota(jnp.int32, (blk, 1), 0)

    def block(i, carry):
        base = off + i * blk
        xb = jnp.concatenate([x_scr[pl.ds(s, blk, stride=n_slab), :] for s in range(n_slab)], axis=1).astype(BF16)
        scatter(state_scr[0])
        gather(jnp.where(i + 1 < n_blocks, base + blk, nxt))
        gu = jnp.dot(xb, wgu_ref[0], preferred_element_type=F32)
        act = (_silu(gu[:, :d_expert]) * gu[:, d_expert:]).astype(BF16)
        y = jnp.dot(act, wd_ref[0], preferred_element_type=F32)
        y = jnp.where(row_id < n - i * blk, y, 0.0)
        for s in range(n_slab):
            y_scr[pl.ds(s, blk, stride=n_slab), :] = y[:, s * LANES:(s + 1) * LANES]
        state_scr[0] = base
        return carry

    lax.fori_loop(0, n_blocks, block, 0)

    @pl.when(e == n_e - 1)
    def _():
        scatter(state_scr[0])
        rows = 256
        for c in range(tokens // rows):
            acc = jnp.concatenate([acc_scr[pl.ds(c * rows * n_slab + s, rows, stride=n_slab), :]
                                   for s in range(n_slab)], axis=1)
            out_ref[c * rows:(c + 1) * rows, :] = _rms(acc, gf_ref[...])


def _moe(cnt_flat, off_flat, next_flat, dest_flat, gate_flat, hn3, hs3, wgu, wd, gf):
    n_slab = gf.shape[1] // LANES
    t = hn3.shape[0] // n_slab
    d = gf.shape[1]
    n_experts, d_expert = wd.shape[0], wd.shape[1]
    tokens = GROUP_TOKENS
    n_groups = t // tokens
    assert TOP_K == n_slab, "assignment index -> slab row uses TOP_K == d_model / 128"
    smem_blk = pl.BlockSpec((tokens * TOP_K,), lambda g, e, *_: (g,), memory_space=pltpu.SMEM)
    grid_spec = pltpu.PrefetchScalarGridSpec(
        num_scalar_prefetch=3,
        grid=(n_groups, n_experts),
        in_specs=[smem_blk, smem_blk,
                  pl.BlockSpec((tokens * n_slab, LANES), lambda g, e, *_: (g, 0)),
                  pl.BlockSpec(memory_space=pl.ANY),
                  pl.BlockSpec((1, d, 2 * d_expert), lambda g, e, *_: (e, 0, 0)),
                  pl.BlockSpec((1, d_expert, d), lambda g, e, *_: (e, 0, 0)),
                  pl.BlockSpec((1, d), lambda g, e, *_: (0, 0))],
        out_specs=pl.BlockSpec((tokens, d), lambda g, e, *_: (g, 0)),
        scratch_shapes=[pltpu.VMEM((tokens * n_slab, LANES), F32),
                        pltpu.VMEM((EXPERT_ROWS * n_slab, LANES), F32),
                        pltpu.VMEM((EXPERT_ROWS * n_slab, LANES), F32),
                        pltpu.SMEM((tokens * TOP_K + EXPERT_ROWS,), jnp.int32),
                        pltpu.SMEM((1,), jnp.int32),
                        pltpu.SemaphoreType.DMA],
    )
    return pl.pallas_call(
        functools.partial(_moe_kernel, d_expert=d_expert, n_slab=n_slab),
        grid_spec=grid_spec,
        out_shape=jax.ShapeDtypeStruct((t, d), F32),
        compiler_params=pltpu.CompilerParams(dimension_semantics=("arbitrary", "arbitrary"),
                                             vmem_limit_bytes=VMEM_LIMIT),
        name="moe",
    )(cnt_flat, off_flat, next_flat, dest_flat, gate_flat, hn3, hs3, wgu, wd, gf)


def kernel(x, norm1_g, w_in, sgu_ln_g, sgu_ln_b, w_spatial, b_spatial, out_norm_a_g, out_norm_b_g, w_out,
           norm2_g, w_router, router_bias, w_gate, w_up, w_down, ws_gate, ws_up, ws_down, final_norm_g):
    bsz, s, d = x.shape
    depth = w_in.shape[0]
    assert depth == 1, "the fused final norm assumes a single trunk layer"
    l = 0
    t = bsz * s
    d_a = sgu_ln_g.shape[-1]
    d_b = out_norm_b_g.shape[-1]
    n_heads, chunk = w_spatial.shape[1], w_spatial.shape[2]
    n_experts = w_router.shape[-1]
    assert d_a == n_heads * A_HEAD_DIM and chunk == LANES and 2 * A_HEAD_DIM == LANES
    assert t % FRONT_ROWS == 0 and FRONT_ROWS % chunk == 0 and s % FRONT_ROWS == 0
    assert t % DISPATCH_TOKENS == 0 and t % COMBINE_TOKENS == 0
    row = lambda v: v.reshape(1, -1).astype(F32)

    x2 = x.reshape(t, d)

    win = w_in[l].astype(BF16)
    ws = w_spatial[l]
    ws_pair = jnp.concatenate([ws[0::2], ws[1::2]], axis=2).astype(BF16)
    bs_full = jnp.repeat(b_spatial[l].T, A_HEAD_DIM, axis=1).astype(F32)
    woa = w_out[l][:d_a].astype(BF16)
    wob = w_out[l][d_a:].astype(BF16)
    wrt = w_router[l].T.astype(F32)
    rb = jnp.broadcast_to(router_bias[l].astype(F32)[:, None], (n_experts, 1))
    wsgu = jnp.concatenate([ws_gate[l], ws_up[l]], axis=1).astype(BF16)
    wsd = ws_down[l].astype(BF16)
    wgu = jnp.concatenate([w_gate[l], w_up[l]], axis=2).astype(BF16)
    wd = w_down[l].astype(BF16)

    an, zb = _front(x2, row(norm1_g[l]), win, row(sgu_ln_g[l]), row(sgu_ln_b[l]), ws_pair, bs_full,
                    row(out_norm_a_g[l]), d_a=d_a, d_b=d_b, chunk=chunk)
    bn = _fnet(zb, row(out_norm_b_g[l]), bsz).reshape(t, d_b)

    hs3, hn3, eidx, gate, rank, cnt = _post(x2, an, bn, woa, wob, row(norm2_g[l]), wrt, rb, wsgu, wsd)

    n_groups = t // GROUP_TOKENS
    counts = cnt[:, :, 0]
    offs = jnp.cumsum(counts, axis=1) - counts
    eidx3 = eidx.reshape(TOP_K, n_groups, GROUP_TOKENS)
    e_ids = jnp.arange(n_experts, dtype=jnp.int32)[:, None, None, None]
    dest = jnp.sum(jnp.where(eidx3[None] == e_ids, offs.T[:, None, :, None], 0), axis=0) \
        + rank.reshape(TOP_K, n_groups, GROUP_TOKENS)
    dest_flat = dest.transpose(1, 2, 0).reshape(-1).astype(jnp.int32)
    gate_flat = gate.T.reshape(-1)
    e_row = jnp.arange(n_experts, dtype=jnp.int32)
    later = (e_row[None, None, :] > e_row[None, :, None]) & (counts[:, None, :] > 0)
    first_later = jnp.min(jnp.where(later, e_row[None, None, :], n_experts), axis=2)
    next_off = jnp.sum(jnp.where(first_later[:, :, None] == e_row[None, None, :], offs[:, None, :], 0), axis=2)
    flat = lambda a: a.reshape(-1).astype(jnp.int32)

    out = _moe(flat(counts), flat(offs), flat(next_off), dest_flat, gate_flat,
               hn3, hs3, wgu, wd, row(final_norm_g))
    return out.reshape(bsz, s, d)
```

```python
import functools
import math

import numpy as np
import jax
import jax.numpy as jnp
from jax import lax
from jax.experimental import pallas as pl
from jax.experimental.pallas import tpu as pltpu

A_HEAD_DIM = 64
B_GROUP_DIM = 64
TOP_K = 8
N_EXPERT_GROUPS = 8
TOPK_GROUPS = 4
ROUTED_SCALE = 2.5
EPS = 1e-6

LANES = 128
FRONT_ROWS = 512
EXPERT_ROWS = 128
DISPATCH_TOKENS = 512
COMBINE_TOKENS = 128
GROUP_TOKENS = 2048
FFT_RADIX = 64
FFT_PAD = 8
VMEM_LIMIT = 56 * 1024 * 1024

F32 = jnp.float32
BF16 = jnp.bfloat16


def _rms(x, g):
    return x * lax.rsqrt(jnp.mean(x * x, axis=-1, keepdims=True) + EPS) * g


def _gelu(x):
    return 0.5 * x * (1.0 + lax.erf(x * (1.0 / math.sqrt(2.0))))


def _silu(x):
    return x / (1.0 + jnp.exp(-x))


def _front_kernel(x_ref, g1_ref, win_ref, lng_ref, lnb_ref, ws_ref, bs_ref, ga_ref,
                  an_ref, zb_ref, ya_scr, *, d_a, chunk):
    x = x_ref[...]
    xn = _rms(x, g1_ref[...]).astype(BF16)
    z = jnp.dot(xn, win_ref[...], preferred_element_type=F32)
    for q in range(zb_ref.shape[0]):
        zb_ref[q] = z[:, 2 * d_a + q * LANES:2 * d_a + (q + 1) * LANES].astype(BF16)
    u = _gelu(z[:, :d_a])
    v = _gelu(z[:, d_a:2 * d_a])
    mu = jnp.mean(v, axis=-1, keepdims=True)
    vc = v - mu
    v = vc * lax.rsqrt(jnp.mean(vc * vc, axis=-1, keepdims=True) + EPS) * lng_ref[...] + lnb_ref[...]
    rows = x.shape[0]
    lane = lax.broadcasted_iota(jnp.int32, (chunk, LANES), 1)
    low = lane < A_HEAD_DIM
    for c in range(rows // chunk):
        r0 = c * chunk
        for hp in range(d_a // LANES):
            c0 = hp * LANES
            vp = v[r0:r0 + chunk, c0:c0 + LANES]
            rhs = jnp.concatenate([jnp.where(low, vp, 0.0), jnp.where(low, 0.0, vp)], axis=0).astype(BF16)
            mixed = jnp.dot(ws_ref[hp], rhs, preferred_element_type=F32) + bs_ref[:, c0:c0 + LANES]
            ya_scr[r0:r0 + chunk, c0:c0 + LANES] = u[r0:r0 + chunk, c0:c0 + LANES] * mixed
    an_ref[...] = _rms(ya_scr[...], ga_ref[...]).astype(BF16)


def _front(x2, g1, win, lng, lnb, ws_pair, bs_full, ga, *, d_a, d_b, chunk):
    t, d = x2.shape
    d_in = win.shape[1]
    rows = FRONT_ROWS
    const = lambda shape: pl.BlockSpec(shape, lambda i: (0,) * len(shape))
    return pl.pallas_call(
        functools.partial(_front_kernel, d_a=d_a, chunk=chunk),
        grid=(t // rows,),
        in_specs=[
            pl.BlockSpec((rows, d), lambda i: (i, 0)),
            const((1, d)), const((d, d_in)), const((1, d_a)), const((1, d_a)),
            const(ws_pair.shape), const(bs_full.shape), const((1, d_a)),
        ],
        out_specs=[pl.BlockSpec((rows, d_a), lambda i: (i, 0)),
                   pl.BlockSpec((d_b // LANES, rows, LANES), lambda i: (0, i, 0))],
        out_shape=[jax.ShapeDtypeStruct((t, d_a), BF16), jax.ShapeDtypeStruct((d_b // LANES, t, LANES), BF16)],
        scratch_shapes=[pltpu.VMEM((rows, d_a), F32)],
        compiler_params=pltpu.CompilerParams(dimension_semantics=("parallel",),
                                             vmem_limit_bytes=VMEM_LIMIT),
        name="front",
    )(x2, g1, win, lng, lnb, ws_pair, bs_full, ga)


def _fnet_tables(seq, half_w):
    r = FFT_RADIX
    assert seq == r * r
    j = np.arange(B_GROUP_DIM)
    ang = 2.0 * np.pi * np.outer(j, j) / B_GROUP_DIM
    ng = half_w // B_GROUP_DIM
    eye = np.eye(ng)
    m_ch = np.concatenate([np.kron(eye, np.cos(ang)), np.kron(eye, np.sin(ang))], axis=1)
    k1 = np.arange(r)[:, None]
    n1 = np.arange(r)[None, :]
    d_tw = np.zeros((r, 2 * r, 2 * r))
    for n2 in range(r):
        th = 2.0 * np.pi * (k1 * n1 / r + k1 * n2 / (r * r))
        c, s = np.cos(th), np.sin(th)
        d_tw[n2] = np.block([[c, -s], [s, c]])
    ph = 2.0 * np.pi * np.outer(np.arange(r), np.arange(r)) / r
    scale = 1.0 / math.sqrt(seq * B_GROUP_DIM)
    d2 = np.concatenate([np.cos(ph), -np.sin(ph)], axis=1) * scale
    return (jnp.asarray(m_ch, F32), jnp.asarray(d_tw, F32), jnp.asarray(d2, F32))


def _fnet_kernel(z_ref, mch32_ref, dtw_ref, d232_ref, gb_ref, out_ref, w_scr, a2_scr, y_scr, mch_ref, d2_ref,
                 *, n_col, unroll):
    r = FFT_RADIX
    rp = r + FFT_PAD
    ap = 2 * r + FFT_PAD
    hf = pl.program_id(1)
    half_w = n_col * LANES
    mch_ref[...] = mch32_ref[...].astype(BF16)
    d2_ref[...] = d232_ref[...].astype(BF16)

    chunk = 8 * r
    for c in range(r * r // chunk):
        zc = jnp.concatenate([z_ref[q, c * chunk:(c + 1) * chunk, :] for q in range(n_col)], axis=1)
        w = jnp.dot(zc, mch_ref[...], preferred_element_type=F32)
        for j in range(chunk // r):
            n1 = c * (chunk // r) + j
            for p in range(2 * n_col):
                w_scr[p, n1 * rp:n1 * rp + r, :] = w[j * r:(j + 1) * r, p * LANES:(p + 1) * LANES]

    def stage1(it, carry):
        for u in range(unroll):
            n2 = it * unroll + u
            rhs = jnp.concatenate(
                [jnp.concatenate([w_scr[ri * n_col + q, pl.ds(n2, r, stride=rp), :] for q in range(n_col)], axis=1)
                 for ri in range(2)], axis=0).astype(BF16)
            a = jnp.dot(dtw_ref[n2].astype(BF16), rhs, preferred_element_type=F32)
            for q in range(n_col):
                a2_scr[q, pl.ds(n2, r, stride=ap), :] = a[:r, q * LANES:(q + 1) * LANES]
                a2_scr[q, pl.ds(r + n2, r, stride=ap), :] = a[r:, q * LANES:(q + 1) * LANES]
        return carry

    lax.fori_loop(0, r // unroll, stage1, 0)

    def stage2(it, carry):
        for u in range(unroll):
            k1 = it * unroll + u
            row0 = pl.multiple_of(k1 * ap, 8)
            blk = jnp.concatenate([a2_scr[q, pl.ds(row0, 2 * r), :] for q in range(n_col)],
                                  axis=1).astype(BF16)
            y = jnp.dot(d2_ref[...], blk, preferred_element_type=F32)
            for q in range(n_col):
                y_scr[hf * n_col + q, pl.ds(k1, r, stride=rp), :] = y[:, q * LANES:(q + 1) * LANES]
        return carry

    lax.fori_loop(0, r // unroll, stage2, 0)

    @pl.when(hf == pl.num_programs(1) - 1)
    def _():
        n_all = y_scr.shape[0]
        for k2 in range(r):
            ys = [y_scr[q, k2 * rp:k2 * rp + r, :] for q in range(n_all)]
            ss = sum(jnp.sum(y * y, axis=-1, keepdims=True) for y in ys)
            inv = lax.rsqrt(ss * (1.0 / (n_all * LANES)) + EPS)
            for q in range(n_all):
                out_ref[0, k2 * r:(k2 + 1) * r, q * LANES:(q + 1) * LANES] = (
                    ys[q] * inv * gb_ref[:, q * LANES:(q + 1) * LANES]).astype(BF16)


def _fnet(zb_cols, gb, bsz):
    n_all, t, _ = zb_cols.shape
    s = t // bsz
    d_b = n_all * LANES
    n_col = 2
    r = FFT_RADIX
    rp, ap = r + FFT_PAD, 2 * r + FFT_PAD
    m_ch, d_tw, d2 = _fnet_tables(s, n_col * LANES)
    return pl.pallas_call(
        functools.partial(_fnet_kernel, n_col=n_col, unroll=8),
        grid=(bsz, n_all // n_col),
        in_specs=[
            pl.BlockSpec((n_col, s, LANES), lambda i, h: (h, i, 0)),
            pl.BlockSpec(m_ch.shape, lambda i, h: (0, 0)),
            pl.BlockSpec(d_tw.shape, lambda i, h: (0, 0, 0)),
            pl.BlockSpec(d2.shape, lambda i, h: (0, 0)),
            pl.BlockSpec((1, d_b), lambda i, h: (0, 0)),
        ],
        out_specs=pl.BlockSpec((1, s, d_b), lambda i, h: (i, 0, 0)),
        out_shape=jax.ShapeDtypeStruct((bsz, s, d_b), BF16),
        scratch_shapes=[pltpu.VMEM((2 * n_col, r * rp, LANES), F32), pltpu.VMEM((n_col, r * ap, LANES), F32),
                        pltpu.VMEM((n_all, r * rp, LANES), F32),
                        pltpu.VMEM(m_ch.shape, BF16), pltpu.VMEM(d2.shape, BF16)],
        compiler_params=pltpu.CompilerParams(dimension_semantics=("parallel", "arbitrary"),
                                             vmem_limit_bytes=VMEM_LIMIT),
        name="fnet",
    )(zb_cols, m_ch, d_tw, d2, gb)


def _post_kernel(x_ref, an_ref, bn_ref, woa_ref, wob_ref, g2_ref, wrt_ref, rb_ref, wsgu_ref, wsd_ref,
                 tri_ref, hs_ref, hn_ref, eidx_ref, gate_ref, rank_ref, cnt_ref, carry_scr,
                 *, n_experts, d_shared, steps_per_group):
    i = pl.program_id(0)

    @pl.when(i % steps_per_group == 0)
    def _():
        carry_scr[...] = jnp.zeros_like(carry_scr)

    h = (x_ref[...]
         + jnp.dot(an_ref[...], woa_ref[...], preferred_element_type=F32)
         + jnp.dot(bn_ref[...], wob_ref[...], preferred_element_type=F32))
    hn = _rms(h, g2_ref[...])
    hnb = hn.astype(BF16)
    gu = jnp.dot(hnb, wsgu_ref[...], preferred_element_type=F32)
    act = (_silu(gu[:, :d_shared]) * gu[:, d_shared:]).astype(BF16)
    hs = h + jnp.dot(act, wsd_ref[...], preferred_element_type=F32)
    n_slab = hn.shape[1] // LANES
    for s in range(n_slab):
        hn_ref[pl.ds(s, hn.shape[0], stride=n_slab), :] = hn[:, s * LANES:(s + 1) * LANES]
        hs_ref[pl.ds(s, hn.shape[0], stride=n_slab), :] = hs[:, s * LANES:(s + 1) * LANES]

    rows = hn.shape[0]
    eg = n_experts // N_EXPERT_GROUPS
    logits_t = lax.dot_general(wrt_ref[...], hn, (((1,), (1,)), ((), ())),
                               precision=lax.Precision.HIGHEST, preferred_element_type=F32)
    scores_t = 1.0 / (1.0 + jnp.exp(-logits_t))
    biased_t = scores_t + rb_ref[...]
    sub = lax.broadcasted_iota(jnp.int32, (eg, rows), 0)
    neg = -jnp.inf
    grp = [biased_t[g * eg:(g + 1) * eg, :] for g in range(N_EXPERT_GROUPS)]
    sco = [scores_t[g * eg:(g + 1) * eg, :] for g in range(N_EXPERT_GROUPS)]
    gid = [sub + g * eg for g in range(N_EXPERT_GROUPS)]

    gscore = []
    for g in range(N_EXPERT_GROUPS):
        m1 = jnp.max(grp[g], axis=0, keepdims=True)
        first = jnp.min(jnp.where(grp[g] == m1, sub, eg), axis=0, keepdims=True)
        m2 = jnp.max(jnp.where(sub == first, neg, grp[g]), axis=0, keepdims=True)
        gscore.append(m1 + m2)
    masked = []
    for g in range(N_EXPERT_GROUPS):
        beaten = jnp.zeros((1, rows), jnp.int32)
        for j in range(N_EXPERT_GROUPS):
            if j == g:
                continue
            wins = (gscore[j] > gscore[g]) if j > g else (gscore[j] >= gscore[g])
            beaten = beaten + wins.astype(jnp.int32)
        masked.append(jnp.where(beaten < TOPK_GROUPS, grp[g], neg))

    sel = [jnp.zeros((eg, rows), F32) for _ in range(N_EXPERT_GROUPS)]
    e_rows, w_rows = [], []
    for _ in range(TOP_K):
        m = masked[0]
        for g in range(1, N_EXPERT_GROUPS):
            m = jnp.maximum(m, masked[g])
        mk = jnp.max(m, axis=0, keepdims=True)
        cand = jnp.where(masked[0] == mk, gid[0], n_experts)
        for g in range(1, N_EXPERT_GROUPS):
            cand = jnp.minimum(cand, jnp.where(masked[g] == mk, gid[g], n_experts))
        ik = jnp.min(cand, axis=0, keepdims=True)
        wk = jnp.zeros((eg, rows), F32)
        for g in range(N_EXPERT_GROUPS):
            hit = gid[g] == ik
            masked[g] = jnp.where(hit, neg, masked[g])
            sel[g] = jnp.where(hit, 1.0, sel[g])
            wk = wk + jnp.where(hit, sco[g], 0.0)
        e_rows.append(ik)
        w_rows.append(jnp.sum(wk, axis=0, keepdims=True))
    wsum = w_rows[0]
    for k in range(1, TOP_K):
        wsum = wsum + w_rows[k]
    gates = [w_rows[k] / wsum * ROUTED_SCALE for k in range(TOP_K)]

    sel_t = jnp.concatenate(sel, axis=0)
    within = jnp.dot(sel_t.astype(BF16), tri_ref[...], preferred_element_type=F32)
    posn = within + carry_scr[:, 0:1]
    r_rows = []
    for k in range(TOP_K):
        acc = jnp.zeros((eg, rows), F32)
        for g in range(N_EXPERT_GROUPS):
            acc = acc + jnp.where(gid[g] == e_rows[k], posn[g * eg:(g + 1) * eg, :], 0.0)
        r_rows.append(jnp.sum(acc, axis=0, keepdims=True))
    new_carry = carry_scr[...] + jnp.sum(sel_t, axis=1, keepdims=True)
    carry_scr[...] = new_carry
    cnt_ref[0] = new_carry.astype(jnp.int32)
    eidx_ref[...] = jnp.concatenate(e_rows, axis=0)
    gate_ref[...] = jnp.concatenate(gates, axis=0)
    rank_ref[...] = jnp.concatenate(r_rows, axis=0).astype(jnp.int32)


def _post(x2, an, bn, woa, wob, g2, wrt, rb, wsgu, wsd):
    t, d = x2.shape
    rows = FRONT_ROWS
    n_experts = wrt.shape[0]
    d_shared = wsd.shape[0]
    n_slab = d // LANES
    steps_per_group = GROUP_TOKENS // rows
    n_groups = t // GROUP_TOKENS
    tri = jnp.asarray(np.triu(np.ones((rows, rows), np.float32), k=1), BF16)
    const = lambda shape: pl.BlockSpec(shape, lambda i: (0,) * len(shape))
    row_blk = lambda w: pl.BlockSpec((rows, w), lambda i: (i, 0))
    slab_blk = pl.BlockSpec((rows * n_slab, LANES), lambda i: (i, 0))
    tok_blk = pl.BlockSpec((TOP_K, rows), lambda i: (0, i))
    return pl.pallas_call(
        functools.partial(_post_kernel, n_experts=n_experts, d_shared=d_shared,
                          steps_per_group=steps_per_group),
        grid=(t // rows,),
        in_specs=[row_blk(d), row_blk(an.shape[1]), row_blk(bn.shape[1]),
                  const(woa.shape), const(wob.shape), const((1, d)), const(wrt.shape), const(rb.shape),
                  const(wsgu.shape), const(wsd.shape), const(tri.shape)],
        out_specs=[slab_blk, slab_blk, tok_blk, tok_blk, tok_blk,
                   pl.BlockSpec((1, n_experts, LANES), lambda i: (i // steps_per_group, 0, 0))],
        out_shape=[jax.ShapeDtypeStruct((t * n_slab, LANES), F32), jax.ShapeDtypeStruct((t * n_slab, LANES), F32),
                   jax.ShapeDtypeStruct((TOP_K, t), jnp.int32), jax.ShapeDtypeStruct((TOP_K, t), F32),
                   jax.ShapeDtypeStruct((TOP_K, t), jnp.int32),
                   jax.ShapeDtypeStruct((n_groups, n_experts, LANES), jnp.int32)],
        scratch_shapes=[pltpu.VMEM((n_experts, LANES), F32)],
        compiler_params=pltpu.CompilerParams(dimension_semantics=("arbitrary",),
                                             vmem_limit_bytes=VMEM_LIMIT),
        name="post",
    )(x2, an, bn, woa, wob, g2, wrt, rb, wsgu, wsd, tri)


def _dispatch_kernel(pend_ref, dest_ref, hn_ref, xs_hbm, zero_scr, zsem, sem, *, n_experts):
    i = pl.program_id(0)
    blk = EXPERT_ROWS
    tokens = DISPATCH_TOKENS

    def zero_copy(e):
        start = pl.multiple_of(jnp.maximum(pend_ref[e] - blk, 0), blk)
        return pltpu.make_async_copy(zero_scr, xs_hbm.at[pl.ds(start, blk)], zsem)

    @pl.when(i == 0)
    def _():
        zero_scr[...] = jnp.zeros_like(zero_scr)

        def start_body(e, c):
            zero_copy(e).start()
            return c

        def wait_body(e, c):
            zero_copy(e).wait()
            return c

        lax.fori_loop(0, n_experts, start_body, 0)
        lax.fori_loop(0, n_experts, wait_body, 0)

        def tail_copy(b):
            return pltpu.make_async_copy(zero_scr, xs_hbm.at[pl.ds(pl.multiple_of(b * blk, blk), blk)], zsem)

        def tail_start(b, c):
            tail_copy(b).start()
            return c

        def tail_wait(b, c):
            tail_copy(b).wait()
            return c

        first_free = pend_ref[n_experts - 1] // blk
        lax.fori_loop(first_free, xs_hbm.shape[0] // blk, tail_start, 0)
        lax.fori_loop(first_free, xs_hbm.shape[0] // blk, tail_wait, 0)

    def issue(j, c):
        for k in range(TOP_K):
            pltpu.make_async_copy(hn_ref.at[pl.ds(j, 1)], xs_hbm.at[pl.ds(dest_ref[j * TOP_K + k], 1)],
                                  sem).start(priority=k % 2)
        return c

    lax.fori_loop(0, tokens, issue, 0)
    for k in range(TOP_K):
        pltpu.make_async_copy(hn_ref, xs_hbm.at[pl.ds(0, tokens)], sem).wait()


def _dispatch(pad_end, dest_flat, hn, n_rows):
    t, d = hn.shape
    n_experts = pad_end.shape[0]
    tokens = DISPATCH_TOKENS
    grid_spec = pltpu.PrefetchScalarGridSpec(
        num_scalar_prefetch=1,
        grid=(t // tokens,),
        in_specs=[pl.BlockSpec((tokens * TOP_K,), lambda i, pe: (i,), memory_space=pltpu.SMEM),
                  pl.BlockSpec((tokens, d), lambda i, pe: (i, 0))],
        out_specs=pl.BlockSpec(memory_space=pl.ANY),
        scratch_shapes=[pltpu.VMEM((EXPERT_ROWS, d), F32), pltpu.SemaphoreType.DMA, pltpu.SemaphoreType.DMA],
    )
    return pl.pallas_call(
        functools.partial(_dispatch_kernel, n_experts=n_experts),
        grid_spec=grid_spec,
        out_shape=jax.ShapeDtypeStruct((n_rows, d), F32),
        compiler_params=pltpu.CompilerParams(dimension_semantics=("arbitrary",), has_side_effects=True,
                                             disable_bounds_checks=True),
        name="dispatch",
    )(pad_end, dest_flat, hn)


def _expert_kernel(be_ref, nu_ref, xs_ref, wgu_ref, wd_ref, ys_ref, *, d_expert):
    i = pl.program_id(0)

    @pl.when(i < nu_ref[0])
    def _():
        xb = xs_ref[...].astype(BF16)
        gu = jnp.dot(xb, wgu_ref[0], preferred_element_type=F32)
        act = (_silu(gu[:, :d_expert]) * gu[:, d_expert:]).astype(BF16)
        ys_ref[...] = jnp.dot(act, wd_ref[0], preferred_element_type=F32)

    @pl.when(i >= nu_ref[0])
    def _():
        ys_ref[...] = jnp.zeros_like(ys_ref)


def _experts(block_e, n_used, xs, wgu, wd):
    n_rows, d = xs.shape
    blk = EXPERT_ROWS
    d_expert = wd.shape[1]
    grid_spec = pltpu.PrefetchScalarGridSpec(
        num_scalar_prefetch=2,
        grid=(n_rows // blk,),
        in_specs=[pl.BlockSpec((blk, d), lambda i, be, nu: (jnp.minimum(i, nu[0] - 1), 0)),
                  pl.BlockSpec((1, d, 2 * d_expert), lambda i, be, nu: (be[i], 0, 0)),
                  pl.BlockSpec((1, d_expert, d), lambda i, be, nu: (be[i], 0, 0))],
        out_specs=pl.BlockSpec((blk, d), lambda i, be, nu: (i, 0)),
    )
    return pl.pallas_call(
        functools.partial(_expert_kernel, d_expert=d_expert),
        grid_spec=grid_spec,
        out_shape=jax.ShapeDtypeStruct((n_rows, d), F32),
        compiler_params=pltpu.CompilerParams(dimension_semantics=("arbitrary",),
                                             vmem_limit_bytes=VMEM_LIMIT),
        name="experts",
    )(block_e, n_used, xs, wgu, wd)


def _combine_kernel(dcur_ref, dnext_ref, gate_ref, hs_ref, gf_ref, ys_hbm, out_ref, buf, sem):
    i = pl.program_id(0)
    n = pl.num_programs(0)
    tokens = COMBINE_TOKENS
    slot = i % 2

    def row_copy(dref, s, j, k):
        return pltpu.make_async_copy(ys_hbm.at[pl.ds(dref[j * TOP_K + k], 1)],
                                     buf.at[s, k, pl.ds(j, 1)], sem.at[s])

    def issue(dref, s):
        def body(j, c):
            for k in range(TOP_K):
                row_copy(dref, s, j, k).start(priority=k % 2)
            return c
        lax.fori_loop(0, tokens, body, 0)

    @pl.when(i == 0)
    def _():
        issue(dcur_ref, 0)

    @pl.when(i + 1 < n)
    def _():
        issue(dnext_ref, 1 - slot)

    for k in range(TOP_K):
        pltpu.make_async_copy(ys_hbm.at[pl.ds(0, tokens)], buf.at[slot, k], sem.at[slot]).wait()

    acc = hs_ref[...]
    g = gate_ref[...]
    for k in range(TOP_K):
        acc = acc + g[:, k:k + 1] * buf[slot, k]
    out_ref[...] = _rms(acc, gf_ref[...])


def _combine(dest_flat, gate_tk, hs, gf, ys):
    t, d = hs.shape
    tokens = COMBINE_TOKENS
    n = t // tokens
    dspec = lambda f: pl.BlockSpec((tokens * TOP_K,), f, memory_space=pltpu.SMEM)
    return pl.pallas_call(
        _combine_kernel,
        grid=(n,),
        in_specs=[dspec(lambda i: (i,)), dspec(lambda i: (jnp.minimum(i + 1, n - 1),)),
                  pl.BlockSpec((tokens, TOP_K), lambda i: (i, 0)),
                  pl.BlockSpec((tokens, d), lambda i: (i, 0)),
                  pl.BlockSpec((1, d), lambda i: (0, 0)),
                  pl.BlockSpec(memory_space=pl.ANY)],
        out_specs=pl.BlockSpec((tokens, d), lambda i: (i, 0)),
        out_shape=jax.ShapeDtypeStruct((t, d), F32),
        scratch_shapes=[pltpu.VMEM((2, TOP_K, tokens, d), F32), pltpu.SemaphoreType.DMA((2,))],
        compiler_params=pltpu.CompilerParams(dimension_semantics=("arbitrary",),
                                             vmem_limit_bytes=VMEM_LIMIT, disable_bounds_checks=True),
        name="combine",
    )(dest_flat, dest_flat, gate_tk, hs, gf, ys)


def _moe_kernel(cnt_ref, off_ref, next_ref, dest_ref, gate_ref, hn_ref, hs_hbm, wgu_ref, wd_ref, gf_ref,
                out_ref, acc_scr, x_scr, y_scr, list_scr, state_scr, sem, *, d_expert, n_slab):
    g = pl.program_id(0)
    e = pl.program_id(1)
    n_e = pl.num_programs(1)
    tokens = GROUP_TOKENS
    blk = EXPERT_ROWS
    batch = 8

    def slab(ref, row0):
        return ref.at[pl.ds(pl.multiple_of(row0, n_slab), n_slab), :]

    def acc_init():
        return pltpu.make_async_copy(hs_hbm.at[pl.ds(pl.multiple_of(g * tokens * n_slab, n_slab), tokens * n_slab), :],
                                     acc_scr.at[pl.ds(0, tokens * n_slab), :], sem)

    def gather(base):
        for r in range(blk):
            slab(x_scr, r * n_slab)[...] = slab(hn_ref, list_scr[base + r] & -n_slab)[...]

    def scatter(base):
        for j in range(blk // batch):
            rows, vals = [], []
            for u in range(batch):
                r = j * batch + u
                a = list_scr[base + r]
                row0 = a & -n_slab
                rows.append(row0)
                vals.append(slab(acc_scr, row0)[...] + gate_ref[a] * slab(y_scr, r * n_slab)[...])
            for u in reversed(range(batch)):
                slab(acc_scr, rows[u])[...] = vals[u]

    n = cnt_ref[g * n_e + e]
    off = off_ref[g * n_e + e]

    @pl.when(e == 0)
    def _():
        acc_init().start()
        y_scr[...] = jnp.zeros_like(y_scr)
        state_scr[0] = 0

        @pl.when(g == 0)
        def _():
            def clear(j, c):
                list_scr[j] = 0
                return c

            lax.fori_loop(tokens * TOP_K, list_scr.shape[0], clear, 0)

        def build(j, c):
            for u in range(batch):
                a = j * batch + u
                list_scr[dest_ref[a]] = a
            return c

        lax.fori_loop(0, tokens * TOP_K // batch, build, 0)
        gather(jnp.where(n > 0, off, next_ref[g * n_e + e]))
        acc_init().wait()

    n_blocks = (n + blk - 1) // blk
    nxt = next_ref[g * n_e + e]
    row_id = lax.broadcasted_iota(jnp.int32, (blk, 1), 0)

    def block(i, carry):
        base = off + i * blk
        xb = jnp.concatenate([x_scr[pl.ds(s, blk, stride=n_slab), :] for s in range(n_slab)], axis=1).astype(BF16)
        scatter(state_scr[0])
        gather(jnp.where(i + 1 < n_blocks, base + blk, nxt))
        gu = jnp.dot(xb, wgu_ref[0], preferred_element_type=F32)
        act = (_silu(gu[:, :d_expert]) * gu[:, d_expert:]).astype(BF16)
        y = jnp.dot(act, wd_ref[0], preferred_element_type=F32)
        y = jnp.where(row_id < n - i * blk, y, 0.0)
        for s in range(n_slab):
            y_scr[pl.ds(s, blk, stride=n_slab), :] = y[:, s * LANES:(s + 1) * LANES]
        state_scr[0] = base
        return carry

    lax.fori_loop(0, n_blocks, block, 0)

    @pl.when(e == n_e - 1)
    def _():
        scatter(state_scr[0])
        rows = 256
        for c in range(tokens // rows):
            acc = jnp.concatenate([acc_scr[pl.ds(c * rows * n_slab + s, rows, stride=n_slab), :]
                                   for s in range(n_slab)], axis=1)
            out_ref[c * rows:(c + 1) * rows, :] = _rms(acc, gf_ref[...])


def _moe(cnt_flat, off_flat, next_flat, dest_flat, gate_flat, hn3, hs3, wgu, wd, gf):
    n_slab = gf.shape[1] // LANES
    t = hn3.shape[0] // n_slab
    d = gf.shape[1]
    n_experts, d_expert = wd.shape[0], wd.shape[1]
    tokens = GROUP_TOKENS
    n_groups = t // tokens
    assert TOP_K == n_slab, "assignment index -> slab row uses TOP_K == d_model / 128"
    smem_blk = pl.BlockSpec((tokens * TOP_K,), lambda g, e, *_: (g,), memory_space=pltpu.SMEM)
    grid_spec = pltpu.PrefetchScalarGridSpec(
        num_scalar_prefetch=3,
        grid=(n_groups, n_experts),
        in_specs=[smem_blk, smem_blk,
                  pl.BlockSpec((tokens * n_slab, LANES), lambda g, e, *_: (g, 0)),
                  pl.BlockSpec(memory_space=pl.ANY),
                  pl.BlockSpec((1, d, 2 * d_expert), lambda g, e, *_: (e, 0, 0)),
                  pl.BlockSpec((1, d_expert, d), lambda g, e, *_: (e, 0, 0)),
                  pl.BlockSpec((1, d), lambda g, e, *_: (0, 0))],
        out_specs=pl.BlockSpec((tokens, d), lambda g, e, *_: (g, 0)),
        scratch_shapes=[pltpu.VMEM((tokens * n_slab, LANES), F32),
                        pltpu.VMEM((EXPERT_ROWS * n_slab, LANES), F32),
                        pltpu.VMEM((EXPERT_ROWS * n_slab, LANES), F32),
                        pltpu.SMEM((tokens * TOP_K + EXPERT_ROWS,), jnp.int32),
                        pltpu.SMEM((1,), jnp.int32),
                        pltpu.SemaphoreType.DMA],
    )
    return pl.pallas_call(
        functools.partial(_moe_kernel, d_expert=d_expert, n_slab=n_slab),
        grid_spec=grid_spec,
        out_shape=jax.ShapeDtypeStruct((t, d), F32),
        compiler_params=pltpu.CompilerParams(dimension_semantics=("arbitrary", "arbitrary"),
                                             vmem_limit_bytes=VMEM_LIMIT),
        name="moe",
    )(cnt_flat, off_flat, next_flat, dest_flat, gate_flat, hn3, hs3, wgu, wd, gf)


def kernel(x, norm1_g, w_in, sgu_ln_g, sgu_ln_b, w_spatial, b_spatial, out_norm_a_g, out_norm_b_g, w_out,
           norm2_g, w_router, router_bias, w_gate, w_up, w_down, ws_gate, ws_up, ws_down, final_norm_g):
    bsz, s, d = x.shape
    depth = w_in.shape[0]
    assert depth == 1, "the fused final norm assumes a single trunk layer"
    l = 0
    t = bsz * s
    d_a = sgu_ln_g.shape[-1]
    d_b = out_norm_b_g.shape[-1]
    n_heads, chunk = w_spatial.shape[1], w_spatial.shape[2]
    n_experts = w_router.shape[-1]
    assert d_a == n_heads * A_HEAD_DIM and chunk == LANES and 2 * A_HEAD_DIM == LANES
    assert t % FRONT_ROWS == 0 and FRONT_ROWS % chunk == 0 and s % FRONT_ROWS == 0
    assert t % DISPATCH_TOKENS == 0 and t % COMBINE_TOKENS == 0
    row = lambda v: v.reshape(1, -1).astype(F32)

    x2 = x.reshape(t, d)

    win = w_in[l].astype(BF16)
    ws = w_spatial[l]
    ws_pair = jnp.concatenate([ws[0::2], ws[1::2]], axis=2).astype(BF16)
    bs_full = jnp.repeat(b_spatial[l].T, A_HEAD_DIM, axis=1).astype(F32)
    woa = w_out[l][:d_a].astype(BF16)
    wob = w_out[l][d_a:].astype(BF16)
    wrt = w_router[l].T.astype(F32)
    rb = jnp.broadcast_to(router_bias[l].astype(F32)[:, None], (n_experts, 1))
    wsgu = jnp.concatenate([ws_gate[l], ws_up[l]], axis=1).astype(BF16)
    wsd = ws_down[l].astype(BF16)
    wgu = jnp.concatenate([w_gate[l], w_up[l]], axis=2).astype(BF16)
    wd = w_down[l].astype(BF16)

    an, zb = _front(x2, row(norm1_g[l]), win, row(sgu_ln_g[l]), row(sgu_ln_b[l]), ws_pair, bs_full,
                    row(out_norm_a_g[l]), d_a=d_a, d_b=d_b, chunk=chunk)
    bn = _fnet(zb, row(out_norm_b_g[l]), bsz).reshape(t, d_b)

    hs3, hn3, eidx, gate, rank, cnt = _post(x2, an, bn, woa, wob, row(norm2_g[l]), wrt, rb, wsgu, wsd)

    n_groups = t // GROUP_TOKENS
    counts = cnt[:, :, 0]
    offs = jnp.cumsum(counts, axis=1) - counts
    eidx3 = eidx.reshape(TOP_K, n_groups, GROUP_TOKENS)
    e_ids = jnp.arange(n_experts, dtype=jnp.int32)[:, None, None, None]
    dest = jnp.sum(jnp.where(eidx3[None] == e_ids, offs.T[:, None, :, None], 0), axis=0) \
        + rank.reshape(TOP_K, n_groups, GROUP_TOKENS)
    dest_flat = dest.transpose(1, 2, 0).reshape(-1).astype(jnp.int32)
    gate_flat = gate.T.reshape(-1)
    e_row = jnp.arange(n_experts, dtype=jnp.int32)
    later = (e_row[None, None, :] > e_row[None, :, None]) & (counts[:, None, :] > 0)
    first_later = jnp.min(jnp.where(later, e_row[None, None, :], n_experts), axis=2)
    next_off = jnp.sum(jnp.where(first_later[:, :, None] == e_row[None, None, :], offs[:, None, :], 0), axis=2)
    flat = lambda a: a.reshape(-1).astype(jnp.int32)

    out = _moe(flat(counts), flat(offs), flat(next_off), dest_flat, gate_flat,
               hn3, hs3, wgu, wd, row(final_norm_g))
    return out.reshape(bsz, s, d)
```

```python
import functools
import math

import numpy as np
import jax
import jax.numpy as jnp
from jax import lax
from jax.experimental import pallas as pl
from jax.experimental.pallas import tpu as pltpu

A_HEAD_DIM = 64
B_GROUP_DIM = 64
TOP_K = 8
N_EXPERT_GROUPS = 8
TOPK_GROUPS = 4
ROUTED_SCALE = 2.5
EPS = 1e-6

LANES = 128
FRONT_ROWS = 512
EXPERT_ROWS = 128
DISPATCH_TOKENS = 512
COMBINE_TOKENS = 128
GROUP_TOKENS = 2048
FFT_RADIX = 64
FFT_PAD = 8
VMEM_LIMIT = 56 * 1024 * 1024

F32 = jnp.float32
BF16 = jnp.bfloat16


def _rms(x, g):
    return x * lax.rsqrt(jnp.mean(x * x, axis=-1, keepdims=True) + EPS) * g


def _gelu(x):
    return 0.5 * x * (1.0 + lax.erf(x * (1.0 / math.sqrt(2.0))))


def _silu(x):
    return x / (1.0 + jnp.exp(-x))


def _front_kernel(x_ref, g1_ref, win_ref, lng_ref, lnb_ref, ws_ref, bs_ref, ga_ref,
                  an_ref, zb_ref, ya_scr, *, d_a, chunk):
    x = x_ref[...]
    xn = _rms(x, g1_ref[...]).astype(BF16)
    z = jnp.dot(xn, win_ref[...], preferred_element_type=F32)
    for q in range(zb_ref.shape[0]):
        zb_ref[q] = z[:, 2 * d_a + q * LANES:2 * d_a + (q + 1) * LANES].astype(BF16)
    u = _gelu(z[:, :d_a])
    v = _gelu(z[:, d_a:2 * d_a])
    mu = jnp.mean(v, axis=-1, keepdims=True)
    vc = v - mu
    v = vc * lax.rsqrt(jnp.mean(vc * vc, axis=-1, keepdims=True) + EPS) * lng_ref[...] + lnb_ref[...]
    rows = x.shape[0]
    lane = lax.broadcasted_iota(jnp.int32, (chunk, LANES), 1)
    low = lane < A_HEAD_DIM
    for c in range(rows // chunk):
        r0 = c * chunk
        for hp in range(d_a // LANES):
            c0 = hp * LANES
            vp = v[r0:r0 + chunk, c0:c0 + LANES]
            rhs = jnp.concatenate([jnp.where(low, vp, 0.0), jnp.where(low, 0.0, vp)], axis=0).astype(BF16)
            mixed = jnp.dot(ws_ref[hp], rhs, preferred_element_type=F32) + bs_ref[:, c0:c0 + LANES]
            ya_scr[r0:r0 + chunk, c0:c0 + LANES] = u[r0:r0 + chunk, c0:c0 + LANES] * mixed
    an_ref[...] = _rms(ya_scr[...], ga_ref[...]).astype(BF16)


def _front(x2, g1, win, lng, lnb, ws_pair, bs_full, ga, *, d_a, d_b, chunk):
    t, d = x2.shape
    d_in = win.shape[1]
    rows = FRONT_ROWS
    const = lambda shape: pl.BlockSpec(shape, lambda i: (0,) * len(shape))
    return pl.pallas_call(
        functools.partial(_front_kernel, d_a=d_a, chunk=chunk),
        grid=(t // rows,),
        in_specs=[
            pl.BlockSpec((rows, d), lambda i: (i, 0)),
            const((1, d)), const((d, d_in)), const((1, d_a)), const((1, d_a)),
            const(ws_pair.shape), const(bs_full.shape), const((1, d_a)),
        ],
        out_specs=[pl.BlockSpec((rows, d_a), lambda i: (i, 0)),
                   pl.BlockSpec((d_b // LANES, rows, LANES), lambda i: (0, i, 0))],
        out_shape=[jax.ShapeDtypeStruct((t, d_a), BF16), jax.ShapeDtypeStruct((d_b // LANES, t, LANES), BF16)],
        scratch_shapes=[pltpu.VMEM((rows, d_a), F32)],
        compiler_params=pltpu.CompilerParams(dimension_semantics=("parallel",),
                                             vmem_limit_bytes=VMEM_LIMIT),
        name="front",
    )(x2, g1, win, lng, lnb, ws_pair, bs_full, ga)


def _fnet_tables(seq, half_w):
    r = FFT_RADIX
    assert seq == r * r
    j = np.arange(B_GROUP_DIM)
    ang = 2.0 * np.pi * np.outer(j, j) / B_GROUP_DIM
    ng = half_w // B_GROUP_DIM
    eye = np.eye(ng)
    m_ch = np.concatenate([np.kron(eye, np.cos(ang)), np.kron(eye, np.sin(ang))], axis=1)
    k1 = np.arange(r)[:, None]
    n1 = np.arange(r)[None, :]
    d_tw = np.zeros((r, 2 * r, 2 * r))
    for n2 in range(r):
        th = 2.0 * np.pi * (k1 * n1 / r + k1 * n2 / (r * r))
        c, s = np.cos(th), np.sin(th)
        d_tw[n2] = np.block([[c, -s], [s, c]])
    ph = 2.0 * np.pi * np.outer(np.arange(r), np.arange(r)) / r
    scale = 1.0 / math.sqrt(seq * B_GROUP_DIM)
    d2 = np.concatenate([np.cos(ph), -np.sin(ph)], axis=1) * scale
    return (jnp.asarray(m_ch, F32), jnp.asarray(d_tw, F32), jnp.asarray(d2, F32))


def _fnet_kernel(z_ref, mch32_ref, dtw_ref, d232_ref, gb_ref, out_ref, w_scr, a2_scr, y_scr, mch_ref, d2_ref,
                 *, n_col, unroll):
    r = FFT_RADIX
    rp = r + FFT_PAD
    ap = 2 * r + FFT_PAD
    hf = pl.program_id(1)
    half_w = n_col * LANES
    mch_ref[...] = mch32_ref[...].astype(BF16)
    d2_ref[...] = d232_ref[...].astype(BF16)

    chunk = 8 * r
    for c in range(r * r // chunk):
        zc = jnp.concatenate([z_ref[q, c * chunk:(c + 1) * chunk, :] for q in range(n_col)], axis=1)
        w = jnp.dot(zc, mch_ref[...], preferred_element_type=F32)
        for j in range(chunk // r):
            n1 = c * (chunk // r) + j
            for p in range(2 * n_col):
                w_scr[p, n1 * rp:n1 * rp + r, :] = w[j * r:(j + 1) * r, p * LANES:(p + 1) * LANES]

    def stage1(it, carry):
        for u in range(unroll):
            n2 = it * unroll + u
            rhs = jnp.concatenate(
                [jnp.concatenate([w_scr[ri * n_col + q, pl.ds(n2, r, stride=rp), :] for q in range(n_col)], axis=1)
                 for ri in range(2)], axis=0).astype(BF16)
            a = jnp.dot(dtw_ref[n2].astype(BF16), rhs, preferred_element_type=F32)
            for q in range(n_col):
                a2_scr[q, pl.ds(n2, r, stride=ap), :] = a[:r, q * LANES:(q + 1) * LANES]
                a2_scr[q, pl.ds(r + n2, r, stride=ap), :] = a[r:, q * LANES:(q + 1) * LANES]
        return carry

    lax.fori_loop(0, r // unroll, stage1, 0)

    def stage2(it, carry):
        for u in range(unroll):
            k1 = it * unroll + u
            row0 = pl.multiple_of(k1 * ap, 8)
            blk = jnp.concatenate([a2_scr[q, pl.ds(row0, 2 * r), :] for q in range(n_col)],
                                  axis=1).astype(BF16)
            y = jnp.dot(d2_ref[...], blk, preferred_element_type=F32)
            for q in range(n_col):
                y_scr[hf * n_col + q, pl.ds(k1, r, stride=rp), :] = y[:, q * LANES:(q + 1) * LANES]
        return carry

    lax.fori_loop(0, r // unroll, stage2, 0)

    @pl.when(hf == pl.num_programs(1) - 1)
    def _():
        n_all = y_scr.shape[0]
        for k2 in range(r):
            ys = [y_scr[q, k2 * rp:k2 * rp + r, :] for q in range(n_all)]
            ss = sum(jnp.sum(y * y, axis=-1, keepdims=True) for y in ys)
            inv = lax.rsqrt(ss * (1.0 / (n_all * LANES)) + EPS)
            for q in range(n_all):
                out_ref[0, k2 * r:(k2 + 1) * r, q * LANES:(q + 1) * LANES] = (
                    ys[q] * inv * gb_ref[:, q * LANES:(q + 1) * LANES]).astype(BF16)


def _fnet(zb_cols, gb, bsz):
    n_all, t, _ = zb_cols.shape
    s = t // bsz
    d_b = n_all * LANES
    n_col = 2
    r = FFT_RADIX
    rp, ap = r + FFT_PAD, 2 * r + FFT_PAD
    m_ch, d_tw, d2 = _fnet_tables(s, n_col * LANES)
    return pl.pallas_call(
        functools.partial(_fnet_kernel, n_col=n_col, unroll=8),
        grid=(bsz, n_all // n_col),
        in_specs=[
            pl.BlockSpec((n_col, s, LANES), lambda i, h: (h, i, 0)),
            pl.BlockSpec(m_ch.shape, lambda i, h: (0, 0)),
            pl.BlockSpec(d_tw.shape, lambda i, h: (0, 0, 0)),
            pl.BlockSpec(d2.shape, lambda i, h: (0, 0)),
            pl.BlockSpec((1, d_b), lambda i, h: (0, 0)),
        ],
        out_specs=pl.BlockSpec((1, s, d_b), lambda i, h: (i, 0, 0)),
        out_shape=jax.ShapeDtypeStruct((bsz, s, d_b), BF16),
        scratch_shapes=[pltpu.VMEM((2 * n_col, r * rp, LANES), F32), pltpu.VMEM((n_col, r * ap, LANES), F32),
                        pltpu.VMEM((n_all, r * rp, LANES), F32),
                        pltpu.VMEM(m_ch.shape, BF16), pltpu.VMEM(d2.shape, BF16)],
        compiler_params=pltpu.CompilerParams(dimension_semantics=("parallel", "arbitrary"),
                                             vmem_limit_bytes=VMEM_LIMIT),
        name="fnet",
    )(zb_cols, m_ch, d_tw, d2, gb)


def _post_kernel(x_ref, an_ref, bn_ref, woa_ref, wob_ref, g2_ref, wrt_ref, rb_ref, wsgu_ref, wsd_ref,
                 tri_ref, hs_ref, hn_ref, eidx_ref, gate_ref, rank_ref, cnt_ref, carry_scr,
                 *, n_experts, d_shared, steps_per_group):
    i = pl.program_id(0)

    @pl.when(i % steps_per_group == 0)
    def _():
        carry_scr[...] = jnp.zeros_like(carry_scr)

    h = (x_ref[...]
         + jnp.dot(an_ref[...], woa_ref[...], preferred_element_type=F32)
         + jnp.dot(bn_ref[...], wob_ref[...], preferred_element_type=F32))
    hn = _rms(h, g2_ref[...])
    hnb = hn.astype(BF16)
    gu = jnp.dot(hnb, wsgu_ref[...], preferred_element_type=F32)
    act = (_silu(gu[:, :d_shared]) * gu[:, d_shared:]).astype(BF16)
    hs = h + jnp.dot(act, wsd_ref[...], preferred_element_type=F32)
    n_slab = hn.shape[1] // LANES
    for s in range(n_slab):
        hn_ref[pl.ds(s, hn.shape[0], stride=n_slab), :] = hn[:, s * LANES:(s + 1) * LANES]
        hs_ref[pl.ds(s, hn.shape[0], stride=n_slab), :] = hs[:, s * LANES:(s + 1) * LANES]

    rows = hn.shape[0]
    eg = n_experts // N_EXPERT_GROUPS
    logits_t = lax.dot_general(wrt_ref[...], hn, (((1,), (1,)), ((), ())),
                               precision=lax.Precision.HIGHEST, preferred_element_type=F32)
    scores_t = 1.0 / (1.0 + jnp.exp(-logits_t))
    biased_t = scores_t + rb_ref[...]
    sub = lax.broadcasted_iota(jnp.int32, (eg, rows), 0)
    neg = -jnp.inf
    grp = [biased_t[g * eg:(g + 1) * eg, :] for g in range(N_EXPERT_GROUPS)]
    sco = [scores_t[g * eg:(g + 1) * eg, :] for g in range(N_EXPERT_GROUPS)]
    gid = [sub + g * eg for g in range(N_EXPERT_GROUPS)]

    gscore = []
    for g in range(N_EXPERT_GROUPS):
        m1 = jnp.max(grp[g], axis=0, keepdims=True)
        first = jnp.min(jnp.where(grp[g] == m1, sub, eg), axis=0, keepdims=True)
        m2 = jnp.max(jnp.where(sub == first, neg, grp[g]), axis=0, keepdims=True)
        gscore.append(m1 + m2)
    masked = []
    for g in range(N_EXPERT_GROUPS):
        beaten = jnp.zeros((1, rows), jnp.int32)
        for j in range(N_EXPERT_GROUPS):
            if j == g:
                continue
            wins = (gscore[j] > gscore[g]) if j > g else (gscore[j] >= gscore[g])
            beaten = beaten + wins.astype(jnp.int32)
        masked.append(jnp.where(beaten < TOPK_GROUPS, grp[g], neg))

    sel = [jnp.zeros((eg, rows), F32) for _ in range(N_EXPERT_GROUPS)]
    e_rows, w_rows = [], []
    for _ in range(TOP_K):
        m = masked[0]
        for g in range(1, N_EXPERT_GROUPS):
            m = jnp.maximum(m, masked[g])
        mk = jnp.max(m, axis=0, keepdims=True)
        cand = jnp.where(masked[0] == mk, gid[0], n_experts)
        for g in range(1, N_EXPERT_GROUPS):
            cand = jnp.minimum(cand, jnp.where(masked[g] == mk, gid[g], n_experts))
        ik = jnp.min(cand, axis=0, keepdims=True)
        wk = jnp.zeros((eg, rows), F32)
        for g in range(N_EXPERT_GROUPS):
            hit = gid[g] == ik
            masked[g] = jnp.where(hit, neg, masked[g])
            sel[g] = jnp.where(hit, 1.0, sel[g])
            wk = wk + jnp.where(hit, sco[g], 0.0)
        e_rows.append(ik)
        w_rows.append(jnp.sum(wk, axis=0, keepdims=True))
    wsum = w_rows[0]
    for k in range(1, TOP_K):
        wsum = wsum + w_rows[k]
    gates = [w_rows[k] / wsum * ROUTED_SCALE for k in range(TOP_K)]

    sel_t = jnp.concatenate(sel, axis=0)
    within = jnp.dot(sel_t.astype(BF16), tri_ref[...], preferred_element_type=F32)
    posn = within + carry_scr[:, 0:1]
    r_rows = []
    for k in range(TOP_K):
        acc = jnp.zeros((eg, rows), F32)
        for g in range(N_EXPERT_GROUPS):
            acc = acc + jnp.where(gid[g] == e_rows[k], posn[g * eg:(g + 1) * eg, :], 0.0)
        r_rows.append(jnp.sum(acc, axis=0, keepdims=True))
    new_carry = carry_scr[...] + jnp.sum(sel_t, axis=1, keepdims=True)
    carry_scr[...] = new_carry
    cnt_ref[0] = new_carry.astype(jnp.int32)
    eidx_ref[...] = jnp.concatenate(e_rows, axis=0)
    gate_ref[...] = jnp.concatenate(gates, axis=0)
    rank_ref[...] = jnp.concatenate(r_rows, axis=0).astype(jnp.int32)


def _post(x2, an, bn, woa, wob, g2, wrt, rb, wsgu, wsd):
    t, d = x2.shape
    rows = FRONT_ROWS
    n_experts = wrt.shape[0]
    d_shared = wsd.shape[0]
    n_slab = d // LANES
    steps_per_group = GROUP_TOKENS // rows
    n_groups = t // GROUP_TOKENS
    tri = jnp.asarray(np.triu(np.ones((rows, rows), np.float32), k=1), BF16)
    const = lambda shape: pl.BlockSpec(shape, lambda i: (0,) * len(shape))
    row_blk = lambda w: pl.BlockSpec((rows, w), lambda i: (i, 0))
    slab_blk = pl.BlockSpec((rows * n_slab, LANES), lambda i: (i, 0))
    tok_blk = pl.BlockSpec((TOP_K, rows), lambda i: (0, i))
    return pl.pallas_call(
        functools.partial(_post_kernel, n_experts=n_experts, d_shared=d_shared,
                          steps_per_group=steps_per_group),
        grid=(t // rows,),
        in_specs=[row_blk(d), row_blk(an.shape[1]), row_blk(bn.shape[1]),
                  const(woa.shape), const(wob.shape), const((1, d)), const(wrt.shape), const(rb.shape),
                  const(wsgu.shape), const(wsd.shape), const(tri.shape)],
        out_specs=[slab_blk, slab_blk, tok_blk, tok_blk, tok_blk,
                   pl.BlockSpec((1, n_experts, LANES), lambda i: (i // steps_per_group, 0, 0))],
        out_shape=[jax.ShapeDtypeStruct((t * n_slab, LANES), F32), jax.ShapeDtypeStruct((t * n_slab, LANES), F32),
                   jax.ShapeDtypeStruct((TOP_K, t), jnp.int32), jax.ShapeDtypeStruct((TOP_K, t), F32),
                   jax.ShapeDtypeStruct((TOP_K, t), jnp.int32),
                   jax.ShapeDtypeStruct((n_groups, n_experts, LANES), jnp.int32)],
        scratch_shapes=[pltpu.VMEM((n_experts, LANES), F32)],
        compiler_params=pltpu.CompilerParams(dimension_semantics=("arbitrary",),
                                             vmem_limit_bytes=VMEM_LIMIT),
        name="post",
    )(x2, an, bn, woa, wob, g2, wrt, rb, wsgu, wsd, tri)


def _dispatch_kernel(pend_ref, dest_ref, hn_ref, xs_hbm, zero_scr, zsem, sem, *, n_experts):
    i = pl.program_id(0)
    blk = EXPERT_ROWS
    tokens = DISPATCH_TOKENS

    def zero_copy(e):
        start = pl.multiple_of(jnp.maximum(pend_ref[e] - blk, 0), blk)
        return pltpu.make_async_copy(zero_scr, xs_hbm.at[pl.ds(start, blk)], zsem)

    @pl.when(i == 0)
    def _():
        zero_scr[...] = jnp.zeros_like(zero_scr)

        def start_body(e, c):
            zero_copy(e).start()
            return c

        def wait_body(e, c):
            zero_copy(e).wait()
            return c

        lax.fori_loop(0, n_experts, start_body, 0)
        lax.fori_loop(0, n_experts, wait_body, 0)

        def tail_copy(b):
            return pltpu.make_async_copy(zero_scr, xs_hbm.at[pl.ds(pl.multiple_of(b * blk, blk), blk)], zsem)

        def tail_start(b, c):
            tail_copy(b).start()
            return c

        def tail_wait(b, c):
            tail_copy(b).wait()
            return c

        first_free = pend_ref[n_experts - 1] // blk
        lax.fori_loop(first_free, xs_hbm.shape[0] // blk, tail_start, 0)
        lax.fori_loop(first_free, xs_hbm.shape[0] // blk, tail_wait, 0)

    def issue(j, c):
        for k in range(TOP_K):
            pltpu.make_async_copy(hn_ref.at[pl.ds(j, 1)], xs_hbm.at[pl.ds(dest_ref[j * TOP_K + k], 1)],
                                  sem).start(priority=k % 2)
        return c

    lax.fori_loop(0, tokens, issue, 0)
    for k in range(TOP_K):
        pltpu.make_async_copy(hn_ref, xs_hbm.at[pl.ds(0, tokens)], sem).wait()


def _dispatch(pad_end, dest_flat, hn, n_rows):
    t, d = hn.shape
    n_experts = pad_end.shape[0]
    tokens = DISPATCH_TOKENS
    grid_spec = pltpu.PrefetchScalarGridSpec(
        num_scalar_prefetch=1,
        grid=(t // tokens,),
        in_specs=[pl.BlockSpec((tokens * TOP_K,), lambda i, pe: (i,), memory_space=pltpu.SMEM),
                  pl.BlockSpec((tokens, d), lambda i, pe: (i, 0))],
        out_specs=pl.BlockSpec(memory_space=pl.ANY),
        scratch_shapes=[pltpu.VMEM((EXPERT_ROWS, d), F32), pltpu.SemaphoreType.DMA, pltpu.SemaphoreType.DMA],
    )
    return pl.pallas_call(
        functools.partial(_dispatch_kernel, n_experts=n_experts),
        grid_spec=grid_spec,
        out_shape=jax.ShapeDtypeStruct((n_rows, d), F32),
        compiler_params=pltpu.CompilerParams(dimension_semantics=("arbitrary",), has_side_effects=True,
                                             disable_bounds_checks=True),
        name="dispatch",
    )(pad_end, dest_flat, hn)


def _expert_kernel(be_ref, nu_ref, xs_ref, wgu_ref, wd_ref, ys_ref, *, d_expert):
    i = pl.program_id(0)

    @pl.when(i < nu_ref[0])
    def _():
        xb = xs_ref[...].astype(BF16)
        gu = jnp.dot(xb, wgu_ref[0], preferred_element_type=F32)
        act = (_silu(gu[:, :d_expert]) * gu[:, d_expert:]).astype(BF16)
        ys_ref[...] = jnp.dot(act, wd_ref[0], preferred_element_type=F32)

    @pl.when(i >= nu_ref[0])
    def _():
        ys_ref[...] = jnp.zeros_like(ys_ref)


def _experts(block_e, n_used, xs, wgu, wd):
    n_rows, d = xs.shape
    blk = EXPERT_ROWS
    d_expert = wd.shape[1]
    grid_spec = pltpu.PrefetchScalarGridSpec(
        num_scalar_prefetch=2,
        grid=(n_rows // blk,),
        in_specs=[pl.BlockSpec((blk, d), lambda i, be, nu: (jnp.minimum(i, nu[0] - 1), 0)),
                  pl.BlockSpec((1, d, 2 * d_expert), lambda i, be, nu: (be[i], 0, 0)),
                  pl.BlockSpec((1, d_expert, d), lambda i, be, nu: (be[i], 0, 0))],
        out_specs=pl.BlockSpec((blk, d), lambda i, be, nu: (i, 0)),
    )
    return pl.pallas_call(
        functools.partial(_expert_kernel, d_expert=d_expert),
        grid_spec=grid_spec,
        out_shape=jax.ShapeDtypeStruct((n_rows, d), F32),
        compiler_params=pltpu.CompilerParams(dimension_semantics=("arbitrary",),
                                             vmem_limit_bytes=VMEM_LIMIT),
        name="experts",
    )(block_e, n_used, xs, wgu, wd)


def _combine_kernel(dcur_ref, dnext_ref, gate_ref, hs_ref, gf_ref, ys_hbm, out_ref, buf, sem):
    i = pl.program_id(0)
    n = pl.num_programs(0)
    tokens = COMBINE_TOKENS
    slot = i % 2

    def row_copy(dref, s, j, k):
        return pltpu.make_async_copy(ys_hbm.at[pl.ds(dref[j * TOP_K + k], 1)],
                                     buf.at[s, k, pl.ds(j, 1)], sem.at[s])

    def issue(dref, s):
        def body(j, c):
            for k in range(TOP_K):
                row_copy(dref, s, j, k).start(priority=k % 2)
            return c
        lax.fori_loop(0, tokens, body, 0)

    @pl.when(i == 0)
    def _():
        issue(dcur_ref, 0)

    @pl.when(i + 1 < n)
    def _():
        issue(dnext_ref, 1 - slot)

    for k in range(TOP_K):
        pltpu.make_async_copy(ys_hbm.at[pl.ds(0, tokens)], buf.at[slot, k], sem.at[slot]).wait()

    acc = hs_ref[...]
    g = gate_ref[...]
    for k in range(TOP_K):
        acc = acc + g[:, k:k + 1] * buf[slot, k]
    out_ref[...] = _rms(acc, gf_ref[...])


def _combine(dest_flat, gate_tk, hs, gf, ys):
    t, d = hs.shape
    tokens = COMBINE_TOKENS
    n = t // tokens
    dspec = lambda f: pl.BlockSpec((tokens * TOP_K,), f, memory_space=pltpu.SMEM)
    return pl.pallas_call(
        _combine_kernel,
        grid=(n,),
        in_specs=[dspec(lambda i: (i,)), dspec(lambda i: (jnp.minimum(i + 1, n - 1),)),
                  pl.BlockSpec((tokens, TOP_K), lambda i: (i, 0)),
                  pl.BlockSpec((tokens, d), lambda i: (i, 0)),
                  pl.BlockSpec((1, d), lambda i: (0, 0)),
                  pl.BlockSpec(memory_space=pl.ANY)],
        out_specs=pl.BlockSpec((tokens, d), lambda i: (i, 0)),
        out_shape=jax.ShapeDtypeStruct((t, d), F32),
        scratch_shapes=[pltpu.VMEM((2, TOP_K, tokens, d), F32), pltpu.SemaphoreType.DMA((2,))],
        compiler_params=pltpu.CompilerParams(dimension_semantics=("arbitrary",),
                                             vmem_limit_bytes=VMEM_LIMIT, disable_bounds_checks=True),
        name="combine",
    )(dest_flat, dest_flat, gate_tk, hs, gf, ys)


def _moe_kernel(cnt_ref, off_ref, next_ref, rows_ref, gates_ref, hn_ref, hs_hbm, wgu_ref, wd_ref, gf_ref,
                out_ref, acc_scr, x_scr, y_scr, state_scr, sem, *, d_expert, n_slab):
    g = pl.program_id(0)
    e = pl.program_id(1)
    n_e = pl.num_programs(1)
    tokens = GROUP_TOKENS
    blk = EXPERT_ROWS
    batch = 16

    def slab(ref, row0):
        return ref.at[pl.ds(pl.multiple_of(row0, n_slab), n_slab), :]

    def acc_init():
        return pltpu.make_async_copy(hs_hbm.at[pl.ds(pl.multiple_of(g * tokens * n_slab, n_slab), tokens * n_slab), :],
                                     acc_scr.at[pl.ds(0, tokens * n_slab), :], sem)

    def gather(base):
        for r in range(blk):
            slab(x_scr, r * n_slab)[...] = slab(hn_ref, rows_ref[base + r])[...]

    def scatter(base):
        for j in range(blk // batch):
            rows, vals = [], []
            for u in range(batch):
                r = j * batch + u
                row0 = rows_ref[base + r]
                rows.append(row0)
                vals.append(slab(acc_scr, row0)[...] + slab(y_scr, r * n_slab)[...])
            for u in reversed(range(batch)):
                slab(acc_scr, rows[u])[...] = vals[u]

    n = cnt_ref[g * n_e + e]
    off = off_ref[g * n_e + e]

    @pl.when(e == 0)
    def _():
        acc_init().start()
        y_scr[...] = jnp.zeros_like(y_scr)
        state_scr[0] = 0
        gather(jnp.where(n > 0, off, next_ref[g * n_e + e]))
        acc_init().wait()

    n_blocks = (n + blk - 1) // blk
    nxt = next_ref[g * n_e + e]
    on_diag = (lax.broadcasted_iota(jnp.int32, (blk, blk), 0) == lax.broadcasted_iota(jnp.int32, (blk, blk), 1))

    def block(i, carry):
        base = off + i * blk
        xb = jnp.concatenate([x_scr[pl.ds(s, blk, stride=n_slab), :] for s in range(n_slab)], axis=1).astype(BF16)
        scatter(state_scr[0])
        gather(jnp.where(i + 1 < n_blocks, base + blk, nxt))
        gu = jnp.dot(xb, wgu_ref[0], preferred_element_type=F32)
        act = (_silu(gu[:, :d_expert]) * gu[:, d_expert:]).astype(BF16)
        y = jnp.dot(act, wd_ref[0], preferred_element_type=F32)
        gate_row = gates_ref[pl.ds(lax.shift_right_logical(base, blk.bit_length() - 1), 1), :]
        gate_col = jnp.sum(jnp.where(on_diag, gate_row, 0.0), axis=1, keepdims=True)
        y = y * gate_col
        for s in range(n_slab):
            y_scr[pl.ds(s, blk, stride=n_slab), :] = y[:, s * LANES:(s + 1) * LANES]
        state_scr[0] = base
        return carry

    lax.fori_loop(0, n_blocks, block, 0)

    @pl.when(e == n_e - 1)
    def _():
        scatter(state_scr[0])
        rows = 256
        for c in range(tokens // rows):
            acc = jnp.concatenate([acc_scr[pl.ds(c * rows * n_slab + s, rows, stride=n_slab), :]
                                   for s in range(n_slab)], axis=1)
            out_ref[c * rows:(c + 1) * rows, :] = _rms(acc, gf_ref[...])


def _moe(cnt_flat, off_flat, next_flat, rows_flat, gates2d, hn3, hs3, wgu, wd, gf):
    n_slab = gf.shape[1] // LANES
    t = hn3.shape[0] // n_slab
    d = gf.shape[1]
    n_experts, d_expert = wd.shape[0], wd.shape[1]
    tokens = GROUP_TOKENS
    n_groups = t // tokens
    list_len = rows_flat.shape[0] // n_groups
    assert EXPERT_ROWS == LANES and list_len % LANES == 0
    grid_spec = pltpu.PrefetchScalarGridSpec(
        num_scalar_prefetch=3,
        grid=(n_groups, n_experts),
        in_specs=[pl.BlockSpec((list_len,), lambda g, e, *_: (g,), memory_space=pltpu.SMEM),
                  pl.BlockSpec((list_len // LANES, LANES), lambda g, e, *_: (g, 0)),
                  pl.BlockSpec((tokens * n_slab, LANES), lambda g, e, *_: (g, 0)),
                  pl.BlockSpec(memory_space=pl.ANY),
                  pl.BlockSpec((1, d, 2 * d_expert), lambda g, e, *_: (e, 0, 0)),
                  pl.BlockSpec((1, d_expert, d), lambda g, e, *_: (e, 0, 0)),
                  pl.BlockSpec((1, d), lambda g, e, *_: (0, 0))],
        out_specs=pl.BlockSpec((tokens, d), lambda g, e, *_: (g, 0)),
        scratch_shapes=[pltpu.VMEM((tokens * n_slab, LANES), F32),
                        pltpu.VMEM((EXPERT_ROWS * n_slab, LANES), F32),
                        pltpu.VMEM((EXPERT_ROWS * n_slab, LANES), F32),
                        pltpu.SMEM((1,), jnp.int32),
                        pltpu.SemaphoreType.DMA],
    )
    return pl.pallas_call(
        functools.partial(_moe_kernel, d_expert=d_expert, n_slab=n_slab),
        grid_spec=grid_spec,
        out_shape=jax.ShapeDtypeStruct((t, d), F32),
        compiler_params=pltpu.CompilerParams(dimension_semantics=("arbitrary", "arbitrary"),
                                             vmem_limit_bytes=VMEM_LIMIT),
        name="moe",
    )(cnt_flat, off_flat, next_flat, rows_flat, gates2d, hn3, hs3, wgu, wd, gf)


def kernel(x, norm1_g, w_in, sgu_ln_g, sgu_ln_b, w_spatial, b_spatial, out_norm_a_g, out_norm_b_g, w_out,
           norm2_g, w_router, router_bias, w_gate, w_up, w_down, ws_gate, ws_up, ws_down, final_norm_g):
    bsz, s, d = x.shape
    depth = w_in.shape[0]
    assert depth == 1, "the fused final norm assumes a single trunk layer"
    l = 0
    t = bsz * s
    d_a = sgu_ln_g.shape[-1]
    d_b = out_norm_b_g.shape[-1]
    n_heads, chunk = w_spatial.shape[1], w_spatial.shape[2]
    n_experts = w_router.shape[-1]
    assert d_a == n_heads * A_HEAD_DIM and chunk == LANES and 2 * A_HEAD_DIM == LANES
    assert t % FRONT_ROWS == 0 and FRONT_ROWS % chunk == 0 and s % FRONT_ROWS == 0
    assert t % DISPATCH_TOKENS == 0 and t % COMBINE_TOKENS == 0
    row = lambda v: v.reshape(1, -1).astype(F32)

    x2 = x.reshape(t, d)

    win = w_in[l].astype(BF16)
    ws = w_spatial[l]
    ws_pair = jnp.concatenate([ws[0::2], ws[1::2]], axis=2).astype(BF16)
    bs_full = jnp.repeat(b_spatial[l].T, A_HEAD_DIM, axis=1).astype(F32)
    woa = w_out[l][:d_a].astype(BF16)
    wob = w_out[l][d_a:].astype(BF16)
    wrt = w_router[l].T.astype(F32)
    rb = jnp.broadcast_to(router_bias[l].astype(F32)[:, None], (n_experts, 1))
    wsgu = jnp.concatenate([ws_gate[l], ws_up[l]], axis=1).astype(BF16)
    wsd = ws_down[l].astype(BF16)
    wgu = jnp.concatenate([w_gate[l], w_up[l]], axis=2).astype(BF16)
    wd = w_down[l].astype(BF16)

    an, zb = _front(x2, row(norm1_g[l]), win, row(sgu_ln_g[l]), row(sgu_ln_b[l]), ws_pair, bs_full,
                    row(out_norm_a_g[l]), d_a=d_a, d_b=d_b, chunk=chunk)
    bn = _fnet(zb, row(out_norm_b_g[l]), bsz).reshape(t, d_b)

    hs3, hn3, eidx, gate, rank, cnt = _post(x2, an, bn, woa, wob, row(norm2_g[l]), wrt, rb, wsgu, wsd)

    n_groups = t // GROUP_TOKENS
    blk = EXPERT_ROWS
    counts = cnt[:, :, 0]
    n_pad = (-counts) % blk
    padded = counts + n_pad
    offs = jnp.cumsum(padded, axis=1) - padded
    tok_bits = (GROUP_TOKENS * TOP_K).bit_length()
    local = jnp.arange(GROUP_TOKENS * TOP_K, dtype=jnp.int32).reshape(1, GROUP_TOKENS, TOP_K)
    eidx_g = eidx.reshape(TOP_K, n_groups, GROUP_TOKENS).transpose(1, 2, 0)
    gate_g = gate.reshape(TOP_K, n_groups, GROUP_TOKENS).transpose(1, 2, 0)
    real_keys = ((eidx_g << tok_bits) | local).reshape(n_groups, -1)
    real_rows = jnp.broadcast_to(local // TOP_K * (d // LANES), eidx_g.shape).reshape(n_groups, -1)
    pad_j = jnp.arange(blk - 1, dtype=jnp.int32)[None, None, :]
    e_col = jnp.arange(n_experts, dtype=jnp.int32)[None, :, None]
    pad_keys = jnp.where(pad_j < n_pad[:, :, None], (e_col << tok_bits) | (GROUP_TOKENS * TOP_K + pad_j),
                         jnp.iinfo(jnp.int32).max).reshape(n_groups, -1)
    keys = jnp.concatenate([real_keys, pad_keys], axis=1)
    rows_in = jnp.concatenate([real_rows, jnp.zeros_like(pad_keys)], axis=1)
    gates_in = jnp.concatenate([gate_g.reshape(n_groups, -1), jnp.zeros(pad_keys.shape, F32)], axis=1)
    _, rows_sorted, gates_sorted = lax.sort((keys, rows_in, gates_in), dimension=1, num_keys=1)
    list_len = -(-keys.shape[1] // LANES) * LANES
    fill = ((0, 0), (0, list_len - keys.shape[1]))
    rows_flat = jnp.pad(rows_sorted, fill).reshape(-1)
    gates2d = jnp.pad(gates_sorted, fill).reshape(n_groups * (list_len // LANES), LANES)
    e_row = jnp.arange(n_experts, dtype=jnp.int32)
    later = (e_row[None, None, :] > e_row[None, :, None]) & (counts[:, None, :] > 0)
    first_later = jnp.min(jnp.where(later, e_row[None, None, :], n_experts), axis=2)
    next_off = jnp.sum(jnp.where(first_later[:, :, None] == e_row[None, None, :], offs[:, None, :], 0), axis=2)
    flat = lambda a: a.reshape(-1).astype(jnp.int32)

    out = _moe(flat(counts), flat(offs), flat(next_off), rows_flat, gates2d,
               hn3, hs3, wgu, wd, row(final_norm_g))
    return out.reshape(bsz, s, d)
```

```python
import functools
import math

import numpy as np
import jax
import jax.numpy as jnp
from jax import lax
from jax.experimental import pallas as pl
from jax.experimental.pallas import tpu as pltpu

A_HEAD_DIM = 64
B_GROUP_DIM = 64
TOP_K = 8
N_EXPERT_GROUPS = 8
TOPK_GROUPS = 4
ROUTED_SCALE = 2.5
EPS = 1e-6

LANES = 128
FRONT_ROWS = 512
EXPERT_ROWS = 384
DISPATCH_TOKENS = 512
COMBINE_TOKENS = 128
GROUP_TOKENS = 2048
FFT_RADIX = 64
FFT_PAD = 8
VMEM_LIMIT = 56 * 1024 * 1024

F32 = jnp.float32
BF16 = jnp.bfloat16


def _rms(x, g):
    return x * lax.rsqrt(jnp.mean(x * x, axis=-1, keepdims=True) + EPS) * g


def _gelu(x):
    return 0.5 * x * (1.0 + lax.erf(x * (1.0 / math.sqrt(2.0))))


def _silu(x):
    return x / (1.0 + jnp.exp(-x))


def _front_kernel(x_ref, g1_ref, win_ref, lng_ref, lnb_ref, ws_ref, bs_ref, ga_ref,
                  an_ref, zb_ref, ya_scr, *, d_a, chunk):
    x = x_ref[...]
    xn = _rms(x, g1_ref[...]).astype(BF16)
    z = jnp.dot(xn, win_ref[...], preferred_element_type=F32)
    for q in range(zb_ref.shape[0]):
        zb_ref[q] = z[:, 2 * d_a + q * LANES:2 * d_a + (q + 1) * LANES].astype(BF16)
    u = _gelu(z[:, :d_a])
    v = _gelu(z[:, d_a:2 * d_a])
    mu = jnp.mean(v, axis=-1, keepdims=True)
    vc = v - mu
    v = vc * lax.rsqrt(jnp.mean(vc * vc, axis=-1, keepdims=True) + EPS) * lng_ref[...] + lnb_ref[...]
    rows = x.shape[0]
    lane = lax.broadcasted_iota(jnp.int32, (chunk, LANES), 1)
    low = lane < A_HEAD_DIM
    for c in range(rows // chunk):
        r0 = c * chunk
        for hp in range(d_a // LANES):
            c0 = hp * LANES
            vp = v[r0:r0 + chunk, c0:c0 + LANES]
            rhs = jnp.concatenate([jnp.where(low, vp, 0.0), jnp.where(low, 0.0, vp)], axis=0).astype(BF16)
            mixed = jnp.dot(ws_ref[hp], rhs, preferred_element_type=F32) + bs_ref[:, c0:c0 + LANES]
            ya_scr[r0:r0 + chunk, c0:c0 + LANES] = u[r0:r0 + chunk, c0:c0 + LANES] * mixed
    an_ref[...] = _rms(ya_scr[...], ga_ref[...]).astype(BF16)


def _front(x2, g1, win, lng, lnb, ws_pair, bs_full, ga, *, d_a, d_b, chunk):
    t, d = x2.shape
    d_in = win.shape[1]
    rows = FRONT_ROWS
    const = lambda shape: pl.BlockSpec(shape, lambda i: (0,) * len(shape))
    return pl.pallas_call(
        functools.partial(_front_kernel, d_a=d_a, chunk=chunk),
        grid=(t // rows,),
        in_specs=[
            pl.BlockSpec((rows, d), lambda i: (i, 0)),
            const((1, d)), const((d, d_in)), const((1, d_a)), const((1, d_a)),
            const(ws_pair.shape), const(bs_full.shape), const((1, d_a)),
        ],
        out_specs=[pl.BlockSpec((rows, d_a), lambda i: (i, 0)),
                   pl.BlockSpec((d_b // LANES, rows, LANES), lambda i: (0, i, 0))],
        out_shape=[jax.ShapeDtypeStruct((t, d_a), BF16), jax.ShapeDtypeStruct((d_b // LANES, t, LANES), BF16)],
        scratch_shapes=[pltpu.VMEM((rows, d_a), F32)],
        compiler_params=pltpu.CompilerParams(dimension_semantics=("parallel",),
                                             vmem_limit_bytes=VMEM_LIMIT),
        name="front",
    )(x2, g1, win, lng, lnb, ws_pair, bs_full, ga)


def _fnet_tables(seq, half_w):
    r = FFT_RADIX
    assert seq == r * r
    j = np.arange(B_GROUP_DIM)
    ang = 2.0 * np.pi * np.outer(j, j) / B_GROUP_DIM
    ng = half_w // B_GROUP_DIM
    eye = np.eye(ng)
    m_ch = np.concatenate([np.kron(eye, np.cos(ang)), np.kron(eye, np.sin(ang))], axis=1)
    k1 = np.arange(r)[:, None]
    n1 = np.arange(r)[None, :]
    d_tw = np.zeros((r, 2 * r, 2 * r))
    for n2 in range(r):
        th = 2.0 * np.pi * (k1 * n1 / r + k1 * n2 / (r * r))
        c, s = np.cos(th), np.sin(th)
        d_tw[n2] = np.block([[c, -s], [s, c]])
    ph = 2.0 * np.pi * np.outer(np.arange(r), np.arange(r)) / r
    scale = 1.0 / math.sqrt(seq * B_GROUP_DIM)
    d2 = np.concatenate([np.cos(ph), -np.sin(ph)], axis=1) * scale
    return (jnp.asarray(m_ch, F32), jnp.asarray(d_tw, F32), jnp.asarray(d2, F32))


def _fnet_kernel(z_ref, mch32_ref, dtw_ref, d232_ref, gb_ref, out_ref, w_scr, a2_scr, y_scr, mch_ref, d2_ref,
                 *, n_col, unroll):
    r = FFT_RADIX
    rp = r + FFT_PAD
    ap = 2 * r + FFT_PAD
    hf = pl.program_id(1)
    half_w = n_col * LANES
    mch_ref[...] = mch32_ref[...].astype(BF16)
    d2_ref[...] = d232_ref[...].astype(BF16)

    chunk = 8 * r
    for c in range(r * r // chunk):
        zc = jnp.concatenate([z_ref[q, c * chunk:(c + 1) * chunk, :] for q in range(n_col)], axis=1)
        w = jnp.dot(zc, mch_ref[...], preferred_element_type=F32)
        for j in range(chunk // r):
            n1 = c * (chunk // r) + j
            for p in range(2 * n_col):
                w_scr[p, n1 * rp:n1 * rp + r, :] = w[j * r:(j + 1) * r, p * LANES:(p + 1) * LANES]

    def stage1(it, carry):
        for u in range(unroll):
            n2 = it * unroll + u
            rhs = jnp.concatenate(
                [jnp.concatenate([w_scr[ri * n_col + q, pl.ds(n2, r, stride=rp), :] for q in range(n_col)], axis=1)
                 for ri in range(2)], axis=0).astype(BF16)
            a = jnp.dot(dtw_ref[n2].astype(BF16), rhs, preferred_element_type=F32)
            for q in range(n_col):
                a2_scr[q, pl.ds(n2, r, stride=ap), :] = a[:r, q * LANES:(q + 1) * LANES]
                a2_scr[q, pl.ds(r + n2, r, stride=ap), :] = a[r:, q * LANES:(q + 1) * LANES]
        return carry

    lax.fori_loop(0, r // unroll, stage1, 0)

    def stage2(it, carry):
        for u in range(unroll):
            k1 = it * unroll + u
            row0 = pl.multiple_of(k1 * ap, 8)
            blk = jnp.concatenate([a2_scr[q, pl.ds(row0, 2 * r), :] for q in range(n_col)],
                                  axis=1).astype(BF16)
            y = jnp.dot(d2_ref[...], blk, preferred_element_type=F32)
            for q in range(n_col):
                y_scr[hf * n_col + q, pl.ds(k1, r, stride=rp), :] = y[:, q * LANES:(q + 1) * LANES]
        return carry

    lax.fori_loop(0, r // unroll, stage2, 0)

    @pl.when(hf == pl.num_programs(1) - 1)
    def _():
        n_all = y_scr.shape[0]
        for k2 in range(r):
            ys = [y_scr[q, k2 * rp:k2 * rp + r, :] for q in range(n_all)]
            ss = sum(jnp.sum(y * y, axis=-1, keepdims=True) for y in ys)
            inv = lax.rsqrt(ss * (1.0 / (n_all * LANES)) + EPS)
            for q in range(n_all):
                out_ref[0, k2 * r:(k2 + 1) * r, q * LANES:(q + 1) * LANES] = (
                    ys[q] * inv * gb_ref[:, q * LANES:(q + 1) * LANES]).astype(BF16)


def _fnet(zb_cols, gb, bsz):
    n_all, t, _ = zb_cols.shape
    s = t // bsz
    d_b = n_all * LANES
    n_col = 2
    r = FFT_RADIX
    rp, ap = r + FFT_PAD, 2 * r + FFT_PAD
    m_ch, d_tw, d2 = _fnet_tables(s, n_col * LANES)
    return pl.pallas_call(
        functools.partial(_fnet_kernel, n_col=n_col, unroll=8),
        grid=(bsz, n_all // n_col),
        in_specs=[
            pl.BlockSpec((n_col, s, LANES), lambda i, h: (h, i, 0)),
            pl.BlockSpec(m_ch.shape, lambda i, h: (0, 0)),
            pl.BlockSpec(d_tw.shape, lambda i, h: (0, 0, 0)),
            pl.BlockSpec(d2.shape, lambda i, h: (0, 0)),
            pl.BlockSpec((1, d_b), lambda i, h: (0, 0)),
        ],
        out_specs=pl.BlockSpec((1, s, d_b), lambda i, h: (i, 0, 0)),
        out_shape=jax.ShapeDtypeStruct((bsz, s, d_b), BF16),
        scratch_shapes=[pltpu.VMEM((2 * n_col, r * rp, LANES), F32), pltpu.VMEM((n_col, r * ap, LANES), F32),
                        pltpu.VMEM((n_all, r * rp, LANES), F32),
                        pltpu.VMEM(m_ch.shape, BF16), pltpu.VMEM(d2.shape, BF16)],
        compiler_params=pltpu.CompilerParams(dimension_semantics=("parallel", "arbitrary"),
                                             vmem_limit_bytes=VMEM_LIMIT),
        name="fnet",
    )(zb_cols, m_ch, d_tw, d2, gb)


def _post_kernel(x_ref, an_ref, bn_ref, woa_ref, wob_ref, g2_ref, wrt_ref, rb_ref, wsgu_ref, wsd_ref,
                 tri_ref, hs_ref, hn_ref, eidx_ref, gate_ref, rank_ref, cnt_ref, carry_scr,
                 *, n_experts, d_shared, steps_per_group):
    i = pl.program_id(0)

    @pl.when(i % steps_per_group == 0)
    def _():
        carry_scr[...] = jnp.zeros_like(carry_scr)

    h = (x_ref[...]
         + jnp.dot(an_ref[...], woa_ref[...], preferred_element_type=F32)
         + jnp.dot(bn_ref[...], wob_ref[...], preferred_element_type=F32))
    hn = _rms(h, g2_ref[...])
    hnb = hn.astype(BF16)
    gu = jnp.dot(hnb, wsgu_ref[...], preferred_element_type=F32)
    act = (_silu(gu[:, :d_shared]) * gu[:, d_shared:]).astype(BF16)
    hs = h + jnp.dot(act, wsd_ref[...], preferred_element_type=F32)
    n_slab = hn.shape[1] // LANES
    for s in range(n_slab):
        hn_ref[pl.ds(s, hn.shape[0], stride=n_slab), :] = hn[:, s * LANES:(s + 1) * LANES]
        hs_ref[pl.ds(s, hn.shape[0], stride=n_slab), :] = hs[:, s * LANES:(s + 1) * LANES]

    rows = hn.shape[0]
    eg = n_experts // N_EXPERT_GROUPS
    logits_t = lax.dot_general(wrt_ref[...], hn, (((1,), (1,)), ((), ())),
                               precision=lax.Precision.HIGHEST, preferred_element_type=F32)
    scores_t = 1.0 / (1.0 + jnp.exp(-logits_t))
    biased_t = scores_t + rb_ref[...]
    sub = lax.broadcasted_iota(jnp.int32, (eg, rows), 0)
    neg = -jnp.inf
    grp = [biased_t[g * eg:(g + 1) * eg, :] for g in range(N_EXPERT_GROUPS)]
    sco = [scores_t[g * eg:(g + 1) * eg, :] for g in range(N_EXPERT_GROUPS)]
    gid = [sub + g * eg for g in range(N_EXPERT_GROUPS)]

    gscore = []
    for g in range(N_EXPERT_GROUPS):
        m1 = jnp.max(grp[g], axis=0, keepdims=True)
        first = jnp.min(jnp.where(grp[g] == m1, sub, eg), axis=0, keepdims=True)
        m2 = jnp.max(jnp.where(sub == first, neg, grp[g]), axis=0, keepdims=True)
        gscore.append(m1 + m2)
    masked = []
    for g in range(N_EXPERT_GROUPS):
        beaten = jnp.zeros((1, rows), jnp.int32)
        for j in range(N_EXPERT_GROUPS):
            if j == g:
                continue
            wins = (gscore[j] > gscore[g]) if j > g else (gscore[j] >= gscore[g])
            beaten = beaten + wins.astype(jnp.int32)
        masked.append(jnp.where(beaten < TOPK_GROUPS, grp[g], neg))

    sel = [jnp.zeros((eg, rows), F32) for _ in range(N_EXPERT_GROUPS)]
    e_rows, w_rows = [], []
    for _ in range(TOP_K):
        m = masked[0]
        for g in range(1, N_EXPERT_GROUPS):
            m = jnp.maximum(m, masked[g])
        mk = jnp.max(m, axis=0, keepdims=True)
        cand = jnp.where(masked[0] == mk, gid[0], n_experts)
        for g in range(1, N_EXPERT_GROUPS):
            cand = jnp.minimum(cand, jnp.where(masked[g] == mk, gid[g], n_experts))
        ik = jnp.min(cand, axis=0, keepdims=True)
        wk = jnp.zeros((eg, rows), F32)
        for g in range(N_EXPERT_GROUPS):
            hit = gid[g] == ik
            masked[g] = jnp.where(hit, neg, masked[g])
            sel[g] = jnp.where(hit, 1.0, sel[g])
            wk = wk + jnp.where(hit, sco[g], 0.0)
        e_rows.append(ik)
        w_rows.append(jnp.sum(wk, axis=0, keepdims=True))
    wsum = w_rows[0]
    for k in range(1, TOP_K):
        wsum = wsum + w_rows[k]
    gates = [w_rows[k] / wsum * ROUTED_SCALE for k in range(TOP_K)]

    sel_t = jnp.concatenate(sel, axis=0)
    within = jnp.dot(sel_t.astype(BF16), tri_ref[...], preferred_element_type=F32)
    posn = within + carry_scr[:, 0:1]
    r_rows = []
    for k in range(TOP_K):
        acc = jnp.zeros((eg, rows), F32)
        for g in range(N_EXPERT_GROUPS):
            acc = acc + jnp.where(gid[g] == e_rows[k], posn[g * eg:(g + 1) * eg, :], 0.0)
        r_rows.append(jnp.sum(acc, axis=0, keepdims=True))
    new_carry = carry_scr[...] + jnp.sum(sel_t, axis=1, keepdims=True)
    carry_scr[...] = new_carry
    cnt_ref[0] = new_carry.astype(jnp.int32)
    eidx_ref[...] = jnp.concatenate(e_rows, axis=0)
    gate_ref[...] = jnp.concatenate(gates, axis=0)
    rank_ref[...] = jnp.concatenate(r_rows, axis=0).astype(jnp.int32)


def _post(x2, an, bn, woa, wob, g2, wrt, rb, wsgu, wsd):
    t, d = x2.shape
    rows = FRONT_ROWS
    n_experts = wrt.shape[0]
    d_shared = wsd.shape[0]
    n_slab = d // LANES
    steps_per_group = GROUP_TOKENS // rows
    n_groups = t // GROUP_TOKENS
    tri = jnp.asarray(np.triu(np.ones((rows, rows), np.float32), k=1), BF16)
    const = lambda shape: pl.BlockSpec(shape, lambda i: (0,) * len(shape))
    row_blk = lambda w: pl.BlockSpec((rows, w), lambda i: (i, 0))
    slab_blk = pl.BlockSpec((rows * n_slab, LANES), lambda i: (i, 0))
    tok_blk = pl.BlockSpec((TOP_K, rows), lambda i: (0, i))
    return pl.pallas_call(
        functools.partial(_post_kernel, n_experts=n_experts, d_shared=d_shared,
                          steps_per_group=steps_per_group),
        grid=(t // rows,),
        in_specs=[row_blk(d), row_blk(an.shape[1]), row_blk(bn.shape[1]),
                  const(woa.shape), const(wob.shape), const((1, d)), const(wrt.shape), const(rb.shape),
                  const(wsgu.shape), const(wsd.shape), const(tri.shape)],
        out_specs=[slab_blk, slab_blk, tok_blk, tok_blk, tok_blk,
                   pl.BlockSpec((1, n_experts, LANES), lambda i: (i // steps_per_group, 0, 0))],
        out_shape=[jax.ShapeDtypeStruct((t * n_slab, LANES), F32), jax.ShapeDtypeStruct((t * n_slab, LANES), F32),
                   jax.ShapeDtypeStruct((TOP_K, t), jnp.int32), jax.ShapeDtypeStruct((TOP_K, t), F32),
                   jax.ShapeDtypeStruct((TOP_K, t), jnp.int32),
                   jax.ShapeDtypeStruct((n_groups, n_experts, LANES), jnp.int32)],
        scratch_shapes=[pltpu.VMEM((n_experts, LANES), F32)],
        compiler_params=pltpu.CompilerParams(dimension_semantics=("arbitrary",),
                                             vmem_limit_bytes=VMEM_LIMIT),
        name="post",
    )(x2, an, bn, woa, wob, g2, wrt, rb, wsgu, wsd, tri)


def _dispatch_kernel(pend_ref, dest_ref, hn_ref, xs_hbm, zero_scr, zsem, sem, *, n_experts):
    i = pl.program_id(0)
    blk = EXPERT_ROWS
    tokens = DISPATCH_TOKENS

    def zero_copy(e):
        start = pl.multiple_of(jnp.maximum(pend_ref[e] - blk, 0), blk)
        return pltpu.make_async_copy(zero_scr, xs_hbm.at[pl.ds(start, blk)], zsem)

    @pl.when(i == 0)
    def _():
        zero_scr[...] = jnp.zeros_like(zero_scr)

        def start_body(e, c):
            zero_copy(e).start()
            return c

        def wait_body(e, c):
            zero_copy(e).wait()
            return c

        lax.fori_loop(0, n_experts, start_body, 0)
        lax.fori_loop(0, n_experts, wait_body, 0)

        def tail_copy(b):
            return pltpu.make_async_copy(zero_scr, xs_hbm.at[pl.ds(pl.multiple_of(b * blk, blk), blk)], zsem)

        def tail_start(b, c):
            tail_copy(b).start()
            return c

        def tail_wait(b, c):
            tail_copy(b).wait()
            return c

        first_free = pend_ref[n_experts - 1] // blk
        lax.fori_loop(first_free, xs_hbm.shape[0] // blk, tail_start, 0)
        lax.fori_loop(first_free, xs_hbm.shape[0] // blk, tail_wait, 0)

    def issue(j, c):
        for k in range(TOP_K):
            pltpu.make_async_copy(hn_ref.at[pl.ds(j, 1)], xs_hbm.at[pl.ds(dest_ref[j * TOP_K + k], 1)],
                                  sem).start(priority=k % 2)
        return c

    lax.fori_loop(0, tokens, issue, 0)
    for k in range(TOP_K):
        pltpu.make_async_copy(hn_ref, xs_hbm.at[pl.ds(0, tokens)], sem).wait()


def _dispatch(pad_end, dest_flat, hn, n_rows):
    t, d = hn.shape
    n_experts = pad_end.shape[0]
    tokens = DISPATCH_TOKENS
    grid_spec = pltpu.PrefetchScalarGridSpec(
        num_scalar_prefetch=1,
        grid=(t // tokens,),
        in_specs=[pl.BlockSpec((tokens * TOP_K,), lambda i, pe: (i,), memory_space=pltpu.SMEM),
                  pl.BlockSpec((tokens, d), lambda i, pe: (i, 0))],
        out_specs=pl.BlockSpec(memory_space=pl.ANY),
        scratch_shapes=[pltpu.VMEM((EXPERT_ROWS, d), F32), pltpu.SemaphoreType.DMA, pltpu.SemaphoreType.DMA],
    )
    return pl.pallas_call(
        functools.partial(_dispatch_kernel, n_experts=n_experts),
        grid_spec=grid_spec,
        out_shape=jax.ShapeDtypeStruct((n_rows, d), F32),
        compiler_params=pltpu.CompilerParams(dimension_semantics=("arbitrary",), has_side_effects=True,
                                             disable_bounds_checks=True),
        name="dispatch",
    )(pad_end, dest_flat, hn)


def _expert_kernel(be_ref, nu_ref, xs_ref, wgu_ref, wd_ref, ys_ref, *, d_expert):
    i = pl.program_id(0)

    @pl.when(i < nu_ref[0])
    def _():
        xb = xs_ref[...].astype(BF16)
        gu = jnp.dot(xb, wgu_ref[0], preferred_element_type=F32)
        act = (_silu(gu[:, :d_expert]) * gu[:, d_expert:]).astype(BF16)
        ys_ref[...] = jnp.dot(act, wd_ref[0], preferred_element_type=F32)

    @pl.when(i >= nu_ref[0])
    def _():
        ys_ref[...] = jnp.zeros_like(ys_ref)


def _experts(block_e, n_used, xs, wgu, wd):
    n_rows, d = xs.shape
    blk = EXPERT_ROWS
    d_expert = wd.shape[1]
    grid_spec = pltpu.PrefetchScalarGridSpec(
        num_scalar_prefetch=2,
        grid=(n_rows // blk,),
        in_specs=[pl.BlockSpec((blk, d), lambda i, be, nu: (jnp.minimum(i, nu[0] - 1), 0)),
                  pl.BlockSpec((1, d, 2 * d_expert), lambda i, be, nu: (be[i], 0, 0)),
                  pl.BlockSpec((1, d_expert, d), lambda i, be, nu: (be[i], 0, 0))],
        out_specs=pl.BlockSpec((blk, d), lambda i, be, nu: (i, 0)),
    )
    return pl.pallas_call(
        functools.partial(_expert_kernel, d_expert=d_expert),
        grid_spec=grid_spec,
        out_shape=jax.ShapeDtypeStruct((n_rows, d), F32),
        compiler_params=pltpu.CompilerParams(dimension_semantics=("arbitrary",),
                                             vmem_limit_bytes=VMEM_LIMIT),
        name="experts",
    )(block_e, n_used, xs, wgu, wd)


def _combine_kernel(dcur_ref, dnext_ref, gate_ref, hs_ref, gf_ref, ys_hbm, out_ref, buf, sem):
    i = pl.program_id(0)
    n = pl.num_programs(0)
    tokens = COMBINE_TOKENS
    slot = i % 2

    def row_copy(dref, s, j, k):
        return pltpu.make_async_copy(ys_hbm.at[pl.ds(dref[j * TOP_K + k], 1)],
                                     buf.at[s, k, pl.ds(j, 1)], sem.at[s])

    def issue(dref, s):
        def body(j, c):
            for k in range(TOP_K):
                row_copy(dref, s, j, k).start(priority=k % 2)
            return c
        lax.fori_loop(0, tokens, body, 0)

    @pl.when(i == 0)
    def _():
        issue(dcur_ref, 0)

    @pl.when(i + 1 < n)
    def _():
        issue(dnext_ref, 1 - slot)

    for k in range(TOP_K):
        pltpu.make_async_copy(ys_hbm.at[pl.ds(0, tokens)], buf.at[slot, k], sem.at[slot]).wait()

    acc = hs_ref[...]
    g = gate_ref[...]
    for k in range(TOP_K):
        acc = acc + g[:, k:k + 1] * buf[slot, k]
    out_ref[...] = _rms(acc, gf_ref[...])


def _combine(dest_flat, gate_tk, hs, gf, ys):
    t, d = hs.shape
    tokens = COMBINE_TOKENS
    n = t // tokens
    dspec = lambda f: pl.BlockSpec((tokens * TOP_K,), f, memory_space=pltpu.SMEM)
    return pl.pallas_call(
        _combine_kernel,
        grid=(n,),
        in_specs=[dspec(lambda i: (i,)), dspec(lambda i: (jnp.minimum(i + 1, n - 1),)),
                  pl.BlockSpec((tokens, TOP_K), lambda i: (i, 0)),
                  pl.BlockSpec((tokens, d), lambda i: (i, 0)),
                  pl.BlockSpec((1, d), lambda i: (0, 0)),
                  pl.BlockSpec(memory_space=pl.ANY)],
        out_specs=pl.BlockSpec((tokens, d), lambda i: (i, 0)),
        out_shape=jax.ShapeDtypeStruct((t, d), F32),
        scratch_shapes=[pltpu.VMEM((2, TOP_K, tokens, d), F32), pltpu.SemaphoreType.DMA((2,))],
        compiler_params=pltpu.CompilerParams(dimension_semantics=("arbitrary",),
                                             vmem_limit_bytes=VMEM_LIMIT, disable_bounds_checks=True),
        name="combine",
    )(dest_flat, dest_flat, gate_tk, hs, gf, ys)


def _moe_kernel(cnt_ref, off_ref, next_ref, rows_ref, gates_ref, hn_ref, hs_hbm, wgu_ref, wd_ref, gf_ref,
                out_ref, acc_scr, x_scr, y_scr, state_scr, sem, *, d_expert, n_slab):
    g = pl.program_id(0)
    e = pl.program_id(1)
    n_e = pl.num_programs(1)
    tokens = GROUP_TOKENS
    blk = EXPERT_ROWS
    batch = 16

    def slab(ref, row0):
        return ref.at[pl.ds(pl.multiple_of(row0, n_slab), n_slab), :]

    def acc_init():
        return pltpu.make_async_copy(hs_hbm.at[pl.ds(pl.multiple_of(g * tokens * n_slab, n_slab), tokens * n_slab), :],
                                     acc_scr.at[pl.ds(0, tokens * n_slab), :], sem)

    def gather(base):
        for r in range(blk):
            slab(x_scr, r * n_slab)[...] = slab(hn_ref, rows_ref[base + r])[...]

    def scatter(base):
        for j in range(blk // batch):
            rows, vals = [], []
            for u in range(batch):
                r = j * batch + u
                row0 = rows_ref[base + r]
                rows.append(row0)
                vals.append(slab(acc_scr, row0)[...] + slab(y_scr, r * n_slab)[...])
            for u in reversed(range(batch)):
                slab(acc_scr, rows[u])[...] = vals[u]

    n = cnt_ref[g * n_e + e]
    off = off_ref[g * n_e + e]

    @pl.when(e == 0)
    def _():
        acc_init().start()
        y_scr[...] = jnp.zeros_like(y_scr)
        state_scr[0] = 0
        gather(jnp.where(n > 0, off, next_ref[g * n_e + e]))
        acc_init().wait()

    n_blocks = (n + blk - 1) // blk
    nxt = next_ref[g * n_e + e]
    on_diag = (lax.broadcasted_iota(jnp.int32, (LANES, LANES), 0)
               == lax.broadcasted_iota(jnp.int32, (LANES, LANES), 1))
    row_id = lax.broadcasted_iota(jnp.int32, (blk, 1), 0)

    def block(i, carry):
        base = off + i * blk
        xb = jnp.concatenate([x_scr[pl.ds(s, blk, stride=n_slab), :] for s in range(n_slab)], axis=1).astype(BF16)
        scatter(state_scr[0])
        gather(jnp.where(i + 1 < n_blocks, base + blk, nxt))
        gu = jnp.dot(xb, wgu_ref[0], preferred_element_type=F32)
        act = (_silu(gu[:, :d_expert]) * gu[:, d_expert:]).astype(BF16)
        y = jnp.dot(act, wd_ref[0], preferred_element_type=F32)
        first_row = lax.shift_right_logical(base, LANES.bit_length() - 1)
        cols = []
        for j in range(blk // LANES):
            gate_row = gates_ref[pl.ds(first_row + j, 1), :]
            cols.append(jnp.sum(jnp.where(on_diag, gate_row, 0.0), axis=1, keepdims=True))
        gate_col = cols[0] if len(cols) == 1 else jnp.concatenate(cols, axis=0)
        y = jnp.where(row_id < n - i * blk, y * gate_col, 0.0)
        for s in range(n_slab):
            y_scr[pl.ds(s, blk, stride=n_slab), :] = y[:, s * LANES:(s + 1) * LANES]
        state_scr[0] = base
        return carry

    lax.fori_loop(0, n_blocks, block, 0)

    @pl.when(e == n_e - 1)
    def _():
        scatter(state_scr[0])
        rows = 256
        for c in range(tokens // rows):
            acc = jnp.concatenate([acc_scr[pl.ds(c * rows * n_slab + s, rows, stride=n_slab), :]
                                   for s in range(n_slab)], axis=1)
            out_ref[c * rows:(c + 1) * rows, :] = _rms(acc, gf_ref[...])


def _moe(cnt_flat, off_flat, next_flat, rows_flat, gates2d, hn3, hs3, wgu, wd, gf):
    n_slab = gf.shape[1] // LANES
    t = hn3.shape[0] // n_slab
    d = gf.shape[1]
    n_experts, d_expert = wd.shape[0], wd.shape[1]
    tokens = GROUP_TOKENS
    n_groups = t // tokens
    list_len = rows_flat.shape[0] // n_groups
    assert EXPERT_ROWS % LANES == 0 and list_len % (8 * LANES) == 0
    grid_spec = pltpu.PrefetchScalarGridSpec(
        num_scalar_prefetch=3,
        grid=(n_groups, n_experts),
        in_specs=[pl.BlockSpec((list_len,), lambda g, e, *_: (g,), memory_space=pltpu.SMEM),
                  pl.BlockSpec((list_len // LANES, LANES), lambda g, e, *_: (g, 0)),
                  pl.BlockSpec((tokens * n_slab, LANES), lambda g, e, *_: (g, 0)),
                  pl.BlockSpec(memory_space=pl.ANY),
                  pl.BlockSpec((1, d, 2 * d_expert), lambda g, e, *_: (e, 0, 0)),
                  pl.BlockSpec((1, d_expert, d), lambda g, e, *_: (e, 0, 0)),
                  pl.BlockSpec((1, d), lambda g, e, *_: (0, 0))],
        out_specs=pl.BlockSpec((tokens, d), lambda g, e, *_: (g, 0)),
        scratch_shapes=[pltpu.VMEM((tokens * n_slab, LANES), F32),
                        pltpu.VMEM((EXPERT_ROWS * n_slab, LANES), F32),
                        pltpu.VMEM((EXPERT_ROWS * n_slab, LANES), F32),
                        pltpu.SMEM((1,), jnp.int32),
                        pltpu.SemaphoreType.DMA],
    )
    return pl.pallas_call(
        functools.partial(_moe_kernel, d_expert=d_expert, n_slab=n_slab),
        grid_spec=grid_spec,
        out_shape=jax.ShapeDtypeStruct((t, d), F32),
        compiler_params=pltpu.CompilerParams(dimension_semantics=("arbitrary", "arbitrary"),
                                             vmem_limit_bytes=VMEM_LIMIT),
        name="moe",
    )(cnt_flat, off_flat, next_flat, rows_flat, gates2d, hn3, hs3, wgu, wd, gf)


def kernel(x, norm1_g, w_in, sgu_ln_g, sgu_ln_b, w_spatial, b_spatial, out_norm_a_g, out_norm_b_g, w_out,
           norm2_g, w_router, router_bias, w_gate, w_up, w_down, ws_gate, ws_up, ws_down, final_norm_g):
    bsz, s, d = x.shape
    depth = w_in.shape[0]
    assert depth == 1, "the fused final norm assumes a single trunk layer"
    l = 0
    t = bsz * s
    d_a = sgu_ln_g.shape[-1]
    d_b = out_norm_b_g.shape[-1]
    n_heads, chunk = w_spatial.shape[1], w_spatial.shape[2]
    n_experts = w_router.shape[-1]
    assert d_a == n_heads * A_HEAD_DIM and chunk == LANES and 2 * A_HEAD_DIM == LANES
    assert t % FRONT_ROWS == 0 and FRONT_ROWS % chunk == 0 and s % FRONT_ROWS == 0
    assert t % DISPATCH_TOKENS == 0 and t % COMBINE_TOKENS == 0
    row = lambda v: v.reshape(1, -1).astype(F32)

    x2 = x.reshape(t, d)

    win = w_in[l].astype(BF16)
    ws = w_spatial[l]
    ws_pair = jnp.concatenate([ws[0::2], ws[1::2]], axis=2).astype(BF16)
    bs_full = jnp.repeat(b_spatial[l].T, A_HEAD_DIM, axis=1).astype(F32)
    woa = w_out[l][:d_a].astype(BF16)
    wob = w_out[l][d_a:].astype(BF16)
    wrt = w_router[l].T.astype(F32)
    rb = jnp.broadcast_to(router_bias[l].astype(F32)[:, None], (n_experts, 1))
    wsgu = jnp.concatenate([ws_gate[l], ws_up[l]], axis=1).astype(BF16)
    wsd = ws_down[l].astype(BF16)
    wgu = jnp.concatenate([w_gate[l], w_up[l]], axis=2).astype(BF16)
    wd = w_down[l].astype(BF16)

    an, zb = _front(x2, row(norm1_g[l]), win, row(sgu_ln_g[l]), row(sgu_ln_b[l]), ws_pair, bs_full,
                    row(out_norm_a_g[l]), d_a=d_a, d_b=d_b, chunk=chunk)
    bn = _fnet(zb, row(out_norm_b_g[l]), bsz).reshape(t, d_b)

    hs3, hn3, eidx, gate, rank, cnt = _post(x2, an, bn, woa, wob, row(norm2_g[l]), wrt, rb, wsgu, wsd)

    n_groups = t // GROUP_TOKENS
    n_assign = GROUP_TOKENS * TOP_K
    counts = cnt[:, :, 0]
    n_pad = (-counts) % LANES
    padded = counts + n_pad
    offs = jnp.cumsum(padded, axis=1) - padded
    tok_bits = n_assign.bit_length()
    local = jnp.arange(n_assign, dtype=jnp.int32).reshape(1, GROUP_TOKENS, TOP_K)
    eidx_g = eidx.reshape(TOP_K, n_groups, GROUP_TOKENS).transpose(1, 2, 0)
    gate_g = gate.reshape(TOP_K, n_groups, GROUP_TOKENS).transpose(1, 2, 0)
    real_keys = ((eidx_g << tok_bits) | local).reshape(n_groups, -1)
    pad_j = jnp.arange(LANES - 1, dtype=jnp.int32)[None, None, :]
    e_col = jnp.arange(n_experts, dtype=jnp.int32)[None, :, None]
    pad_keys = jnp.where(pad_j < n_pad[:, :, None], (e_col << tok_bits) | (n_assign + pad_j),
                         jnp.iinfo(jnp.int32).max).reshape(n_groups, -1)
    keys = jnp.concatenate([real_keys, pad_keys], axis=1)
    gates_in = jnp.concatenate([gate_g.reshape(n_groups, -1), jnp.zeros(pad_keys.shape, F32)], axis=1)
    keys_sorted, gates_sorted = lax.sort((keys, gates_in), dimension=1, num_keys=1)
    local_sorted = keys_sorted & ((1 << tok_bits) - 1)
    rows_sorted = jnp.where(local_sorted < n_assign, local_sorted // TOP_K * (d // LANES), 0)
    list_len = -(-(keys.shape[1] + EXPERT_ROWS - LANES) // (8 * LANES)) * (8 * LANES)
    fill = ((0, 0), (0, list_len - keys.shape[1]))
    rows_flat = jnp.pad(rows_sorted, fill).reshape(-1)
    gates2d = jnp.pad(gates_sorted, fill).reshape(n_groups * (list_len // LANES), LANES)
    e_row = jnp.arange(n_experts, dtype=jnp.int32)
    later = (e_row[None, None, :] > e_row[None, :, None]) & (counts[:, None, :] > 0)
    first_later = jnp.min(jnp.where(later, e_row[None, None, :], n_experts), axis=2)
    next_off = jnp.sum(jnp.where(first_later[:, :, None] == e_row[None, None, :], offs[:, None, :], 0), axis=2)
    flat = lambda a: a.reshape(-1).astype(jnp.int32)

    out = _moe(flat(counts), flat(offs), flat(next_off), rows_flat, gates2d,
               hn3, hs3, wgu, wd, row(final_norm_g))
    return out.reshape(bsz, s, d)
```

```python
import functools
import math

import numpy as np
import jax
import jax.numpy as jnp
from jax import lax
from jax.experimental import pallas as pl
from jax.experimental.pallas import tpu as pltpu

A_HEAD_DIM = 64
B_GROUP_DIM = 64
TOP_K = 8
N_EXPERT_GROUPS = 8
TOPK_GROUPS = 4
ROUTED_SCALE = 2.5
EPS = 1e-6

LANES = 128
FRONT_ROWS = 512
EXPERT_ROWS = 128
DISPATCH_TOKENS = 512
COMBINE_TOKENS = 128
GROUP_TOKENS = 2048
FFT_RADIX = 64
FFT_PAD = 8
VMEM_LIMIT = 56 * 1024 * 1024

F32 = jnp.float32
BF16 = jnp.bfloat16


def _rms(x, g):
    return x * lax.rsqrt(jnp.mean(x * x, axis=-1, keepdims=True) + EPS) * g


def _gelu(x):
    return 0.5 * x * (1.0 + lax.erf(x * (1.0 / math.sqrt(2.0))))


def _silu(x):
    return x / (1.0 + jnp.exp(-x))


def _front_kernel(x_ref, g1_ref, win_ref, lng_ref, lnb_ref, ws_ref, bs_ref, ga_ref,
                  an_ref, zb_ref, ya_scr, *, d_a, chunk):
    x = x_ref[...]
    xn = _rms(x, g1_ref[...]).astype(BF16)
    z = jnp.dot(xn, win_ref[...], preferred_element_type=F32)
    for q in range(zb_ref.shape[0]):
        zb_ref[q] = z[:, 2 * d_a + q * LANES:2 * d_a + (q + 1) * LANES].astype(BF16)
    u = _gelu(z[:, :d_a])
    v = _gelu(z[:, d_a:2 * d_a])
    mu = jnp.mean(v, axis=-1, keepdims=True)
    vc = v - mu
    v = vc * lax.rsqrt(jnp.mean(vc * vc, axis=-1, keepdims=True) + EPS) * lng_ref[...] + lnb_ref[...]
    rows = x.shape[0]
    lane = lax.broadcasted_iota(jnp.int32, (chunk, LANES), 1)
    low = lane < A_HEAD_DIM
    for c in range(rows // chunk):
        r0 = c * chunk
        for hp in range(d_a // LANES):
            c0 = hp * LANES
            vp = v[r0:r0 + chunk, c0:c0 + LANES]
            rhs = jnp.concatenate([jnp.where(low, vp, 0.0), jnp.where(low, 0.0, vp)], axis=0).astype(BF16)
            mixed = jnp.dot(ws_ref[hp], rhs, preferred_element_type=F32) + bs_ref[:, c0:c0 + LANES]
            ya_scr[r0:r0 + chunk, c0:c0 + LANES] = u[r0:r0 + chunk, c0:c0 + LANES] * mixed
    an_ref[...] = _rms(ya_scr[...], ga_ref[...]).astype(BF16)


def _front(x2, g1, win, lng, lnb, ws_pair, bs_full, ga, *, d_a, d_b, chunk):
    t, d = x2.shape
    d_in = win.shape[1]
    rows = FRONT_ROWS
    const = lambda shape: pl.BlockSpec(shape, lambda i: (0,) * len(shape))
    return pl.pallas_call(
        functools.partial(_front_kernel, d_a=d_a, chunk=chunk),
        grid=(t // rows,),
        in_specs=[
            pl.BlockSpec((rows, d), lambda i: (i, 0)),
            const((1, d)), const((d, d_in)), const((1, d_a)), const((1, d_a)),
            const(ws_pair.shape), const(bs_full.shape), const((1, d_a)),
        ],
        out_specs=[pl.BlockSpec((rows, d_a), lambda i: (i, 0)),
                   pl.BlockSpec((d_b // LANES, rows, LANES), lambda i: (0, i, 0))],
        out_shape=[jax.ShapeDtypeStruct((t, d_a), BF16), jax.ShapeDtypeStruct((d_b // LANES, t, LANES), BF16)],
        scratch_shapes=[pltpu.VMEM((rows, d_a), F32)],
        compiler_params=pltpu.CompilerParams(dimension_semantics=("parallel",),
                                             vmem_limit_bytes=VMEM_LIMIT),
        name="front",
    )(x2, g1, win, lng, lnb, ws_pair, bs_full, ga)


def _fnet_tables(seq, half_w):
    r = FFT_RADIX
    assert seq == r * r
    j = np.arange(B_GROUP_DIM)
    ang = 2.0 * np.pi * np.outer(j, j) / B_GROUP_DIM
    ng = half_w // B_GROUP_DIM
    eye = np.eye(ng)
    m_ch = np.concatenate([np.kron(eye, np.cos(ang)), np.kron(eye, np.sin(ang))], axis=1)
    k1 = np.arange(r)[:, None]
    n1 = np.arange(r)[None, :]
    d_tw = np.zeros((r, 2 * r, 2 * r))
    for n2 in range(r):
        th = 2.0 * np.pi * (k1 * n1 / r + k1 * n2 / (r * r))
        c, s = np.cos(th), np.sin(th)
        d_tw[n2] = np.block([[c, -s], [s, c]])
    ph = 2.0 * np.pi * np.outer(np.arange(r), np.arange(r)) / r
    scale = 1.0 / math.sqrt(seq * B_GROUP_DIM)
    d2 = np.concatenate([np.cos(ph), -np.sin(ph)], axis=1) * scale
    return (jnp.asarray(m_ch, F32), jnp.asarray(d_tw, F32), jnp.asarray(d2, F32))


def _fnet_kernel(z_ref, mch32_ref, dtw_ref, d232_ref, gb_ref, out_ref, w_scr, a2_scr, y_scr, mch_ref, d2_ref,
                 *, n_col, unroll):
    r = FFT_RADIX
    rp = r + FFT_PAD
    ap = 2 * r + FFT_PAD
    hf = pl.program_id(1)
    half_w = n_col * LANES
    mch_ref[...] = mch32_ref[...].astype(BF16)
    d2_ref[...] = d232_ref[...].astype(BF16)

    chunk = 8 * r
    for c in range(r * r // chunk):
        zc = jnp.concatenate([z_ref[q, c * chunk:(c + 1) * chunk, :] for q in range(n_col)], axis=1)
        w = jnp.dot(zc, mch_ref[...], preferred_element_type=F32)
        for j in range(chunk // r):
            n1 = c * (chunk // r) + j
            for p in range(2 * n_col):
                w_scr[p, n1 * rp:n1 * rp + r, :] = w[j * r:(j + 1) * r, p * LANES:(p + 1) * LANES]

    def stage1(it, carry):
        for u in range(unroll):
            n2 = it * unroll + u
            rhs = jnp.concatenate(
                [jnp.concatenate([w_scr[ri * n_col + q, pl.ds(n2, r, stride=rp), :] for q in range(n_col)], axis=1)
                 for ri in range(2)], axis=0).astype(BF16)
            a = jnp.dot(dtw_ref[n2].astype(BF16), rhs, preferred_element_type=F32)
            for q in range(n_col):
                a2_scr[q, pl.ds(n2, r, stride=ap), :] = a[:r, q * LANES:(q + 1) * LANES]
                a2_scr[q, pl.ds(r + n2, r, stride=ap), :] = a[r:, q * LANES:(q + 1) * LANES]
        return carry

    lax.fori_loop(0, r // unroll, stage1, 0)

    def stage2(it, carry):
        for u in range(unroll):
            k1 = it * unroll + u
            row0 = pl.multiple_of(k1 * ap, 8)
            blk = jnp.concatenate([a2_scr[q, pl.ds(row0, 2 * r), :] for q in range(n_col)],
                                  axis=1).astype(BF16)
            y = jnp.dot(d2_ref[...], blk, preferred_element_type=F32)
            for q in range(n_col):
                y_scr[hf * n_col + q, pl.ds(k1, r, stride=rp), :] = y[:, q * LANES:(q + 1) * LANES]
        return carry

    lax.fori_loop(0, r // unroll, stage2, 0)

    @pl.when(hf == pl.num_programs(1) - 1)
    def _():
        n_all = y_scr.shape[0]
        for k2 in range(r):
            ys = [y_scr[q, k2 * rp:k2 * rp + r, :] for q in range(n_all)]
            ss = sum(jnp.sum(y * y, axis=-1, keepdims=True) for y in ys)
            inv = lax.rsqrt(ss * (1.0 / (n_all * LANES)) + EPS)
            for q in range(n_all):
                out_ref[0, k2 * r:(k2 + 1) * r, q * LANES:(q + 1) * LANES] = (
                    ys[q] * inv * gb_ref[:, q * LANES:(q + 1) * LANES]).astype(BF16)


def _fnet(zb_cols, gb, bsz):
    n_all, t, _ = zb_cols.shape
    s = t // bsz
    d_b = n_all * LANES
    n_col = 2
    r = FFT_RADIX
    rp, ap = r + FFT_PAD, 2 * r + FFT_PAD
    m_ch, d_tw, d2 = _fnet_tables(s, n_col * LANES)
    return pl.pallas_call(
        functools.partial(_fnet_kernel, n_col=n_col, unroll=8),
        grid=(bsz, n_all // n_col),
        in_specs=[
            pl.BlockSpec((n_col, s, LANES), lambda i, h: (h, i, 0)),
            pl.BlockSpec(m_ch.shape, lambda i, h: (0, 0)),
            pl.BlockSpec(d_tw.shape, lambda i, h: (0, 0, 0)),
            pl.BlockSpec(d2.shape, lambda i, h: (0, 0)),
            pl.BlockSpec((1, d_b), lambda i, h: (0, 0)),
        ],
        out_specs=pl.BlockSpec((1, s, d_b), lambda i, h: (i, 0, 0)),
        out_shape=jax.ShapeDtypeStruct((bsz, s, d_b), BF16),
        scratch_shapes=[pltpu.VMEM((2 * n_col, r * rp, LANES), F32), pltpu.VMEM((n_col, r * ap, LANES), F32),
                        pltpu.VMEM((n_all, r * rp, LANES), F32),
                        pltpu.VMEM(m_ch.shape, BF16), pltpu.VMEM(d2.shape, BF16)],
        compiler_params=pltpu.CompilerParams(dimension_semantics=("parallel", "arbitrary"),
                                             vmem_limit_bytes=VMEM_LIMIT),
        name="fnet",
    )(zb_cols, m_ch, d_tw, d2, gb)


def _post_kernel(x_ref, an_ref, bn_ref, woa_ref, wob_ref, g2_ref, wrt_ref, rb_ref, wsgu_ref, wsd_ref,
                 hs_ref, hn_ref, eidx_ref, gate_ref, cnt_ref, carry_scr,
                 *, n_experts, d_shared, steps_per_group):
    i = pl.program_id(0)

    @pl.when(i % steps_per_group == 0)
    def _():
        carry_scr[...] = jnp.zeros_like(carry_scr)

    h = (x_ref[...]
         + jnp.dot(an_ref[...], woa_ref[...], preferred_element_type=F32)
         + jnp.dot(bn_ref[...], wob_ref[...], preferred_element_type=F32))
    hn = _rms(h, g2_ref[...])
    hnb = hn.astype(BF16)
    gu = jnp.dot(hnb, wsgu_ref[...], preferred_element_type=F32)
    act = (_silu(gu[:, :d_shared]) * gu[:, d_shared:]).astype(BF16)
    hs = h + jnp.dot(act, wsd_ref[...], preferred_element_type=F32)
    n_slab = hn.shape[1] // LANES
    for s in range(n_slab):
        hn_ref[pl.ds(s, hn.shape[0], stride=n_slab), :] = hn[:, s * LANES:(s + 1) * LANES]
        hs_ref[pl.ds(s, hn.shape[0], stride=n_slab), :] = hs[:, s * LANES:(s + 1) * LANES]

    rows = hn.shape[0]
    eg = n_experts // N_EXPERT_GROUPS
    logits_t = lax.dot_general(wrt_ref[...], hn, (((1,), (1,)), ((), ())),
                               precision=lax.Precision.HIGHEST, preferred_element_type=F32)
    scores_t = 1.0 / (1.0 + jnp.exp(-logits_t))
    biased_t = scores_t + rb_ref[...]
    sub = lax.broadcasted_iota(jnp.int32, (eg, rows), 0)
    neg = -jnp.inf
    grp = [biased_t[g * eg:(g + 1) * eg, :] for g in range(N_EXPERT_GROUPS)]
    sco = [scores_t[g * eg:(g + 1) * eg, :] for g in range(N_EXPERT_GROUPS)]
    gid = [sub + g * eg for g in range(N_EXPERT_GROUPS)]

    gscore = []
    for g in range(N_EXPERT_GROUPS):
        m1 = jnp.max(grp[g], axis=0, keepdims=True)
        first = jnp.min(jnp.where(grp[g] == m1, sub, eg), axis=0, keepdims=True)
        m2 = jnp.max(jnp.where(sub == first, neg, grp[g]), axis=0, keepdims=True)
        gscore.append(m1 + m2)
    masked = []
    for g in range(N_EXPERT_GROUPS):
        beaten = jnp.zeros((1, rows), jnp.int32)
        for j in range(N_EXPERT_GROUPS):
            if j == g:
                continue
            wins = (gscore[j] > gscore[g]) if j > g else (gscore[j] >= gscore[g])
            beaten = beaten + wins.astype(jnp.int32)
        masked.append(jnp.where(beaten < TOPK_GROUPS, grp[g], neg))

    sel = [jnp.zeros((eg, rows), F32) for _ in range(N_EXPERT_GROUPS)]
    e_rows, w_rows = [], []
    for _ in range(TOP_K):
        m = masked[0]
        for g in range(1, N_EXPERT_GROUPS):
            m = jnp.maximum(m, masked[g])
        mk = jnp.max(m, axis=0, keepdims=True)
        cand = jnp.where(masked[0] == mk, gid[0], n_experts)
        for g in range(1, N_EXPERT_GROUPS):
            cand = jnp.minimum(cand, jnp.where(masked[g] == mk, gid[g], n_experts))
        ik = jnp.min(cand, axis=0, keepdims=True)
        wk = jnp.zeros((eg, rows), F32)
        for g in range(N_EXPERT_GROUPS):
            hit = gid[g] == ik
            masked[g] = jnp.where(hit, neg, masked[g])
            sel[g] = jnp.where(hit, 1.0, sel[g])
            wk = wk + jnp.where(hit, sco[g], 0.0)
        e_rows.append(ik)
        w_rows.append(jnp.sum(wk, axis=0, keepdims=True))
    wsum = w_rows[0]
    for k in range(1, TOP_K):
        wsum = wsum + w_rows[k]
    gates = [w_rows[k] / wsum * ROUTED_SCALE for k in range(TOP_K)]

    sel_t = jnp.concatenate(sel, axis=0)
    new_carry = carry_scr[...] + jnp.sum(sel_t, axis=1, keepdims=True)
    carry_scr[...] = new_carry
    cnt_ref[0] = new_carry.astype(jnp.int32)
    eidx_ref[...] = jnp.concatenate(e_rows, axis=0)
    gate_ref[...] = jnp.concatenate(gates, axis=0)


def _post(x2, an, bn, woa, wob, g2, wrt, rb, wsgu, wsd):
    t, d = x2.shape
    rows = FRONT_ROWS
    n_experts = wrt.shape[0]
    d_shared = wsd.shape[0]
    n_slab = d // LANES
    steps_per_group = GROUP_TOKENS // rows
    n_groups = t // GROUP_TOKENS
    const = lambda shape: pl.BlockSpec(shape, lambda i: (0,) * len(shape))
    row_blk = lambda w: pl.BlockSpec((rows, w), lambda i: (i, 0))
    slab_blk = pl.BlockSpec((rows * n_slab, LANES), lambda i: (i, 0))
    tok_blk = pl.BlockSpec((TOP_K, rows), lambda i: (0, i))
    return pl.pallas_call(
        functools.partial(_post_kernel, n_experts=n_experts, d_shared=d_shared,
                          steps_per_group=steps_per_group),
        grid=(t // rows,),
        in_specs=[row_blk(d), row_blk(an.shape[1]), row_blk(bn.shape[1]),
                  const(woa.shape), const(wob.shape), const((1, d)), const(wrt.shape), const(rb.shape),
                  const(wsgu.shape), const(wsd.shape)],
        out_specs=[slab_blk, slab_blk, tok_blk, tok_blk,
                   pl.BlockSpec((1, n_experts, LANES), lambda i: (i // steps_per_group, 0, 0))],
        out_shape=[jax.ShapeDtypeStruct((t * n_slab, LANES), F32), jax.ShapeDtypeStruct((t * n_slab, LANES), F32),
                   jax.ShapeDtypeStruct((TOP_K, t), jnp.int32), jax.ShapeDtypeStruct((TOP_K, t), F32),
                   jax.ShapeDtypeStruct((n_groups, n_experts, LANES), jnp.int32)],
        scratch_shapes=[pltpu.VMEM((n_experts, LANES), F32)],
        compiler_params=pltpu.CompilerParams(dimension_semantics=("arbitrary",),
                                             vmem_limit_bytes=VMEM_LIMIT),
        name="post",
    )(x2, an, bn, woa, wob, g2, wrt, rb, wsgu, wsd)


def _dispatch_kernel(pend_ref, dest_ref, hn_ref, xs_hbm, zero_scr, zsem, sem, *, n_experts):
    i = pl.program_id(0)
    blk = EXPERT_ROWS
    tokens = DISPATCH_TOKENS

    def zero_copy(e):
        start = pl.multiple_of(jnp.maximum(pend_ref[e] - blk, 0), blk)
        return pltpu.make_async_copy(zero_scr, xs_hbm.at[pl.ds(start, blk)], zsem)

    @pl.when(i == 0)
    def _():
        zero_scr[...] = jnp.zeros_like(zero_scr)

        def start_body(e, c):
            zero_copy(e).start()
            return c

        def wait_body(e, c):
            zero_copy(e).wait()
            return c

        lax.fori_loop(0, n_experts, start_body, 0)
        lax.fori_loop(0, n_experts, wait_body, 0)

        def tail_copy(b):
            return pltpu.make_async_copy(zero_scr, xs_hbm.at[pl.ds(pl.multiple_of(b * blk, blk), blk)], zsem)

        def tail_start(b, c):
            tail_copy(b).start()
            return c

        def tail_wait(b, c):
            tail_copy(b).wait()
            return c

        first_free = pend_ref[n_experts - 1] // blk
        lax.fori_loop(first_free, xs_hbm.shape[0] // blk, tail_start, 0)
        lax.fori_loop(first_free, xs_hbm.shape[0] // blk, tail_wait, 0)

    def issue(j, c):
        for k in range(TOP_K):
            pltpu.make_async_copy(hn_ref.at[pl.ds(j, 1)], xs_hbm.at[pl.ds(dest_ref[j * TOP_K + k], 1)],
                                  sem).start(priority=k % 2)
        return c

    lax.fori_loop(0, tokens, issue, 0)
    for k in range(TOP_K):
        pltpu.make_async_copy(hn_ref, xs_hbm.at[pl.ds(0, tokens)], sem).wait()


def _dispatch(pad_end, dest_flat, hn, n_rows):
    t, d = hn.shape
    n_experts = pad_end.shape[0]
    tokens = DISPATCH_TOKENS
    grid_spec = pltpu.PrefetchScalarGridSpec(
        num_scalar_prefetch=1,
        grid=(t // tokens,),
        in_specs=[pl.BlockSpec((tokens * TOP_K,), lambda i, pe: (i,), memory_space=pltpu.SMEM),
                  pl.BlockSpec((tokens, d), lambda i, pe: (i, 0))],
        out_specs=pl.BlockSpec(memory_space=pl.ANY),
        scratch_shapes=[pltpu.VMEM((EXPERT_ROWS, d), F32), pltpu.SemaphoreType.DMA, pltpu.SemaphoreType.DMA],
    )
    return pl.pallas_call(
        functools.partial(_dispatch_kernel, n_experts=n_experts),
        grid_spec=grid_spec,
        out_shape=jax.ShapeDtypeStruct((n_rows, d), F32),
        compiler_params=pltpu.CompilerParams(dimension_semantics=("arbitrary",), has_side_effects=True,
                                             disable_bounds_checks=True),
        name="dispatch",
    )(pad_end, dest_flat, hn)


def _expert_kernel(be_ref, nu_ref, xs_ref, wgu_ref, wd_ref, ys_ref, *, d_expert):
    i = pl.program_id(0)

    @pl.when(i < nu_ref[0])
    def _():
        xb = xs_ref[...].astype(BF16)
        gu = jnp.dot(xb, wgu_ref[0], preferred_element_type=F32)
        act = (_silu(gu[:, :d_expert]) * gu[:, d_expert:]).astype(BF16)
        ys_ref[...] = jnp.dot(act, wd_ref[0], preferred_element_type=F32)

    @pl.when(i >= nu_ref[0])
    def _():
        ys_ref[...] = jnp.zeros_like(ys_ref)


def _experts(block_e, n_used, xs, wgu, wd):
    n_rows, d = xs.shape
    blk = EXPERT_ROWS
    d_expert = wd.shape[1]
    grid_spec = pltpu.PrefetchScalarGridSpec(
        num_scalar_prefetch=2,
        grid=(n_rows // blk,),
        in_specs=[pl.BlockSpec((blk, d), lambda i, be, nu: (jnp.minimum(i, nu[0] - 1), 0)),
                  pl.BlockSpec((1, d, 2 * d_expert), lambda i, be, nu: (be[i], 0, 0)),
                  pl.BlockSpec((1, d_expert, d), lambda i, be, nu: (be[i], 0, 0))],
        out_specs=pl.BlockSpec((blk, d), lambda i, be, nu: (i, 0)),
    )
    return pl.pallas_call(
        functools.partial(_expert_kernel, d_expert=d_expert),
        grid_spec=grid_spec,
        out_shape=jax.ShapeDtypeStruct((n_rows, d), F32),
        compiler_params=pltpu.CompilerParams(dimension_semantics=("arbitrary",),
                                             vmem_limit_bytes=VMEM_LIMIT),
        name="experts",
    )(block_e, n_used, xs, wgu, wd)


def _combine_kernel(dcur_ref, dnext_ref, gate_ref, hs_ref, gf_ref, ys_hbm, out_ref, buf, sem):
    i = pl.program_id(0)
    n = pl.num_programs(0)
    tokens = COMBINE_TOKENS
    slot = i % 2

    def row_copy(dref, s, j, k):
        return pltpu.make_async_copy(ys_hbm.at[pl.ds(dref[j * TOP_K + k], 1)],
                                     buf.at[s, k, pl.ds(j, 1)], sem.at[s])

    def issue(dref, s):
        def body(j, c):
            for k in range(TOP_K):
                row_copy(dref, s, j, k).start(priority=k % 2)
            return c
        lax.fori_loop(0, tokens, body, 0)

    @pl.when(i == 0)
    def _():
        issue(dcur_ref, 0)

    @pl.when(i + 1 < n)
    def _():
        issue(dnext_ref, 1 - slot)

    for k in range(TOP_K):
        pltpu.make_async_copy(ys_hbm.at[pl.ds(0, tokens)], buf.at[slot, k], sem.at[slot]).wait()

    acc = hs_ref[...]
    g = gate_ref[...]
    for k in range(TOP_K):
        acc = acc + g[:, k:k + 1] * buf[slot, k]
    out_ref[...] = _rms(acc, gf_ref[...])


def _combine(dest_flat, gate_tk, hs, gf, ys):
    t, d = hs.shape
    tokens = COMBINE_TOKENS
    n = t // tokens
    dspec = lambda f: pl.BlockSpec((tokens * TOP_K,), f, memory_space=pltpu.SMEM)
    return pl.pallas_call(
        _combine_kernel,
        grid=(n,),
        in_specs=[dspec(lambda i: (i,)), dspec(lambda i: (jnp.minimum(i + 1, n - 1),)),
                  pl.BlockSpec((tokens, TOP_K), lambda i: (i, 0)),
                  pl.BlockSpec((tokens, d), lambda i: (i, 0)),
                  pl.BlockSpec((1, d), lambda i: (0, 0)),
                  pl.BlockSpec(memory_space=pl.ANY)],
        out_specs=pl.BlockSpec((tokens, d), lambda i: (i, 0)),
        out_shape=jax.ShapeDtypeStruct((t, d), F32),
        scratch_shapes=[pltpu.VMEM((2, TOP_K, tokens, d), F32), pltpu.SemaphoreType.DMA((2,))],
        compiler_params=pltpu.CompilerParams(dimension_semantics=("arbitrary",),
                                             vmem_limit_bytes=VMEM_LIMIT, disable_bounds_checks=True),
        name="combine",
    )(dest_flat, dest_flat, gate_tk, hs, gf, ys)


def _moe_kernel(cnt_ref, off_ref, next_ref, rows_ref, gates_ref, hn_ref, hs_hbm, wg_ref, wu_ref, wd_ref, gf_ref,
                out_ref, acc_scr, x_scr, y_scr, state_scr, sem, *, d_expert, n_slab):
    g = pl.program_id(0)
    e = pl.program_id(1)
    n_e = pl.num_programs(1)
    tokens = GROUP_TOKENS
    blk = EXPERT_ROWS
    batch = 16

    def slab(ref, row0):
        return ref.at[pl.ds(pl.multiple_of(row0, n_slab), n_slab), :]

    def acc_init():
        return pltpu.make_async_copy(hs_hbm.at[pl.ds(pl.multiple_of(g * tokens * n_slab, n_slab), tokens * n_slab), :],
                                     acc_scr.at[pl.ds(0, tokens * n_slab), :], sem)

    def gather(base):
        for r in range(blk):
            slab(x_scr, r * n_slab)[...] = slab(hn_ref, rows_ref[base + r])[...]

    def scatter(base):
        for j in range(blk // batch):
            rows, vals = [], []
            for u in range(batch):
                r = j * batch + u
                row0 = rows_ref[base + r]
                rows.append(row0)
                vals.append(slab(acc_scr, row0)[...] + slab(y_scr, r * n_slab)[...])
            for u in reversed(range(batch)):
                slab(acc_scr, rows[u])[...] = vals[u]

    n = cnt_ref[g * n_e + e]
    off = off_ref[g * n_e + e]

    @pl.when(e == 0)
    def _():
        acc_init().start()
        y_scr[...] = jnp.zeros_like(y_scr)
        state_scr[0] = 0
        gather(jnp.where(n > 0, off, next_ref[g * n_e + e]))
        acc_init().wait()

    n_blocks = (n + blk - 1) // blk
    nxt = next_ref[g * n_e + e]
    on_diag = (lax.broadcasted_iota(jnp.int32, (LANES, LANES), 0)
               == lax.broadcasted_iota(jnp.int32, (LANES, LANES), 1))
    row_id = lax.broadcasted_iota(jnp.int32, (blk, 1), 0)

    def block(i, carry):
        base = off + i * blk
        xb = jnp.concatenate([x_scr[pl.ds(s, blk, stride=n_slab), :] for s in range(n_slab)], axis=1).astype(BF16)
        scatter(state_scr[0])
        gather(jnp.where(i + 1 < n_blocks, base + blk, nxt))
        act = (_silu(jnp.dot(xb, wg_ref[0], preferred_element_type=F32))
               * jnp.dot(xb, wu_ref[0], preferred_element_type=F32)).astype(BF16)
        y = jnp.dot(act, wd_ref[0], preferred_element_type=F32)
        first_row = lax.shift_right_logical(base, LANES.bit_length() - 1)
        cols = []
        for j in range(blk // LANES):
            gate_row = gates_ref[pl.ds(first_row + j, 1), :]
            cols.append(jnp.sum(jnp.where(on_diag, gate_row, 0.0), axis=1, keepdims=True))
        gate_col = cols[0] if len(cols) == 1 else jnp.concatenate(cols, axis=0)
        y = jnp.where(row_id < n - i * blk, y * gate_col, 0.0)
        for s in range(n_slab):
            y_scr[pl.ds(s, blk, stride=n_slab), :] = y[:, s * LANES:(s + 1) * LANES]
        state_scr[0] = base
        return carry

    lax.fori_loop(0, n_blocks, block, 0)

    @pl.when(e == n_e - 1)
    def _():
        scatter(state_scr[0])
        rows = 256
        for c in range(tokens // rows):
            acc = jnp.concatenate([acc_scr[pl.ds(c * rows * n_slab + s, rows, stride=n_slab), :]
                                   for s in range(n_slab)], axis=1)
            out_ref[c * rows:(c + 1) * rows, :] = _rms(acc, gf_ref[...])


def _moe(cnt_flat, off_flat, next_flat, rows_flat, gates2d, hn3, hs3, wg, wu, wd, gf):
    n_slab = gf.shape[1] // LANES
    t = hn3.shape[0] // n_slab
    d = gf.shape[1]
    n_experts, d_expert = wd.shape[0], wd.shape[1]
    tokens = GROUP_TOKENS
    n_groups = t // tokens
    list_len = rows_flat.shape[0] // n_groups
    assert EXPERT_ROWS % LANES == 0 and list_len % (8 * LANES) == 0
    grid_spec = pltpu.PrefetchScalarGridSpec(
        num_scalar_prefetch=3,
        grid=(n_groups, n_experts),
        in_specs=[pl.BlockSpec((list_len,), lambda g, e, *_: (g,), memory_space=pltpu.SMEM),
                  pl.BlockSpec((list_len // LANES, LANES), lambda g, e, *_: (g, 0)),
                  pl.BlockSpec((tokens * n_slab, LANES), lambda g, e, *_: (g, 0)),
                  pl.BlockSpec(memory_space=pl.ANY),
                  pl.BlockSpec((1, d, d_expert), lambda g, e, *_: (e, 0, 0)),
                  pl.BlockSpec((1, d, d_expert), lambda g, e, *_: (e, 0, 0)),
                  pl.BlockSpec((1, d_expert, d), lambda g, e, *_: (e, 0, 0)),
                  pl.BlockSpec((1, d), lambda g, e, *_: (0, 0))],
        out_specs=pl.BlockSpec((tokens, d), lambda g, e, *_: (g, 0)),
        scratch_shapes=[pltpu.VMEM((tokens * n_slab, LANES), F32),
                        pltpu.VMEM((EXPERT_ROWS * n_slab, LANES), F32),
                        pltpu.VMEM((EXPERT_ROWS * n_slab, LANES), F32),
                        pltpu.SMEM((1,), jnp.int32),
                        pltpu.SemaphoreType.DMA],
    )
    return pl.pallas_call(
        functools.partial(_moe_kernel, d_expert=d_expert, n_slab=n_slab),
        grid_spec=grid_spec,
        out_shape=jax.ShapeDtypeStruct((t, d), F32),
        compiler_params=pltpu.CompilerParams(dimension_semantics=("arbitrary", "arbitrary"),
                                             vmem_limit_bytes=VMEM_LIMIT),
        name="moe",
    )(cnt_flat, off_flat, next_flat, rows_flat, gates2d, hn3, hs3, wg, wu, wd, gf)


def kernel(x, norm1_g, w_in, sgu_ln_g, sgu_ln_b, w_spatial, b_spatial, out_norm_a_g, out_norm_b_g, w_out,
           norm2_g, w_router, router_bias, w_gate, w_up, w_down, ws_gate, ws_up, ws_down, final_norm_g):
    bsz, s, d = x.shape
    depth = w_in.shape[0]
    assert depth == 1, "the fused final norm assumes a single trunk layer"
    l = 0
    t = bsz * s
    d_a = sgu_ln_g.shape[-1]
    d_b = out_norm_b_g.shape[-1]
    n_heads, chunk = w_spatial.shape[1], w_spatial.shape[2]
    n_experts = w_router.shape[-1]
    assert d_a == n_heads * A_HEAD_DIM and chunk == LANES and 2 * A_HEAD_DIM == LANES
    assert t % FRONT_ROWS == 0 and FRONT_ROWS % chunk == 0 and s % FRONT_ROWS == 0
    assert t % DISPATCH_TOKENS == 0 and t % COMBINE_TOKENS == 0
    row = lambda v: v.reshape(1, -1).astype(F32)

    x2 = x.reshape(t, d)

    win = w_in[l].astype(BF16)
    ws = w_spatial[l]
    ws_pair = jnp.concatenate([ws[0::2], ws[1::2]], axis=2).astype(BF16)
    bs_full = jnp.repeat(b_spatial[l].T, A_HEAD_DIM, axis=1).astype(F32)
    woa = w_out[l][:d_a].astype(BF16)
    wob = w_out[l][d_a:].astype(BF16)
    wrt = w_router[l].T.astype(F32)
    rb = jnp.broadcast_to(router_bias[l].astype(F32)[:, None], (n_experts, 1))
    wsgu = jnp.concatenate([ws_gate[l], ws_up[l]], axis=1).astype(BF16)
    wsd = ws_down[l].astype(BF16)
    wg = w_gate[l].astype(BF16)
    wu = w_up[l].astype(BF16)
    wd = w_down[l].astype(BF16)

    an, zb = _front(x2, row(norm1_g[l]), win, row(sgu_ln_g[l]), row(sgu_ln_b[l]), ws_pair, bs_full,
                    row(out_norm_a_g[l]), d_a=d_a, d_b=d_b, chunk=chunk)
    bn = _fnet(zb, row(out_norm_b_g[l]), bsz).reshape(t, d_b)

    hs3, hn3, eidx, gate, cnt = _post(x2, an, bn, woa, wob, row(norm2_g[l]), wrt, rb, wsgu, wsd)

    n_groups = t // GROUP_TOKENS
    n_assign = GROUP_TOKENS * TOP_K
    counts = cnt[:, :, 0]
    n_pad = (-counts) % LANES
    padded = counts + n_pad
    offs = jnp.cumsum(padded, axis=1) - padded
    tok_bits = n_assign.bit_length()
    local = jnp.arange(n_assign, dtype=jnp.int32).reshape(1, GROUP_TOKENS, TOP_K)
    eidx_g = eidx.reshape(TOP_K, n_groups, GROUP_TOKENS).transpose(1, 2, 0)
    gate_g = gate.reshape(TOP_K, n_groups, GROUP_TOKENS).transpose(1, 2, 0)
    real_keys = ((eidx_g << tok_bits) | local).reshape(n_groups, -1)
    pad_j = jnp.arange(LANES - 1, dtype=jnp.int32)[None, None, :]
    e_col = jnp.arange(n_experts, dtype=jnp.int32)[None, :, None]
    pad_keys = jnp.where(pad_j < n_pad[:, :, None], (e_col << tok_bits) | (n_assign + pad_j),
                         jnp.iinfo(jnp.int32).max).reshape(n_groups, -1)
    keys = jnp.concatenate([real_keys, pad_keys], axis=1)
    gates_in = jnp.concatenate([gate_g.reshape(n_groups, -1), jnp.zeros(pad_keys.shape, F32)], axis=1)
    keys_sorted, gates_sorted = lax.sort((keys, gates_in), dimension=1, num_keys=1)
    local_sorted = keys_sorted & ((1 << tok_bits) - 1)
    rows_sorted = jnp.where(local_sorted < n_assign, local_sorted // TOP_K * (d // LANES), 0)
    list_len = -(-(keys.shape[1] + EXPERT_ROWS - LANES) // (8 * LANES)) * (8 * LANES)
    fill = ((0, 0), (0, list_len - keys.shape[1]))
    rows_flat = jnp.pad(rows_sorted, fill).reshape(-1)
    gates2d = jnp.pad(gates_sorted, fill).reshape(n_groups * (list_len // LANES), LANES)
    e_row = jnp.arange(n_experts, dtype=jnp.int32)
    later = (e_row[None, None, :] > e_row[None, :, None]) & (counts[:, None, :] > 0)
    first_later = jnp.min(jnp.where(later, e_row[None, None, :], n_experts), axis=2)
    next_off = jnp.sum(jnp.where(first_later[:, :, None] == e_row[None, None, :], offs[:, None, :], 0), axis=2)
    flat = lambda a: a.reshape(-1).astype(jnp.int32)

    out = _moe(flat(counts), flat(offs), flat(next_off), rows_flat, gates2d,
               hn3, hs3, wg, wu, wd, row(final_norm_g))
    return out.reshape(bsz, s, d)
```

```python
import functools
import math

import numpy as np
import jax
import jax.numpy as jnp
from jax import lax
from jax.experimental import pallas as pl
from jax.experimental.pallas import tpu as pltpu

A_HEAD_DIM = 64
B_GROUP_DIM = 64
TOP_K = 8
N_EXPERT_GROUPS = 8
TOPK_GROUPS = 4
ROUTED_SCALE = 2.5
EPS = 1e-6

LANES = 128
FRONT_ROWS = 512
EXPERT_ROWS = 128
LIST_PAD = 1024
DISPATCH_TOKENS = 512
COMBINE_TOKENS = 128
GROUP_TOKENS = 2048
FFT_RADIX = 64
FFT_PAD = 8
VMEM_LIMIT = 56 * 1024 * 1024

F32 = jnp.float32
BF16 = jnp.bfloat16


def _rms(x, g):
    return x * lax.rsqrt(jnp.mean(x * x, axis=-1, keepdims=True) + EPS) * g


def _gelu(x):
    return 0.5 * x * (1.0 + lax.erf(x * (1.0 / math.sqrt(2.0))))


def _silu(x):
    return x / (1.0 + jnp.exp(-x))


def _front_kernel(x_ref, g1_ref, win_ref, lng_ref, lnb_ref, ws_ref, bs_ref, ga_ref,
                  an_ref, zb_ref, ya_scr, *, d_a, chunk):
    x = x_ref[...]
    xn = _rms(x, g1_ref[...]).astype(BF16)
    z = jnp.dot(xn, win_ref[...], preferred_element_type=F32)
    for q in range(zb_ref.shape[0]):
        zb_ref[q] = z[:, 2 * d_a + q * LANES:2 * d_a + (q + 1) * LANES].astype(BF16)
    u = _gelu(z[:, :d_a])
    v = _gelu(z[:, d_a:2 * d_a])
    mu = jnp.mean(v, axis=-1, keepdims=True)
    vc = v - mu
    v = vc * lax.rsqrt(jnp.mean(vc * vc, axis=-1, keepdims=True) + EPS) * lng_ref[...] + lnb_ref[...]
    rows = x.shape[0]
    lane = lax.broadcasted_iota(jnp.int32, (chunk, LANES), 1)
    low = lane < A_HEAD_DIM
    for c in range(rows // chunk):
        r0 = c * chunk
        for hp in range(d_a // LANES):
            c0 = hp * LANES
            vp = v[r0:r0 + chunk, c0:c0 + LANES]
            rhs = jnp.concatenate([jnp.where(low, vp, 0.0), jnp.where(low, 0.0, vp)], axis=0).astype(BF16)
            mixed = jnp.dot(ws_ref[hp], rhs, preferred_element_type=F32) + bs_ref[:, c0:c0 + LANES]
            ya_scr[r0:r0 + chunk, c0:c0 + LANES] = u[r0:r0 + chunk, c0:c0 + LANES] * mixed
    an_ref[...] = _rms(ya_scr[...], ga_ref[...]).astype(BF16)


def _front(x2, g1, win, lng, lnb, ws_pair, bs_full, ga, *, d_a, d_b, chunk):
    t, d = x2.shape
    d_in = win.shape[1]
    rows = FRONT_ROWS
    const = lambda shape: pl.BlockSpec(shape, lambda i: (0,) * len(shape))
    return pl.pallas_call(
        functools.partial(_front_kernel, d_a=d_a, chunk=chunk),
        grid=(t // rows,),
        in_specs=[
            pl.BlockSpec((rows, d), lambda i: (i, 0)),
            const((1, d)), const((d, d_in)), const((1, d_a)), const((1, d_a)),
            const(ws_pair.shape), const(bs_full.shape), const((1, d_a)),
        ],
        out_specs=[pl.BlockSpec((rows, d_a), lambda i: (i, 0)),
                   pl.BlockSpec((d_b // LANES, rows, LANES), lambda i: (0, i, 0))],
        out_shape=[jax.ShapeDtypeStruct((t, d_a), BF16), jax.ShapeDtypeStruct((d_b // LANES, t, LANES), BF16)],
        scratch_shapes=[pltpu.VMEM((rows, d_a), F32)],
        compiler_params=pltpu.CompilerParams(dimension_semantics=("parallel",),
                                             vmem_limit_bytes=VMEM_LIMIT),
        name="front",
    )(x2, g1, win, lng, lnb, ws_pair, bs_full, ga)


def _fnet_tables(seq, half_w):
    r = FFT_RADIX
    assert seq == r * r
    j = np.arange(B_GROUP_DIM)
    ang = 2.0 * np.pi * np.outer(j, j) / B_GROUP_DIM
    ng = half_w // B_GROUP_DIM
    eye = np.eye(ng)
    m_ch = np.concatenate([np.kron(eye, np.cos(ang)), np.kron(eye, np.sin(ang))], axis=1)
    k1 = np.arange(r)[:, None]
    n1 = np.arange(r)[None, :]
    d_tw = np.zeros((r, 2 * r, 2 * r))
    for n2 in range(r):
        th = 2.0 * np.pi * (k1 * n1 / r + k1 * n2 / (r * r))
        c, s = np.cos(th), np.sin(th)
        d_tw[n2] = np.block([[c, -s], [s, c]])
    ph = 2.0 * np.pi * np.outer(np.arange(r), np.arange(r)) / r
    scale = 1.0 / math.sqrt(seq * B_GROUP_DIM)
    d2 = np.concatenate([np.cos(ph), -np.sin(ph)], axis=1) * scale
    return (jnp.asarray(m_ch, F32), jnp.asarray(d_tw, F32), jnp.asarray(d2, F32))


def _fnet_kernel(z_ref, mch32_ref, dtw_ref, d232_ref, gb_ref, out_ref, w_scr, a2_scr, y_scr, mch_ref, d2_ref,
                 *, n_col, unroll):
    r = FFT_RADIX
    rp = r + FFT_PAD
    ap = 2 * r + FFT_PAD
    hf = pl.program_id(1)
    half_w = n_col * LANES
    mch_ref[...] = mch32_ref[...].astype(BF16)
    d2_ref[...] = d232_ref[...].astype(BF16)

    chunk = 8 * r
    for c in range(r * r // chunk):
        zc = jnp.concatenate([z_ref[q, c * chunk:(c + 1) * chunk, :] for q in range(n_col)], axis=1)
        w = jnp.dot(zc, mch_ref[...], preferred_element_type=F32)
        for j in range(chunk // r):
            n1 = c * (chunk // r) + j
            for p in range(2 * n_col):
                w_scr[p, n1 * rp:n1 * rp + r, :] = w[j * r:(j + 1) * r, p * LANES:(p + 1) * LANES]

    def stage1(it, carry):
        for u in range(unroll):
            n2 = it * unroll + u
            rhs = jnp.concatenate(
                [jnp.concatenate([w_scr[ri * n_col + q, pl.ds(n2, r, stride=rp), :] for q in range(n_col)], axis=1)
                 for ri in range(2)], axis=0).astype(BF16)
            a = jnp.dot(dtw_ref[n2].astype(BF16), rhs, preferred_element_type=F32)
            for q in range(n_col):
                a2_scr[q, pl.ds(n2, r, stride=ap), :] = a[:r, q * LANES:(q + 1) * LANES]
                a2_scr[q, pl.ds(r + n2, r, stride=ap), :] = a[r:, q * LANES:(q + 1) * LANES]
        return carry

    lax.fori_loop(0, r // unroll, stage1, 0)

    def stage2(it, carry):
        for u in range(unroll):
            k1 = it * unroll + u
            row0 = pl.multiple_of(k1 * ap, 8)
            blk = jnp.concatenate([a2_scr[q, pl.ds(row0, 2 * r), :] for q in range(n_col)],
                                  axis=1).astype(BF16)
            y = jnp.dot(d2_ref[...], blk, preferred_element_type=F32)
            for q in range(n_col):
                y_scr[hf * n_col + q, pl.ds(k1, r, stride=rp), :] = y[:, q * LANES:(q + 1) * LANES]
        return carry

    lax.fori_loop(0, r // unroll, stage2, 0)

    @pl.when(hf == pl.num_programs(1) - 1)
    def _():
        n_all = y_scr.shape[0]
        for k2 in range(r):
            ys = [y_scr[q, k2 * rp:k2 * rp + r, :] for q in range(n_all)]
            ss = sum(jnp.sum(y * y, axis=-1, keepdims=True) for y in ys)
            inv = lax.rsqrt(ss * (1.0 / (n_all * LANES)) + EPS)
            for q in range(n_all):
                out_ref[0, k2 * r:(k2 + 1) * r, q * LANES:(q + 1) * LANES] = (
                    ys[q] * inv * gb_ref[:, q * LANES:(q + 1) * LANES]).astype(BF16)


def _fnet(zb_cols, gb, bsz):
    n_all, t, _ = zb_cols.shape
    s = t // bsz
    d_b = n_all * LANES
    n_col = 2
    r = FFT_RADIX
    rp, ap = r + FFT_PAD, 2 * r + FFT_PAD
    m_ch, d_tw, d2 = _fnet_tables(s, n_col * LANES)
    return pl.pallas_call(
        functools.partial(_fnet_kernel, n_col=n_col, unroll=8),
        grid=(bsz, n_all // n_col),
        in_specs=[
            pl.BlockSpec((n_col, s, LANES), lambda i, h: (h, i, 0)),
            pl.BlockSpec(m_ch.shape, lambda i, h: (0, 0)),
            pl.BlockSpec(d_tw.shape, lambda i, h: (0, 0, 0)),
            pl.BlockSpec(d2.shape, lambda i, h: (0, 0)),
            pl.BlockSpec((1, d_b), lambda i, h: (0, 0)),
        ],
        out_specs=pl.BlockSpec((1, s, d_b), lambda i, h: (i, 0, 0)),
        out_shape=jax.ShapeDtypeStruct((bsz, s, d_b), BF16),
        scratch_shapes=[pltpu.VMEM((2 * n_col, r * rp, LANES), F32), pltpu.VMEM((n_col, r * ap, LANES), F32),
                        pltpu.VMEM((n_all, r * rp, LANES), F32),
                        pltpu.VMEM(m_ch.shape, BF16), pltpu.VMEM(d2.shape, BF16)],
        compiler_params=pltpu.CompilerParams(dimension_semantics=("parallel", "arbitrary"),
                                             vmem_limit_bytes=VMEM_LIMIT),
        name="fnet",
    )(zb_cols, m_ch, d_tw, d2, gb)


def _post_kernel(x_ref, an_ref, bn_ref, woa_ref, wob_ref, g2_ref, wrt_ref, rb_ref, wsgu_ref, wsd_ref,
                 hs_ref, hn_ref, eidx_ref, gate_ref, cnt_ref, carry_scr,
                 *, n_experts, d_shared, steps_per_group):
    i = pl.program_id(0)

    @pl.when(i % steps_per_group == 0)
    def _():
        carry_scr[...] = jnp.zeros_like(carry_scr)

    h = (x_ref[...]
         + jnp.dot(an_ref[...], woa_ref[...], preferred_element_type=F32)
         + jnp.dot(bn_ref[...], wob_ref[...], preferred_element_type=F32))
    hn = _rms(h, g2_ref[...])
    hnb = hn.astype(BF16)
    gu = jnp.dot(hnb, wsgu_ref[...], preferred_element_type=F32)
    act = (_silu(gu[:, :d_shared]) * gu[:, d_shared:]).astype(BF16)
    hs = h + jnp.dot(act, wsd_ref[...], preferred_element_type=F32)
    n_slab = hn.shape[1] // LANES
    for s in range(n_slab):
        hn_ref[pl.ds(s, hn.shape[0], stride=n_slab), :] = hn[:, s * LANES:(s + 1) * LANES]
        hs_ref[pl.ds(s, hn.shape[0], stride=n_slab), :] = hs[:, s * LANES:(s + 1) * LANES]

    rows = hn.shape[0]
    eg = n_experts // N_EXPERT_GROUPS
    logits_t = lax.dot_general(wrt_ref[...], hn, (((1,), (1,)), ((), ())),
                               precision=lax.Precision.HIGHEST, preferred_element_type=F32)
    scores_t = 1.0 / (1.0 + jnp.exp(-logits_t))
    biased_t = scores_t + rb_ref[...]
    sub = lax.broadcasted_iota(jnp.int32, (eg, rows), 0)
    neg = -jnp.inf
    grp = [biased_t[g * eg:(g + 1) * eg, :] for g in range(N_EXPERT_GROUPS)]
    sco = [scores_t[g * eg:(g + 1) * eg, :] for g in range(N_EXPERT_GROUPS)]
    gid = [sub + g * eg for g in range(N_EXPERT_GROUPS)]

    gscore = []
    for g in range(N_EXPERT_GROUPS):
        m1 = jnp.max(grp[g], axis=0, keepdims=True)
        first = jnp.min(jnp.where(grp[g] == m1, sub, eg), axis=0, keepdims=True)
        m2 = jnp.max(jnp.where(sub == first, neg, grp[g]), axis=0, keepdims=True)
        gscore.append(m1 + m2)
    masked = []
    for g in range(N_EXPERT_GROUPS):
        beaten = jnp.zeros((1, rows), jnp.int32)
        for j in range(N_EXPERT_GROUPS):
            if j == g:
                continue
            wins = (gscore[j] > gscore[g]) if j > g else (gscore[j] >= gscore[g])
            beaten = beaten + wins.astype(jnp.int32)
        masked.append(jnp.where(beaten < TOPK_GROUPS, grp[g], neg))

    sel = [jnp.zeros((eg, rows), F32) for _ in range(N_EXPERT_GROUPS)]
    e_rows, w_rows = [], []
    for _ in range(TOP_K):
        m = masked[0]
        for g in range(1, N_EXPERT_GROUPS):
            m = jnp.maximum(m, masked[g])
        mk = jnp.max(m, axis=0, keepdims=True)
        cand = jnp.where(masked[0] == mk, gid[0], n_experts)
        for g in range(1, N_EXPERT_GROUPS):
            cand = jnp.minimum(cand, jnp.where(masked[g] == mk, gid[g], n_experts))
        ik = jnp.min(cand, axis=0, keepdims=True)
        wk = jnp.zeros((eg, rows), F32)
        for g in range(N_EXPERT_GROUPS):
            hit = gid[g] == ik
            masked[g] = jnp.where(hit, neg, masked[g])
            sel[g] = jnp.where(hit, 1.0, sel[g])
            wk = wk + jnp.where(hit, sco[g], 0.0)
        e_rows.append(ik)
        w_rows.append(jnp.sum(wk, axis=0, keepdims=True))
    wsum = w_rows[0]
    for k in range(1, TOP_K):
        wsum = wsum + w_rows[k]
    gates = [w_rows[k] / wsum * ROUTED_SCALE for k in range(TOP_K)]

    sel_t = jnp.concatenate(sel, axis=0)
    new_carry = carry_scr[...] + jnp.sum(sel_t, axis=1, keepdims=True)
    carry_scr[...] = new_carry
    cnt_ref[0] = new_carry.astype(jnp.int32)
    eidx_ref[...] = jnp.concatenate(e_rows, axis=0)
    gate_ref[...] = jnp.concatenate(gates, axis=0)


def _post(x2, an, bn, woa, wob, g2, wrt, rb, wsgu, wsd):
    t, d = x2.shape
    rows = FRONT_ROWS
    n_experts = wrt.shape[0]
    d_shared = wsd.shape[0]
    n_slab = d // LANES
    steps_per_group = GROUP_TOKENS // rows
    n_groups = t // GROUP_TOKENS
    const = lambda shape: pl.BlockSpec(shape, lambda i: (0,) * len(shape))
    row_blk = lambda w: pl.BlockSpec((rows, w), lambda i: (i, 0))
    slab_blk = pl.BlockSpec((rows * n_slab, LANES), lambda i: (i, 0))
    tok_blk = pl.BlockSpec((TOP_K, rows), lambda i: (0, i))
    return pl.pallas_call(
        functools.partial(_post_kernel, n_experts=n_experts, d_shared=d_shared,
                          steps_per_group=steps_per_group),
        grid=(t // rows,),
        in_specs=[row_blk(d), row_blk(an.shape[1]), row_blk(bn.shape[1]),
                  const(woa.shape), const(wob.shape), const((1, d)), const(wrt.shape), const(rb.shape),
                  const(wsgu.shape), const(wsd.shape)],
        out_specs=[slab_blk, slab_blk, tok_blk, tok_blk,
                   pl.BlockSpec((1, n_experts, LANES), lambda i: (i // steps_per_group, 0, 0))],
        out_shape=[jax.ShapeDtypeStruct((t * n_slab, LANES), F32), jax.ShapeDtypeStruct((t * n_slab, LANES), F32),
                   jax.ShapeDtypeStruct((TOP_K, t), jnp.int32), jax.ShapeDtypeStruct((TOP_K, t), F32),
                   jax.ShapeDtypeStruct((n_groups, n_experts, LANES), jnp.int32)],
        scratch_shapes=[pltpu.VMEM((n_experts, LANES), F32)],
        compiler_params=pltpu.CompilerParams(dimension_semantics=("arbitrary",),
                                             vmem_limit_bytes=VMEM_LIMIT),
        name="post",
    )(x2, an, bn, woa, wob, g2, wrt, rb, wsgu, wsd)


def _dispatch_kernel(pend_ref, dest_ref, hn_ref, xs_hbm, zero_scr, zsem, sem, *, n_experts):
    i = pl.program_id(0)
    blk = EXPERT_ROWS
    tokens = DISPATCH_TOKENS

    def zero_copy(e):
        start = pl.multiple_of(jnp.maximum(pend_ref[e] - blk, 0), blk)
        return pltpu.make_async_copy(zero_scr, xs_hbm.at[pl.ds(start, blk)], zsem)

    @pl.when(i == 0)
    def _():
        zero_scr[...] = jnp.zeros_like(zero_scr)

        def start_body(e, c):
            zero_copy(e).start()
            return c

        def wait_body(e, c):
            zero_copy(e).wait()
            return c

        lax.fori_loop(0, n_experts, start_body, 0)
        lax.fori_loop(0, n_experts, wait_body, 0)

        def tail_copy(b):
            return pltpu.make_async_copy(zero_scr, xs_hbm.at[pl.ds(pl.multiple_of(b * blk, blk), blk)], zsem)

        def tail_start(b, c):
            tail_copy(b).start()
            return c

        def tail_wait(b, c):
            tail_copy(b).wait()
            return c

        first_free = pend_ref[n_experts - 1] // blk
        lax.fori_loop(first_free, xs_hbm.shape[0] // blk, tail_start, 0)
        lax.fori_loop(first_free, xs_hbm.shape[0] // blk, tail_wait, 0)

    def issue(j, c):
        for k in range(TOP_K):
            pltpu.make_async_copy(hn_ref.at[pl.ds(j, 1)], xs_hbm.at[pl.ds(dest_ref[j * TOP_K + k], 1)],
                                  sem).start(priority=k % 2)
        return c

    lax.fori_loop(0, tokens, issue, 0)
    for k in range(TOP_K):
        pltpu.make_async_copy(hn_ref, xs_hbm.at[pl.ds(0, tokens)], sem).wait()


def _dispatch(pad_end, dest_flat, hn, n_rows):
    t, d = hn.shape
    n_experts = pad_end.shape[0]
    tokens = DISPATCH_TOKENS
    grid_spec = pltpu.PrefetchScalarGridSpec(
        num_scalar_prefetch=1,
        grid=(t // tokens,),
        in_specs=[pl.BlockSpec((tokens * TOP_K,), lambda i, pe: (i,), memory_space=pltpu.SMEM),
                  pl.BlockSpec((tokens, d), lambda i, pe: (i, 0))],
        out_specs=pl.BlockSpec(memory_space=pl.ANY),
        scratch_shapes=[pltpu.VMEM((EXPERT_ROWS, d), F32), pltpu.SemaphoreType.DMA, pltpu.SemaphoreType.DMA],
    )
    return pl.pallas_call(
        functools.partial(_dispatch_kernel, n_experts=n_experts),
        grid_spec=grid_spec,
        out_shape=jax.ShapeDtypeStruct((n_rows, d), F32),
        compiler_params=pltpu.CompilerParams(dimension_semantics=("arbitrary",), has_side_effects=True,
                                             disable_bounds_checks=True),
        name="dispatch",
    )(pad_end, dest_flat, hn)


def _expert_kernel(be_ref, nu_ref, xs_ref, wgu_ref, wd_ref, ys_ref, *, d_expert):
    i = pl.program_id(0)

    @pl.when(i < nu_ref[0])
    def _():
        xb = xs_ref[...].astype(BF16)
        gu = jnp.dot(xb, wgu_ref[0], preferred_element_type=F32)
        act = (_silu(gu[:, :d_expert]) * gu[:, d_expert:]).astype(BF16)
        ys_ref[...] = jnp.dot(act, wd_ref[0], preferred_element_type=F32)

    @pl.when(i >= nu_ref[0])
    def _():
        ys_ref[...] = jnp.zeros_like(ys_ref)


def _experts(block_e, n_used, xs, wgu, wd):
    n_rows, d = xs.shape
    blk = EXPERT_ROWS
    d_expert = wd.shape[1]
    grid_spec = pltpu.PrefetchScalarGridSpec(
        num_scalar_prefetch=2,
        grid=(n_rows // blk,),
        in_specs=[pl.BlockSpec((blk, d), lambda i, be, nu: (jnp.minimum(i, nu[0] - 1), 0)),
                  pl.BlockSpec((1, d, 2 * d_expert), lambda i, be, nu: (be[i], 0, 0)),
                  pl.BlockSpec((1, d_expert, d), lambda i, be, nu: (be[i], 0, 0))],
        out_specs=pl.BlockSpec((blk, d), lambda i, be, nu: (i, 0)),
    )
    return pl.pallas_call(
        functools.partial(_expert_kernel, d_expert=d_expert),
        grid_spec=grid_spec,
        out_shape=jax.ShapeDtypeStruct((n_rows, d), F32),
        compiler_params=pltpu.CompilerParams(dimension_semantics=("arbitrary",),
                                             vmem_limit_bytes=VMEM_LIMIT),
        name="experts",
    )(block_e, n_used, xs, wgu, wd)


def _combine_kernel(dcur_ref, dnext_ref, gate_ref, hs_ref, gf_ref, ys_hbm, out_ref, buf, sem):
    i = pl.program_id(0)
    n = pl.num_programs(0)
    tokens = COMBINE_TOKENS
    slot = i % 2

    def row_copy(dref, s, j, k):
        return pltpu.make_async_copy(ys_hbm.at[pl.ds(dref[j * TOP_K + k], 1)],
                                     buf.at[s, k, pl.ds(j, 1)], sem.at[s])

    def issue(dref, s):
        def body(j, c):
            for k in range(TOP_K):
                row_copy(dref, s, j, k).start(priority=k % 2)
            return c
        lax.fori_loop(0, tokens, body, 0)

    @pl.when(i == 0)
    def _():
        issue(dcur_ref, 0)

    @pl.when(i + 1 < n)
    def _():
        issue(dnext_ref, 1 - slot)

    for k in range(TOP_K):
        pltpu.make_async_copy(ys_hbm.at[pl.ds(0, tokens)], buf.at[slot, k], sem.at[slot]).wait()

    acc = hs_ref[...]
    g = gate_ref[...]
    for k in range(TOP_K):
        acc = acc + g[:, k:k + 1] * buf[slot, k]
    out_ref[...] = _rms(acc, gf_ref[...])


def _combine(dest_flat, gate_tk, hs, gf, ys):
    t, d = hs.shape
    tokens = COMBINE_TOKENS
    n = t // tokens
    dspec = lambda f: pl.BlockSpec((tokens * TOP_K,), f, memory_space=pltpu.SMEM)
    return pl.pallas_call(
        _combine_kernel,
        grid=(n,),
        in_specs=[dspec(lambda i: (i,)), dspec(lambda i: (jnp.minimum(i + 1, n - 1),)),
                  pl.BlockSpec((tokens, TOP_K), lambda i: (i, 0)),
                  pl.BlockSpec((tokens, d), lambda i: (i, 0)),
                  pl.BlockSpec((1, d), lambda i: (0, 0)),
                  pl.BlockSpec(memory_space=pl.ANY)],
        out_specs=pl.BlockSpec((tokens, d), lambda i: (i, 0)),
        out_shape=jax.ShapeDtypeStruct((t, d), F32),
        scratch_shapes=[pltpu.VMEM((2, TOP_K, tokens, d), F32), pltpu.SemaphoreType.DMA((2,))],
        compiler_params=pltpu.CompilerParams(dimension_semantics=("arbitrary",),
                                             vmem_limit_bytes=VMEM_LIMIT, disable_bounds_checks=True),
        name="combine",
    )(dest_flat, dest_flat, gate_tk, hs, gf, ys)


def _moe_kernel(cnt_ref, off_ref, next_ref, list_ref, gate_ref, hn_ref, hs_hbm, wg_ref, wu_ref, wd_ref, gf_ref,
                out_ref, acc_scr, x_scr, y_scr, state_scr, sem, *, d_expert, n_slab):
    g = pl.program_id(0)
    e = pl.program_id(1)
    n_e = pl.num_programs(1)
    tokens = GROUP_TOKENS
    blk = EXPERT_ROWS
    batch = 16

    def slab(ref, row0):
        return ref.at[pl.ds(pl.multiple_of(row0, n_slab), n_slab), :]

    def acc_init():
        return pltpu.make_async_copy(hs_hbm.at[pl.ds(pl.multiple_of(g * tokens * n_slab, n_slab), tokens * n_slab), :],
                                     acc_scr.at[pl.ds(0, tokens * n_slab), :], sem)

    def gather(base):
        for r in range(blk):
            slab(x_scr, r * n_slab)[...] = slab(hn_ref, list_ref[base + r] & -n_slab)[...]

    def scatter(base):
        for j in range(blk // batch):
            rows, vals = [], []
            for u in range(batch):
                r = j * batch + u
                a = list_ref[base + r]
                row0 = a & -n_slab
                rows.append(row0)
                vals.append(slab(acc_scr, row0)[...] + gate_ref[a] * slab(y_scr, r * n_slab)[...])
            for u in reversed(range(batch)):
                slab(acc_scr, rows[u])[...] = vals[u]

    n = cnt_ref[g * n_e + e]
    off = off_ref[g * n_e + e]

    @pl.when(e == 0)
    def _():
        acc_init().start()
        y_scr[...] = jnp.zeros_like(y_scr)
        state_scr[0] = 0
        gather(jnp.where(n > 0, off, next_ref[g * n_e + e]))
        acc_init().wait()

    n_blocks = (n + blk - 1) // blk
    nxt = next_ref[g * n_e + e]
    row_id = lax.broadcasted_iota(jnp.int32, (blk, 1), 0)

    def block(i, carry):
        base = off + i * blk
        xb = jnp.concatenate([x_scr[pl.ds(s, blk, stride=n_slab), :] for s in range(n_slab)], axis=1).astype(BF16)
        scatter(state_scr[0])
        gather(jnp.where(i + 1 < n_blocks, base + blk, nxt))
        act = (_silu(jnp.dot(xb, wg_ref[0], preferred_element_type=F32))
               * jnp.dot(xb, wu_ref[0], preferred_element_type=F32)).astype(BF16)
        y = jnp.dot(act, wd_ref[0], preferred_element_type=F32)
        y = jnp.where(row_id < n - i * blk, y, 0.0)
        for s in range(n_slab):
            y_scr[pl.ds(s, blk, stride=n_slab), :] = y[:, s * LANES:(s + 1) * LANES]
        state_scr[0] = base
        return carry

    lax.fori_loop(0, n_blocks, block, 0)

    @pl.when(e == n_e - 1)
    def _():
        scatter(state_scr[0])
        rows = 256
        for c in range(tokens // rows):
            acc = jnp.concatenate([acc_scr[pl.ds(c * rows * n_slab + s, rows, stride=n_slab), :]
                                   for s in range(n_slab)], axis=1)
            out_ref[c * rows:(c + 1) * rows, :] = _rms(acc, gf_ref[...])


def _moe(cnt_flat, off_flat, next_flat, list_flat, gate_flat, hn3, hs3, wg, wu, wd, gf):
    n_slab = gf.shape[1] // LANES
    t = hn3.shape[0] // n_slab
    d = gf.shape[1]
    n_experts, d_expert = wd.shape[0], wd.shape[1]
    tokens = GROUP_TOKENS
    n_groups = t // tokens
    n_assign = tokens * TOP_K
    assert TOP_K == n_slab, "assignment index -> slab row uses TOP_K == d_model / 128"
    assert EXPERT_ROWS % LANES == 0 and n_assign & (n_assign - 1) == 0
    grid_spec = pltpu.PrefetchScalarGridSpec(
        num_scalar_prefetch=3,
        grid=(n_groups, n_experts),
        in_specs=[pl.BlockSpec((n_assign + LIST_PAD,), lambda g, e, *_: (g,), memory_space=pltpu.SMEM),
                  pl.BlockSpec((n_assign,), lambda g, e, *_: (g,), memory_space=pltpu.SMEM),
                  pl.BlockSpec((tokens * n_slab, LANES), lambda g, e, *_: (g, 0)),
                  pl.BlockSpec(memory_space=pl.ANY),
                  pl.BlockSpec((1, d, d_expert), lambda g, e, *_: (e, 0, 0)),
                  pl.BlockSpec((1, d, d_expert), lambda g, e, *_: (e, 0, 0)),
                  pl.BlockSpec((1, d_expert, d), lambda g, e, *_: (e, 0, 0)),
                  pl.BlockSpec((1, d), lambda g, e, *_: (0, 0))],
        out_specs=pl.BlockSpec((tokens, d), lambda g, e, *_: (g, 0)),
        scratch_shapes=[pltpu.VMEM((tokens * n_slab, LANES), F32),
                        pltpu.VMEM((EXPERT_ROWS * n_slab, LANES), F32),
                        pltpu.VMEM((EXPERT_ROWS * n_slab, LANES), F32),
                        pltpu.SMEM((1,), jnp.int32),
                        pltpu.SemaphoreType.DMA],
    )
    return pl.pallas_call(
        functools.partial(_moe_kernel, d_expert=d_expert, n_slab=n_slab),
        grid_spec=grid_spec,
        out_shape=jax.ShapeDtypeStruct((t, d), F32),
        compiler_params=pltpu.CompilerParams(dimension_semantics=("arbitrary", "arbitrary"),
                                             vmem_limit_bytes=VMEM_LIMIT),
        name="moe",
    )(cnt_flat, off_flat, next_flat, list_flat, gate_flat, hn3, hs3, wg, wu, wd, gf)


def kernel(x, norm1_g, w_in, sgu_ln_g, sgu_ln_b, w_spatial, b_spatial, out_norm_a_g, out_norm_b_g, w_out,
           norm2_g, w_router, router_bias, w_gate, w_up, w_down, ws_gate, ws_up, ws_down, final_norm_g):
    bsz, s, d = x.shape
    depth = w_in.shape[0]
    assert depth == 1, "the fused final norm assumes a single trunk layer"
    l = 0
    t = bsz * s
    d_a = sgu_ln_g.shape[-1]
    d_b = out_norm_b_g.shape[-1]
    n_heads, chunk = w_spatial.shape[1], w_spatial.shape[2]
    n_experts = w_router.shape[-1]
    assert d_a == n_heads * A_HEAD_DIM and chunk == LANES and 2 * A_HEAD_DIM == LANES
    assert t % FRONT_ROWS == 0 and FRONT_ROWS % chunk == 0 and s % FRONT_ROWS == 0
    assert t % DISPATCH_TOKENS == 0 and t % COMBINE_TOKENS == 0
    row = lambda v: v.reshape(1, -1).astype(F32)

    x2 = x.reshape(t, d)

    win = w_in[l].astype(BF16)
    ws = w_spatial[l]
    ws_pair = jnp.concatenate([ws[0::2], ws[1::2]], axis=2).astype(BF16)
    bs_full = jnp.repeat(b_spatial[l].T, A_HEAD_DIM, axis=1).astype(F32)
    woa = w_out[l][:d_a].astype(BF16)
    wob = w_out[l][d_a:].astype(BF16)
    wrt = w_router[l].T.astype(F32)
    rb = jnp.broadcast_to(router_bias[l].astype(F32)[:, None], (n_experts, 1))
    wsgu = jnp.concatenate([ws_gate[l], ws_up[l]], axis=1).astype(BF16)
    wsd = ws_down[l].astype(BF16)
    wg = w_gate[l].astype(BF16)
    wu = w_up[l].astype(BF16)
    wd = w_down[l].astype(BF16)

    an, zb = _front(x2, row(norm1_g[l]), win, row(sgu_ln_g[l]), row(sgu_ln_b[l]), ws_pair, bs_full,
                    row(out_norm_a_g[l]), d_a=d_a, d_b=d_b, chunk=chunk)
    bn = _fnet(zb, row(out_norm_b_g[l]), bsz).reshape(t, d_b)

    hs3, hn3, eidx, gate, cnt = _post(x2, an, bn, woa, wob, row(norm2_g[l]), wrt, rb, wsgu, wsd)

    n_groups = t // GROUP_TOKENS
    n_assign = GROUP_TOKENS * TOP_K
    counts = cnt[:, :, 0]
    offs = jnp.cumsum(counts, axis=1) - counts
    tok_bits = (n_assign - 1).bit_length()
    local = jnp.arange(n_assign, dtype=jnp.int32).reshape(1, GROUP_TOKENS, TOP_K)
    eidx_g = eidx.reshape(TOP_K, n_groups, GROUP_TOKENS).transpose(1, 2, 0)
    gate_flat = gate.reshape(TOP_K, n_groups, GROUP_TOKENS).transpose(1, 2, 0).reshape(-1)
    keys = ((eidx_g << tok_bits) | local).reshape(n_groups, n_assign)
    sorted_local = lax.sort(keys, dimension=1) & (n_assign - 1)
    list_flat = jnp.pad(sorted_local, ((0, 0), (0, LIST_PAD))).reshape(-1)
    e_row = jnp.arange(n_experts, dtype=jnp.int32)
    later = (e_row[None, None, :] > e_row[None, :, None]) & (counts[:, None, :] > 0)
    first_later = jnp.min(jnp.where(later, e_row[None, None, :], n_experts), axis=2)
    next_off = jnp.sum(jnp.where(first_later[:, :, None] == e_row[None, None, :], offs[:, None, :], 0), axis=2)
    flat = lambda a: a.reshape(-1).astype(jnp.int32)

    out = _moe(flat(counts), flat(offs), flat(next_off), list_flat, gate_flat,
               hn3, hs3, wg, wu, wd, row(final_norm_g))
    return out.reshape(bsz, s, d)
```

```python
import functools
import math

import numpy as np
import jax
import jax.numpy as jnp
from jax import lax
from jax.experimental import pallas as pl
from jax.experimental.pallas import tpu as pltpu

A_HEAD_DIM = 64
B_GROUP_DIM = 64
TOP_K = 8
N_EXPERT_GROUPS = 8
TOPK_GROUPS = 4
ROUTED_SCALE = 2.5
EPS = 1e-6

LANES = 128
FRONT_ROWS = 512
EXPERT_ROWS = 128
LIST_PAD = 1024
DISPATCH_TOKENS = 512
COMBINE_TOKENS = 128
GROUP_TOKENS = 2048
FFT_RADIX = 64
FFT_PAD = 8
VMEM_LIMIT = 56 * 1024 * 1024

F32 = jnp.float32
BF16 = jnp.bfloat16


def _rms(x, g):
    return x * lax.rsqrt(jnp.mean(x * x, axis=-1, keepdims=True) + EPS) * g


def _gelu(x):
    return 0.5 * x * (1.0 + lax.erf(x * (1.0 / math.sqrt(2.0))))


def _silu(x):
    return x / (1.0 + jnp.exp(-x))


def _front_kernel(x_ref, g1_ref, win_ref, lng_ref, lnb_ref, ws_ref, bs_ref, ga_ref,
                  an_ref, zb_ref, ya_scr, *, d_a, chunk):
    x = x_ref[...]
    xn = _rms(x, g1_ref[...]).astype(BF16)
    z = jnp.dot(xn, win_ref[...], preferred_element_type=F32)
    for q in range(zb_ref.shape[0]):
        zb_ref[q] = z[:, 2 * d_a + q * LANES:2 * d_a + (q + 1) * LANES].astype(BF16)
    u = _gelu(z[:, :d_a])
    v = _gelu(z[:, d_a:2 * d_a])
    mu = jnp.mean(v, axis=-1, keepdims=True)
    vc = v - mu
    v = vc * lax.rsqrt(jnp.mean(vc * vc, axis=-1, keepdims=True) + EPS) * lng_ref[...] + lnb_ref[...]
    rows = x.shape[0]
    lane = lax.broadcasted_iota(jnp.int32, (chunk, LANES), 1)
    low = lane < A_HEAD_DIM
    for c in range(rows // chunk):
        r0 = c * chunk
        for hp in range(d_a // LANES):
            c0 = hp * LANES
            vp = v[r0:r0 + chunk, c0:c0 + LANES]
            rhs = jnp.concatenate([jnp.where(low, vp, 0.0), jnp.where(low, 0.0, vp)], axis=0).astype(BF16)
            mixed = jnp.dot(ws_ref[hp], rhs, preferred_element_type=F32) + bs_ref[:, c0:c0 + LANES]
            ya_scr[r0:r0 + chunk, c0:c0 + LANES] = u[r0:r0 + chunk, c0:c0 + LANES] * mixed
    an_ref[...] = _rms(ya_scr[...], ga_ref[...]).astype(BF16)


def _front(x2, g1, win, lng, lnb, ws_pair, bs_full, ga, *, d_a, d_b, chunk):
    t, d = x2.shape
    d_in = win.shape[1]
    rows = FRONT_ROWS
    const = lambda shape: pl.BlockSpec(shape, lambda i: (0,) * len(shape))
    return pl.pallas_call(
        functools.partial(_front_kernel, d_a=d_a, chunk=chunk),
        grid=(t // rows,),
        in_specs=[
            pl.BlockSpec((rows, d), lambda i: (i, 0)),
            const((1, d)), const((d, d_in)), const((1, d_a)), const((1, d_a)),
            const(ws_pair.shape), const(bs_full.shape), const((1, d_a)),
        ],
        out_specs=[pl.BlockSpec((rows, d_a), lambda i: (i, 0)),
                   pl.BlockSpec((d_b // LANES, rows, LANES), lambda i: (0, i, 0))],
        out_shape=[jax.ShapeDtypeStruct((t, d_a), BF16), jax.ShapeDtypeStruct((d_b // LANES, t, LANES), BF16)],
        scratch_shapes=[pltpu.VMEM((rows, d_a), F32)],
        compiler_params=pltpu.CompilerParams(dimension_semantics=("parallel",),
                                             vmem_limit_bytes=VMEM_LIMIT),
        name="front",
    )(x2, g1, win, lng, lnb, ws_pair, bs_full, ga)


def _fnet_tables(seq, half_w):
    r = FFT_RADIX
    assert seq == r * r
    j = np.arange(B_GROUP_DIM)
    ang = 2.0 * np.pi * np.outer(j, j) / B_GROUP_DIM
    ng = half_w // B_GROUP_DIM
    eye = np.eye(ng)
    m_ch = np.concatenate([np.kron(eye, np.cos(ang)), np.kron(eye, np.sin(ang))], axis=1)
    k1 = np.arange(r)[:, None]
    n1 = np.arange(r)[None, :]
    d_tw = np.zeros((r, 2 * r, 2 * r))
    for n2 in range(r):
        th = 2.0 * np.pi * (k1 * n1 / r + k1 * n2 / (r * r))
        c, s = np.cos(th), np.sin(th)
        d_tw[n2] = np.block([[c, -s], [s, c]])
    ph = 2.0 * np.pi * np.outer(np.arange(r), np.arange(r)) / r
    scale = 1.0 / math.sqrt(seq * B_GROUP_DIM)
    d2 = np.concatenate([np.cos(ph), -np.sin(ph)], axis=1) * scale
    return (jnp.asarray(m_ch, F32), jnp.asarray(d_tw, F32), jnp.asarray(d2, F32))


def _fnet_kernel(z_ref, mch32_ref, dtw_ref, d232_ref, gb_ref, out_ref, w_scr, a2_scr, y_scr, mch_ref, d2_ref,
                 *, n_col, unroll):
    r = FFT_RADIX
    rp = r + FFT_PAD
    ap = 2 * r + FFT_PAD
    hf = pl.program_id(1)
    half_w = n_col * LANES
    mch_ref[...] = mch32_ref[...].astype(BF16)
    d2_ref[...] = d232_ref[...].astype(BF16)

    chunk = 8 * r
    for c in range(r * r // chunk):
        zc = jnp.concatenate([z_ref[q, c * chunk:(c + 1) * chunk, :] for q in range(n_col)], axis=1)
        w = jnp.dot(zc, mch_ref[...], preferred_element_type=F32)
        for j in range(chunk // r):
            n1 = c * (chunk // r) + j
            for p in range(2 * n_col):
                w_scr[p, n1 * rp:n1 * rp + r, :] = w[j * r:(j + 1) * r, p * LANES:(p + 1) * LANES]

    def stage1(it, carry):
        for u in range(unroll):
            n2 = it * unroll + u
            rhs = jnp.concatenate(
                [jnp.concatenate([w_scr[ri * n_col + q, pl.ds(n2, r, stride=rp), :] for q in range(n_col)], axis=1)
                 for ri in range(2)], axis=0).astype(BF16)
            a = jnp.dot(dtw_ref[n2].astype(BF16), rhs, preferred_element_type=F32)
            for q in range(n_col):
                a2_scr[q, pl.ds(n2, r, stride=ap), :] = a[:r, q * LANES:(q + 1) * LANES]
                a2_scr[q, pl.ds(r + n2, r, stride=ap), :] = a[r:, q * LANES:(q + 1) * LANES]
        return carry

    lax.fori_loop(0, r // unroll, stage1, 0)

    def stage2(it, carry):
        for u in range(unroll):
            k1 = it * unroll + u
            row0 = pl.multiple_of(k1 * ap, 8)
            blk = jnp.concatenate([a2_scr[q, pl.ds(row0, 2 * r), :] for q in range(n_col)],
                                  axis=1).astype(BF16)
            y = jnp.dot(d2_ref[...], blk, preferred_element_type=F32)
            for q in range(n_col):
                y_scr[hf * n_col + q, pl.ds(k1, r, stride=rp), :] = y[:, q * LANES:(q + 1) * LANES]
        return carry

    lax.fori_loop(0, r // unroll, stage2, 0)

    @pl.when(hf == pl.num_programs(1) - 1)
    def _():
        n_all = y_scr.shape[0]
        for k2 in range(r):
            ys = [y_scr[q, k2 * rp:k2 * rp + r, :] for q in range(n_all)]
            ss = sum(jnp.sum(y * y, axis=-1, keepdims=True) for y in ys)
            inv = lax.rsqrt(ss * (1.0 / (n_all * LANES)) + EPS)
            for q in range(n_all):
                out_ref[0, k2 * r:(k2 + 1) * r, q * LANES:(q + 1) * LANES] = (
                    ys[q] * inv * gb_ref[:, q * LANES:(q + 1) * LANES]).astype(BF16)


def _fnet(zb_cols, gb, bsz):
    n_all, t, _ = zb_cols.shape
    s = t // bsz
    d_b = n_all * LANES
    n_col = 2
    r = FFT_RADIX
    rp, ap = r + FFT_PAD, 2 * r + FFT_PAD
    m_ch, d_tw, d2 = _fnet_tables(s, n_col * LANES)
    return pl.pallas_call(
        functools.partial(_fnet_kernel, n_col=n_col, unroll=8),
        grid=(bsz, n_all // n_col),
        in_specs=[
            pl.BlockSpec((n_col, s, LANES), lambda i, h: (h, i, 0)),
            pl.BlockSpec(m_ch.shape, lambda i, h: (0, 0)),
            pl.BlockSpec(d_tw.shape, lambda i, h: (0, 0, 0)),
            pl.BlockSpec(d2.shape, lambda i, h: (0, 0)),
            pl.BlockSpec((1, d_b), lambda i, h: (0, 0)),
        ],
        out_specs=pl.BlockSpec((1, s, d_b), lambda i, h: (i, 0, 0)),
        out_shape=jax.ShapeDtypeStruct((bsz, s, d_b), BF16),
        scratch_shapes=[pltpu.VMEM((2 * n_col, r * rp, LANES), F32), pltpu.VMEM((n_col, r * ap, LANES), F32),
                        pltpu.VMEM((n_all, r * rp, LANES), F32),
                        pltpu.VMEM(m_ch.shape, BF16), pltpu.VMEM(d2.shape, BF16)],
        compiler_params=pltpu.CompilerParams(dimension_semantics=("parallel", "arbitrary"),
                                             vmem_limit_bytes=VMEM_LIMIT),
        name="fnet",
    )(zb_cols, m_ch, d_tw, d2, gb)


def _post_kernel(x_ref, an_ref, bn_ref, woa_ref, wob_ref, g2_ref, wrt_ref, rb_ref, wsgu_ref, wsd_ref,
                 hs_ref, hn_ref, eidx_ref, gate_ref, cnt_ref, carry_scr,
                 *, n_experts, d_shared, steps_per_group):
    i = pl.program_id(0)

    @pl.when(i % steps_per_group == 0)
    def _():
        carry_scr[...] = jnp.zeros_like(carry_scr)

    h = (x_ref[...]
         + jnp.dot(an_ref[...], woa_ref[...], preferred_element_type=F32)
         + jnp.dot(bn_ref[...], wob_ref[...], preferred_element_type=F32))
    hn = _rms(h, g2_ref[...])
    hnb = hn.astype(BF16)
    gu = jnp.dot(hnb, wsgu_ref[...], preferred_element_type=F32)
    act = (_silu(gu[:, :d_shared]) * gu[:, d_shared:]).astype(BF16)
    hs = h + jnp.dot(act, wsd_ref[...], preferred_element_type=F32)
    n_slab = hn.shape[1] // LANES
    for s in range(n_slab):
        hn_ref[pl.ds(s, hn.shape[0], stride=n_slab), :] = hn[:, s * LANES:(s + 1) * LANES]
        hs_ref[pl.ds(s, hn.shape[0], stride=n_slab), :] = hs[:, s * LANES:(s + 1) * LANES]

    rows = hn.shape[0]
    eg = n_experts // N_EXPERT_GROUPS
    logits_t = lax.dot_general(wrt_ref[...], hn, (((1,), (1,)), ((), ())),
                               precision=lax.Precision.HIGHEST, preferred_element_type=F32)
    scores_t = 1.0 / (1.0 + jnp.exp(-logits_t))
    biased_t = scores_t + rb_ref[...]
    sub = lax.broadcasted_iota(jnp.int32, (eg, rows), 0)
    neg = -jnp.inf
    grp = [biased_t[g * eg:(g + 1) * eg, :] for g in range(N_EXPERT_GROUPS)]
    sco = [scores_t[g * eg:(g + 1) * eg, :] for g in range(N_EXPERT_GROUPS)]
    gid = [sub + g * eg for g in range(N_EXPERT_GROUPS)]

    gscore = []
    for g in range(N_EXPERT_GROUPS):
        m1 = jnp.max(grp[g], axis=0, keepdims=True)
        first = jnp.min(jnp.where(grp[g] == m1, sub, eg), axis=0, keepdims=True)
        m2 = jnp.max(jnp.where(sub == first, neg, grp[g]), axis=0, keepdims=True)
        gscore.append(m1 + m2)
    masked = []
    for g in range(N_EXPERT_GROUPS):
        beaten = jnp.zeros((1, rows), jnp.int32)
        for j in range(N_EXPERT_GROUPS):
            if j == g:
                continue
            wins = (gscore[j] > gscore[g]) if j > g else (gscore[j] >= gscore[g])
            beaten = beaten + wins.astype(jnp.int32)
        masked.append(jnp.where(beaten < TOPK_GROUPS, grp[g], neg))

    sel = [jnp.zeros((eg, rows), F32) for _ in range(N_EXPERT_GROUPS)]
    e_rows, w_rows = [], []
    for _ in range(TOP_K):
        m = masked[0]
        for g in range(1, N_EXPERT_GROUPS):
            m = jnp.maximum(m, masked[g])
        mk = jnp.max(m, axis=0, keepdims=True)
        cand = jnp.where(masked[0] == mk, gid[0], n_experts)
        for g in range(1, N_EXPERT_GROUPS):
            cand = jnp.minimum(cand, jnp.where(masked[g] == mk, gid[g], n_experts))
        ik = jnp.min(cand, axis=0, keepdims=True)
        wk = jnp.zeros((eg, rows), F32)
        for g in range(N_EXPERT_GROUPS):
            hit = gid[g] == ik
            masked[g] = jnp.where(hit, neg, masked[g])
            sel[g] = jnp.where(hit, 1.0, sel[g])
            wk = wk + jnp.where(hit, sco[g], 0.0)
        e_rows.append(ik)
        w_rows.append(jnp.sum(wk, axis=0, keepdims=True))
    wsum = w_rows[0]
    for k in range(1, TOP_K):
        wsum = wsum + w_rows[k]
    gates = [w_rows[k] / wsum * ROUTED_SCALE for k in range(TOP_K)]

    sel_t = jnp.concatenate(sel, axis=0)
    new_carry = carry_scr[...] + jnp.sum(sel_t, axis=1, keepdims=True)
    carry_scr[...] = new_carry
    cnt_ref[0] = new_carry.astype(jnp.int32)
    eidx_ref[...] = jnp.concatenate(e_rows, axis=0)
    gate_ref[...] = jnp.concatenate(gates, axis=0)


def _post(x2, an, bn, woa, wob, g2, wrt, rb, wsgu, wsd):
    t, d = x2.shape
    rows = FRONT_ROWS
    n_experts = wrt.shape[0]
    d_shared = wsd.shape[0]
    n_slab = d // LANES
    steps_per_group = GROUP_TOKENS // rows
    n_groups = t // GROUP_TOKENS
    const = lambda shape: pl.BlockSpec(shape, lambda i: (0,) * len(shape))
    row_blk = lambda w: pl.BlockSpec((rows, w), lambda i: (i, 0))
    slab_blk = pl.BlockSpec((rows * n_slab, LANES), lambda i: (i, 0))
    tok_blk = pl.BlockSpec((TOP_K, rows), lambda i: (0, i))
    return pl.pallas_call(
        functools.partial(_post_kernel, n_experts=n_experts, d_shared=d_shared,
                          steps_per_group=steps_per_group),
        grid=(t // rows,),
        in_specs=[row_blk(d), row_blk(an.shape[1]), row_blk(bn.shape[1]),
                  const(woa.shape), const(wob.shape), const((1, d)), const(wrt.shape), const(rb.shape),
                  const(wsgu.shape), const(wsd.shape)],
        out_specs=[slab_blk, slab_blk, tok_blk, tok_blk,
                   pl.BlockSpec((1, n_experts, LANES), lambda i: (i // steps_per_group, 0, 0))],
        out_shape=[jax.ShapeDtypeStruct((t * n_slab, LANES), F32), jax.ShapeDtypeStruct((t * n_slab, LANES), F32),
                   jax.ShapeDtypeStruct((TOP_K, t), jnp.int32), jax.ShapeDtypeStruct((TOP_K, t), F32),
                   jax.ShapeDtypeStruct((n_groups, n_experts, LANES), jnp.int32)],
        scratch_shapes=[pltpu.VMEM((n_experts, LANES), F32)],
        compiler_params=pltpu.CompilerParams(dimension_semantics=("arbitrary",),
                                             vmem_limit_bytes=VMEM_LIMIT),
        name="post",
    )(x2, an, bn, woa, wob, g2, wrt, rb, wsgu, wsd)


def _dispatch_kernel(pend_ref, dest_ref, hn_ref, xs_hbm, zero_scr, zsem, sem, *, n_experts):
    i = pl.program_id(0)
    blk = EXPERT_ROWS
    tokens = DISPATCH_TOKENS

    def zero_copy(e):
        start = pl.multiple_of(jnp.maximum(pend_ref[e] - blk, 0), blk)
        return pltpu.make_async_copy(zero_scr, xs_hbm.at[pl.ds(start, blk)], zsem)

    @pl.when(i == 0)
    def _():
        zero_scr[...] = jnp.zeros_like(zero_scr)

        def start_body(e, c):
            zero_copy(e).start()
            return c

        def wait_body(e, c):
            zero_copy(e).wait()
            return c

        lax.fori_loop(0, n_experts, start_body, 0)
        lax.fori_loop(0, n_experts, wait_body, 0)

        def tail_copy(b):
            return pltpu.make_async_copy(zero_scr, xs_hbm.at[pl.ds(pl.multiple_of(b * blk, blk), blk)], zsem)

        def tail_start(b, c):
            tail_copy(b).start()
            return c

        def tail_wait(b, c):
            tail_copy(b).wait()
            return c

        first_free = pend_ref[n_experts - 1] // blk
        lax.fori_loop(first_free, xs_hbm.shape[0] // blk, tail_start, 0)
        lax.fori_loop(first_free, xs_hbm.shape[0] // blk, tail_wait, 0)

    def issue(j, c):
        for k in range(TOP_K):
            pltpu.make_async_copy(hn_ref.at[pl.ds(j, 1)], xs_hbm.at[pl.ds(dest_ref[j * TOP_K + k], 1)],
                                  sem).start(priority=k % 2)
        return c

    lax.fori_loop(0, tokens, issue, 0)
    for k in range(TOP_K):
        pltpu.make_async_copy(hn_ref, xs_hbm.at[pl.ds(0, tokens)], sem).wait()


def _dispatch(pad_end, dest_flat, hn, n_rows):
    t, d = hn.shape
    n_experts = pad_end.shape[0]
    tokens = DISPATCH_TOKENS
    grid_spec = pltpu.PrefetchScalarGridSpec(
        num_scalar_prefetch=1,
        grid=(t // tokens,),
        in_specs=[pl.BlockSpec((tokens * TOP_K,), lambda i, pe: (i,), memory_space=pltpu.SMEM),
                  pl.BlockSpec((tokens, d), lambda i, pe: (i, 0))],
        out_specs=pl.BlockSpec(memory_space=pl.ANY),
        scratch_shapes=[pltpu.VMEM((EXPERT_ROWS, d), F32), pltpu.SemaphoreType.DMA, pltpu.SemaphoreType.DMA],
    )
    return pl.pallas_call(
        functools.partial(_dispatch_kernel, n_experts=n_experts),
        grid_spec=grid_spec,
        out_shape=jax.ShapeDtypeStruct((n_rows, d), F32),
        compiler_params=pltpu.CompilerParams(dimension_semantics=("arbitrary",), has_side_effects=True,
                                             disable_bounds_checks=True),
        name="dispatch",
    )(pad_end, dest_flat, hn)


def _expert_kernel(be_ref, nu_ref, xs_ref, wgu_ref, wd_ref, ys_ref, *, d_expert):
    i = pl.program_id(0)

    @pl.when(i < nu_ref[0])
    def _():
        xb = xs_ref[...].astype(BF16)
        gu = jnp.dot(xb, wgu_ref[0], preferred_element_type=F32)
        act = (_silu(gu[:, :d_expert]) * gu[:, d_expert:]).astype(BF16)
        ys_ref[...] = jnp.dot(act, wd_ref[0], preferred_element_type=F32)

    @pl.when(i >= nu_ref[0])
    def _():
        ys_ref[...] = jnp.zeros_like(ys_ref)


def _experts(block_e, n_used, xs, wgu, wd):
    n_rows, d = xs.shape
    blk = EXPERT_ROWS
    d_expert = wd.shape[1]
    grid_spec = pltpu.PrefetchScalarGridSpec(
        num_scalar_prefetch=2,
        grid=(n_rows // blk,),
        in_specs=[pl.BlockSpec((blk, d), lambda i, be, nu: (jnp.minimum(i, nu[0] - 1), 0)),
                  pl.BlockSpec((1, d, 2 * d_expert), lambda i, be, nu: (be[i], 0, 0)),
                  pl.BlockSpec((1, d_expert, d), lambda i, be, nu: (be[i], 0, 0))],
        out_specs=pl.BlockSpec((blk, d), lambda i, be, nu: (i, 0)),
    )
    return pl.pallas_call(
        functools.partial(_expert_kernel, d_expert=d_expert),
        grid_spec=grid_spec,
        out_shape=jax.ShapeDtypeStruct((n_rows, d), F32),
        compiler_params=pltpu.CompilerParams(dimension_semantics=("arbitrary",),
                                             vmem_limit_bytes=VMEM_LIMIT),
        name="experts",
    )(block_e, n_used, xs, wgu, wd)


def _combine_kernel(dcur_ref, dnext_ref, gate_ref, hs_ref, gf_ref, ys_hbm, out_ref, buf, sem):
    i = pl.program_id(0)
    n = pl.num_programs(0)
    tokens = COMBINE_TOKENS
    slot = i % 2

    def row_copy(dref, s, j, k):
        return pltpu.make_async_copy(ys_hbm.at[pl.ds(dref[j * TOP_K + k], 1)],
                                     buf.at[s, k, pl.ds(j, 1)], sem.at[s])

    def issue(dref, s):
        def body(j, c):
            for k in range(TOP_K):
                row_copy(dref, s, j, k).start(priority=k % 2)
            return c
        lax.fori_loop(0, tokens, body, 0)

    @pl.when(i == 0)
    def _():
        issue(dcur_ref, 0)

    @pl.when(i + 1 < n)
    def _():
        issue(dnext_ref, 1 - slot)

    for k in range(TOP_K):
        pltpu.make_async_copy(ys_hbm.at[pl.ds(0, tokens)], buf.at[slot, k], sem.at[slot]).wait()

    acc = hs_ref[...]
    g = gate_ref[...]
    for k in range(TOP_K):
        acc = acc + g[:, k:k + 1] * buf[slot, k]
    out_ref[...] = _rms(acc, gf_ref[...])


def _combine(dest_flat, gate_tk, hs, gf, ys):
    t, d = hs.shape
    tokens = COMBINE_TOKENS
    n = t // tokens
    dspec = lambda f: pl.BlockSpec((tokens * TOP_K,), f, memory_space=pltpu.SMEM)
    return pl.pallas_call(
        _combine_kernel,
        grid=(n,),
        in_specs=[dspec(lambda i: (i,)), dspec(lambda i: (jnp.minimum(i + 1, n - 1),)),
                  pl.BlockSpec((tokens, TOP_K), lambda i: (i, 0)),
                  pl.BlockSpec((tokens, d), lambda i: (i, 0)),
                  pl.BlockSpec((1, d), lambda i: (0, 0)),
                  pl.BlockSpec(memory_space=pl.ANY)],
        out_specs=pl.BlockSpec((tokens, d), lambda i: (i, 0)),
        out_shape=jax.ShapeDtypeStruct((t, d), F32),
        scratch_shapes=[pltpu.VMEM((2, TOP_K, tokens, d), F32), pltpu.SemaphoreType.DMA((2,))],
        compiler_params=pltpu.CompilerParams(dimension_semantics=("arbitrary",),
                                             vmem_limit_bytes=VMEM_LIMIT, disable_bounds_checks=True),
        name="combine",
    )(dest_flat, dest_flat, gate_tk, hs, gf, ys)


def _moe_kernel(cnt_ref, off_ref, next_ref, rows_ref, gates_ref, hn_ref, hs_hbm, wg_ref, wu_ref, wd_ref, gf_ref,
                out_ref, acc_scr, x_scr, y_scr, state_scr, sem, *, d_expert, n_slab):
    g = pl.program_id(0)
    e = pl.program_id(1)
    n_e = pl.num_programs(1)
    tokens = GROUP_TOKENS
    blk = EXPERT_ROWS
    batch = 16

    def slab(ref, row0):
        return ref.at[pl.ds(pl.multiple_of(row0, n_slab), n_slab), :]

    def acc_init():
        return pltpu.make_async_copy(hs_hbm.at[pl.ds(pl.multiple_of(g * tokens * n_slab, n_slab), tokens * n_slab), :],
                                     acc_scr.at[pl.ds(0, tokens * n_slab), :], sem)

    def gather(base):
        for r in range(blk):
            slab(x_scr, r * n_slab)[...] = slab(hn_ref, rows_ref[base + r])[...]

    def scatter(base):
        for j in range(blk // batch):
            rows, vals = [], []
            for u in range(batch):
                r = j * batch + u
                row0 = rows_ref[base + r]
                rows.append(row0)
                vals.append(slab(acc_scr, row0)[...] + slab(y_scr, r * n_slab)[...])
            for u in reversed(range(batch)):
                slab(acc_scr, rows[u])[...] = vals[u]

    n = cnt_ref[g * n_e + e]
    off = off_ref[g * n_e + e]

    @pl.when(e == 0)
    def _():
        acc_init().start()
        y_scr[...] = jnp.zeros_like(y_scr)
        state_scr[0] = 0
        gather(jnp.where(n > 0, off, next_ref[g * n_e + e]))
        acc_init().wait()

    n_blocks = (n + blk - 1) // blk
    nxt = next_ref[g * n_e + e]
    row_id = lax.broadcasted_iota(jnp.int32, (blk, 1), 0)
    diag_row = lax.broadcasted_iota(jnp.int32, (LANES, 2 * LANES), 0)
    diag_lane = lax.broadcasted_iota(jnp.int32, (LANES, 2 * LANES), 1)

    def block(i, carry):
        base = off + i * blk
        xb = jnp.concatenate([x_scr[pl.ds(s, blk, stride=n_slab), :] for s in range(n_slab)], axis=1).astype(BF16)
        scatter(state_scr[0])
        gather(jnp.where(i + 1 < n_blocks, base + blk, nxt))
        act = (_silu(jnp.dot(xb, wg_ref[0], preferred_element_type=F32))
               * jnp.dot(xb, wu_ref[0], preferred_element_type=F32)).astype(BF16)
        y = jnp.dot(act, wd_ref[0], preferred_element_type=F32)
        first_row = lax.shift_right_logical(base, LANES.bit_length() - 1)
        shift = base & (LANES - 1)
        cols = []
        for j in range(blk // LANES):
            pair = jnp.concatenate([gates_ref[pl.ds(first_row + j, 1), :],
                                    gates_ref[pl.ds(first_row + j + 1, 1), :]], axis=1)
            cols.append(jnp.sum(jnp.where(diag_lane == diag_row + shift, pair, 0.0), axis=1, keepdims=True))
        gate_col = cols[0] if len(cols) == 1 else jnp.concatenate(cols, axis=0)
        y = jnp.where(row_id < n - i * blk, y * gate_col, 0.0)
        for s in range(n_slab):
            y_scr[pl.ds(s, blk, stride=n_slab), :] = y[:, s * LANES:(s + 1) * LANES]
        state_scr[0] = base
        return carry

    lax.fori_loop(0, n_blocks, block, 0)

    @pl.when(e == n_e - 1)
    def _():
        scatter(state_scr[0])
        rows = 256
        for c in range(tokens // rows):
            acc = jnp.concatenate([acc_scr[pl.ds(c * rows * n_slab + s, rows, stride=n_slab), :]
                                   for s in range(n_slab)], axis=1)
            out_ref[c * rows:(c + 1) * rows, :] = _rms(acc, gf_ref[...])


def _moe(cnt_flat, off_flat, next_flat, rows_flat, gates2d, hn3, hs3, wg, wu, wd, gf):
    n_slab = gf.shape[1] // LANES
    t = hn3.shape[0] // n_slab
    d = gf.shape[1]
    n_experts, d_expert = wd.shape[0], wd.shape[1]
    tokens = GROUP_TOKENS
    n_groups = t // tokens
    list_len = tokens * TOP_K + LIST_PAD
    assert EXPERT_ROWS % LANES == 0 and EXPERT_ROWS + LANES <= LIST_PAD and list_len % (8 * LANES) == 0
    grid_spec = pltpu.PrefetchScalarGridSpec(
        num_scalar_prefetch=3,
        grid=(n_groups, n_experts),
        in_specs=[pl.BlockSpec((list_len,), lambda g, e, *_: (g,), memory_space=pltpu.SMEM),
                  pl.BlockSpec((list_len // LANES, LANES), lambda g, e, *_: (g, 0)),
                  pl.BlockSpec((tokens * n_slab, LANES), lambda g, e, *_: (g, 0)),
                  pl.BlockSpec(memory_space=pl.ANY),
                  pl.BlockSpec((1, d, d_expert), lambda g, e, *_: (e, 0, 0)),
                  pl.BlockSpec((1, d, d_expert), lambda g, e, *_: (e, 0, 0)),
                  pl.BlockSpec((1, d_expert, d), lambda g, e, *_: (e, 0, 0)),
                  pl.BlockSpec((1, d), lambda g, e, *_: (0, 0))],
        out_specs=pl.BlockSpec((tokens, d), lambda g, e, *_: (g, 0)),
        scratch_shapes=[pltpu.VMEM((tokens * n_slab, LANES), F32),
                        pltpu.VMEM((EXPERT_ROWS * n_slab, LANES), F32),
                        pltpu.VMEM((EXPERT_ROWS * n_slab, LANES), F32),
                        pltpu.SMEM((1,), jnp.int32),
                        pltpu.SemaphoreType.DMA],
    )
    return pl.pallas_call(
        functools.partial(_moe_kernel, d_expert=d_expert, n_slab=n_slab),
        grid_spec=grid_spec,
        out_shape=jax.ShapeDtypeStruct((t, d), F32),
        compiler_params=pltpu.CompilerParams(dimension_semantics=("arbitrary", "arbitrary"),
                                             vmem_limit_bytes=VMEM_LIMIT),
        name="moe",
    )(cnt_flat, off_flat, next_flat, rows_flat, gates2d, hn3, hs3, wg, wu, wd, gf)


def kernel(x, norm1_g, w_in, sgu_ln_g, sgu_ln_b, w_spatial, b_spatial, out_norm_a_g, out_norm_b_g, w_out,
           norm2_g, w_router, router_bias, w_gate, w_up, w_down, ws_gate, ws_up, ws_down, final_norm_g):
    bsz, s, d = x.shape
    depth = w_in.shape[0]
    assert depth == 1, "the fused final norm assumes a single trunk layer"
    l = 0
    t = bsz * s
    d_a = sgu_ln_g.shape[-1]
    d_b = out_norm_b_g.shape[-1]
    n_heads, chunk = w_spatial.shape[1], w_spatial.shape[2]
    n_experts = w_router.shape[-1]
    assert d_a == n_heads * A_HEAD_DIM and chunk == LANES and 2 * A_HEAD_DIM == LANES
    assert t % FRONT_ROWS == 0 and FRONT_ROWS % chunk == 0 and s % FRONT_ROWS == 0
    assert t % DISPATCH_TOKENS == 0 and t % COMBINE_TOKENS == 0
    row = lambda v: v.reshape(1, -1).astype(F32)

    x2 = x.reshape(t, d)

    win = w_in[l].astype(BF16)
    ws = w_spatial[l]
    ws_pair = jnp.concatenate([ws[0::2], ws[1::2]], axis=2).astype(BF16)
    bs_full = jnp.repeat(b_spatial[l].T, A_HEAD_DIM, axis=1).astype(F32)
    woa = w_out[l][:d_a].astype(BF16)
    wob = w_out[l][d_a:].astype(BF16)
    wrt = w_router[l].T.astype(F32)
    rb = jnp.broadcast_to(router_bias[l].astype(F32)[:, None], (n_experts, 1))
    wsgu = jnp.concatenate([ws_gate[l], ws_up[l]], axis=1).astype(BF16)
    wsd = ws_down[l].astype(BF16)
    wg = w_gate[l].astype(BF16)
    wu = w_up[l].astype(BF16)
    wd = w_down[l].astype(BF16)

    an, zb = _front(x2, row(norm1_g[l]), win, row(sgu_ln_g[l]), row(sgu_ln_b[l]), ws_pair, bs_full,
                    row(out_norm_a_g[l]), d_a=d_a, d_b=d_b, chunk=chunk)
    bn = _fnet(zb, row(out_norm_b_g[l]), bsz).reshape(t, d_b)

    hs3, hn3, eidx, gate, cnt = _post(x2, an, bn, woa, wob, row(norm2_g[l]), wrt, rb, wsgu, wsd)

    n_groups = t // GROUP_TOKENS
    n_assign = GROUP_TOKENS * TOP_K
    counts = cnt[:, :, 0]
    offs = jnp.cumsum(counts, axis=1) - counts
    tok_bits = (n_assign - 1).bit_length()
    local = jnp.arange(n_assign, dtype=jnp.int32).reshape(1, GROUP_TOKENS, TOP_K)
    eidx_g = eidx.reshape(TOP_K, n_groups, GROUP_TOKENS).transpose(1, 2, 0)
    gate_g = gate.reshape(TOP_K, n_groups, GROUP_TOKENS).transpose(1, 2, 0).reshape(n_groups, n_assign)
    keys = ((eidx_g << tok_bits) | local).reshape(n_groups, n_assign)
    keys_sorted, gates_sorted = lax.sort((keys, gate_g), dimension=1, num_keys=1)
    rows_sorted = (keys_sorted & (n_assign - 1)) // TOP_K * (d // LANES)
    fill = ((0, 0), (0, LIST_PAD))
    rows_flat = jnp.pad(rows_sorted, fill).reshape(-1)
    gates2d = jnp.pad(gates_sorted, fill).reshape(-1, LANES)
    e_row = jnp.arange(n_experts, dtype=jnp.int32)
    later = (e_row[None, None, :] > e_row[None, :, None]) & (counts[:, None, :] > 0)
    first_later = jnp.min(jnp.where(later, e_row[None, None, :], n_experts), axis=2)
    next_off = jnp.sum(jnp.where(first_later[:, :, None] == e_row[None, None, :], offs[:, None, :], 0), axis=2)
    flat = lambda a: a.reshape(-1).astype(jnp.int32)

    out = _moe(flat(counts), flat(offs), flat(next_off), rows_flat, gates2d,
               hn3, hs3, wg, wu, wd, row(final_norm_g))
    return out.reshape(bsz, s, d)
```

```python
import functools
import math

import numpy as np
import jax
import jax.numpy as jnp
from jax import lax
from jax.experimental import pallas as pl
from jax.experimental.pallas import tpu as pltpu

A_HEAD_DIM = 64
B_GROUP_DIM = 64
TOP_K = 8
N_EXPERT_GROUPS = 8
TOPK_GROUPS = 4
ROUTED_SCALE = 2.5
EPS = 1e-6

LANES = 128
FRONT_ROWS = 1024
EXPERT_ROWS = 128
LIST_PAD = 1024
GROUP_TOKENS = 2048
FFT_RADIX = 64
FFT_PAD = 8
VMEM_LIMIT = 56 * 1024 * 1024

F32 = jnp.float32
BF16 = jnp.bfloat16


def _rms(x, g):
    return x * lax.rsqrt(jnp.mean(x * x, axis=-1, keepdims=True) + EPS) * g


def _gelu(x):
    return 0.5 * x * (1.0 + lax.erf(x * (1.0 / math.sqrt(2.0))))


def _silu(x):
    return x / (1.0 + jnp.exp(-x))


def _front_kernel(x_ref, g1_ref, win_ref, lng_ref, lnb_ref, ws_ref, bs_ref, ga_ref,
                  an_ref, zb_ref, ya_scr, *, d_a, chunk):
    x = x_ref[...]
    xn = _rms(x, g1_ref[...]).astype(BF16)
    z = jnp.dot(xn, win_ref[...], preferred_element_type=F32)
    for q in range(zb_ref.shape[0]):
        zb_ref[q] = z[:, 2 * d_a + q * LANES:2 * d_a + (q + 1) * LANES].astype(BF16)
    u = _gelu(z[:, :d_a])
    v = _gelu(z[:, d_a:2 * d_a])
    mu = jnp.mean(v, axis=-1, keepdims=True)
    vc = v - mu
    v = vc * lax.rsqrt(jnp.mean(vc * vc, axis=-1, keepdims=True) + EPS) * lng_ref[...] + lnb_ref[...]
    rows = x.shape[0]
    lane = lax.broadcasted_iota(jnp.int32, (chunk, LANES), 1)
    low = lane < A_HEAD_DIM
    for c in range(rows // chunk):
        r0 = c * chunk
        for hp in range(d_a // LANES):
            c0 = hp * LANES
            vp = v[r0:r0 + chunk, c0:c0 + LANES]
            rhs = jnp.concatenate([jnp.where(low, vp, 0.0), jnp.where(low, 0.0, vp)], axis=0).astype(BF16)
            mixed = jnp.dot(ws_ref[hp], rhs, preferred_element_type=F32) + bs_ref[:, c0:c0 + LANES]
            ya_scr[r0:r0 + chunk, c0:c0 + LANES] = u[r0:r0 + chunk, c0:c0 + LANES] * mixed
    an_ref[...] = _rms(ya_scr[...], ga_ref[...]).astype(BF16)


def _front(x2, g1, win, lng, lnb, ws_pair, bs_full, ga, *, d_a, d_b, chunk):
    t, d = x2.shape
    d_in = win.shape[1]
    rows = FRONT_ROWS
    const = lambda shape: pl.BlockSpec(shape, lambda i: (0,) * len(shape))
    return pl.pallas_call(
        functools.partial(_front_kernel, d_a=d_a, chunk=chunk),
        grid=(t // rows,),
        in_specs=[
            pl.BlockSpec((rows, d), lambda i: (i, 0)),
            const((1, d)), const((d, d_in)), const((1, d_a)), const((1, d_a)),
            const(ws_pair.shape), const(bs_full.shape), const((1, d_a)),
        ],
        out_specs=[pl.BlockSpec((rows, d_a), lambda i: (i, 0)),
                   pl.BlockSpec((d_b // LANES, rows, LANES), lambda i: (0, i, 0))],
        out_shape=[jax.ShapeDtypeStruct((t, d_a), BF16), jax.ShapeDtypeStruct((d_b // LANES, t, LANES), BF16)],
        scratch_shapes=[pltpu.VMEM((rows, d_a), F32)],
        compiler_params=pltpu.CompilerParams(dimension_semantics=("parallel",),
                                             vmem_limit_bytes=VMEM_LIMIT),
        name="front",
    )(x2, g1, win, lng, lnb, ws_pair, bs_full, ga)


def _fnet_tables(seq, half_w):
    r = FFT_RADIX
    assert seq == r * r
    j = np.arange(B_GROUP_DIM)
    ang = 2.0 * np.pi * np.outer(j, j) / B_GROUP_DIM
    ng = half_w // B_GROUP_DIM
    eye = np.eye(ng)
    m_ch = np.concatenate([np.kron(eye, np.cos(ang)), np.kron(eye, np.sin(ang))], axis=1)
    k1 = np.arange(r)[:, None]
    n1 = np.arange(r)[None, :]
    d_tw = np.zeros((r, 2 * r, 2 * r))
    for n2 in range(r):
        th = 2.0 * np.pi * (k1 * n1 / r + k1 * n2 / (r * r))
        c, s = np.cos(th), np.sin(th)
        d_tw[n2] = np.block([[c, -s], [s, c]])
    ph = 2.0 * np.pi * np.outer(np.arange(r), np.arange(r)) / r
    scale = 1.0 / math.sqrt(seq * B_GROUP_DIM)
    d2 = np.concatenate([np.cos(ph), -np.sin(ph)], axis=1) * scale
    return (jnp.asarray(m_ch, F32), jnp.asarray(d_tw, F32), jnp.asarray(d2, F32))


def _fnet_kernel(z_ref, mch32_ref, dtw_ref, d232_ref, gb_ref, out_ref, w_scr, a2_scr, y_scr, mch_ref, d2_ref,
                 *, n_col, unroll):
    r = FFT_RADIX
    rp = r + FFT_PAD
    ap = 2 * r + FFT_PAD
    hf = pl.program_id(1)
    half_w = n_col * LANES
    mch_ref[...] = mch32_ref[...].astype(BF16)
    d2_ref[...] = d232_ref[...].astype(BF16)

    chunk = 8 * r
    for c in range(r * r // chunk):
        zc = jnp.concatenate([z_ref[q, c * chunk:(c + 1) * chunk, :] for q in range(n_col)], axis=1)
        w = jnp.dot(zc, mch_ref[...], preferred_element_type=F32)
        for j in range(chunk // r):
            n1 = c * (chunk // r) + j
            for p in range(2 * n_col):
                w_scr[p, n1 * rp:n1 * rp + r, :] = w[j * r:(j + 1) * r, p * LANES:(p + 1) * LANES]

    def stage1(it, carry):
        for u in range(unroll):
            n2 = it * unroll + u
            rhs = jnp.concatenate(
                [jnp.concatenate([w_scr[ri * n_col + q, pl.ds(n2, r, stride=rp), :] for q in range(n_col)], axis=1)
                 for ri in range(2)], axis=0).astype(BF16)
            a = jnp.dot(dtw_ref[n2].astype(BF16), rhs, preferred_element_type=F32)
            for q in range(n_col):
                a2_scr[q, pl.ds(n2, r, stride=ap), :] = a[:r, q * LANES:(q + 1) * LANES]
                a2_scr[q, pl.ds(r + n2, r, stride=ap), :] = a[r:, q * LANES:(q + 1) * LANES]
        return carry

    lax.fori_loop(0, r // unroll, stage1, 0)

    def stage2(it, carry):
        for u in range(unroll):
            k1 = it * unroll + u
            row0 = pl.multiple_of(k1 * ap, 8)
            blk = jnp.concatenate([a2_scr[q, pl.ds(row0, 2 * r), :] for q in range(n_col)],
                                  axis=1).astype(BF16)
            y = jnp.dot(d2_ref[...], blk, preferred_element_type=F32)
            for q in range(n_col):
                y_scr[hf * n_col + q, pl.ds(k1, r, stride=rp), :] = y[:, q * LANES:(q + 1) * LANES]
        return carry

    lax.fori_loop(0, r // unroll, stage2, 0)

    @pl.when(hf == pl.num_programs(1) - 1)
    def _():
        n_all = y_scr.shape[0]
        for k2 in range(r):
            ys = [y_scr[q, k2 * rp:k2 * rp + r, :] for q in range(n_all)]
            ss = sum(jnp.sum(y * y, axis=-1, keepdims=True) for y in ys)
            inv = lax.rsqrt(ss * (1.0 / (n_all * LANES)) + EPS)
            for q in range(n_all):
                out_ref[0, k2 * r:(k2 + 1) * r, q * LANES:(q + 1) * LANES] = (
                    ys[q] * inv * gb_ref[:, q * LANES:(q + 1) * LANES]).astype(BF16)


def _fnet(zb_cols, gb, bsz):
    n_all, t, _ = zb_cols.shape
    s = t // bsz
    d_b = n_all * LANES
    n_col = 2
    r = FFT_RADIX
    rp, ap = r + FFT_PAD, 2 * r + FFT_PAD
    m_ch, d_tw, d2 = _fnet_tables(s, n_col * LANES)
    return pl.pallas_call(
        functools.partial(_fnet_kernel, n_col=n_col, unroll=8),
        grid=(bsz, n_all // n_col),
        in_specs=[
            pl.BlockSpec((n_col, s, LANES), lambda i, h: (h, i, 0)),
            pl.BlockSpec(m_ch.shape, lambda i, h: (0, 0)),
            pl.BlockSpec(d_tw.shape, lambda i, h: (0, 0, 0)),
            pl.BlockSpec(d2.shape, lambda i, h: (0, 0)),
            pl.BlockSpec((1, d_b), lambda i, h: (0, 0)),
        ],
        out_specs=pl.BlockSpec((1, s, d_b), lambda i, h: (i, 0, 0)),
        out_shape=jax.ShapeDtypeStruct((bsz, s, d_b), BF16),
        scratch_shapes=[pltpu.VMEM((2 * n_col, r * rp, LANES), F32), pltpu.VMEM((n_col, r * ap, LANES), F32),
                        pltpu.VMEM((n_all, r * rp, LANES), F32),
                        pltpu.VMEM(m_ch.shape, BF16), pltpu.VMEM(d2.shape, BF16)],
        compiler_params=pltpu.CompilerParams(dimension_semantics=("parallel", "arbitrary"),
                                             vmem_limit_bytes=VMEM_LIMIT),
        name="fnet",
    )(zb_cols, m_ch, d_tw, d2, gb)


def _post_kernel(x_ref, an_ref, bn_ref, woa_ref, wob_ref, g2_ref, wrt_ref, rb_ref, wsgu_ref, wsd_ref,
                 hs_ref, hn_ref, eidx_ref, gate_ref, cnt_ref, carry_scr,
                 *, n_experts, d_shared, steps_per_group):
    i = pl.program_id(0)

    @pl.when(i % steps_per_group == 0)
    def _():
        carry_scr[...] = jnp.zeros_like(carry_scr)

    h = (x_ref[...]
         + jnp.dot(an_ref[...], woa_ref[...], preferred_element_type=F32)
         + jnp.dot(bn_ref[...], wob_ref[...], preferred_element_type=F32))
    hn = _rms(h, g2_ref[...])
    hnb = hn.astype(BF16)
    gu = jnp.dot(hnb, wsgu_ref[...], preferred_element_type=F32)
    act = (_silu(gu[:, :d_shared]) * gu[:, d_shared:]).astype(BF16)
    hs = h + jnp.dot(act, wsd_ref[...], preferred_element_type=F32)
    n_slab = hn.shape[1] // LANES
    for s in range(n_slab):
        hn_ref[pl.ds(s, hn.shape[0], stride=n_slab), :] = hn[:, s * LANES:(s + 1) * LANES]
        hs_ref[pl.ds(s, hn.shape[0], stride=n_slab), :] = hs[:, s * LANES:(s + 1) * LANES]

    rows = hn.shape[0]
    eg = n_experts // N_EXPERT_GROUPS
    logits_t = lax.dot_general(wrt_ref[...], hn, (((1,), (1,)), ((), ())),
                               precision=lax.Precision.HIGHEST, preferred_element_type=F32)
    scores_t = 1.0 / (1.0 + jnp.exp(-logits_t))
    biased_t = scores_t + rb_ref[...]
    sub = lax.broadcasted_iota(jnp.int32, (eg, rows), 0)
    neg = -jnp.inf
    grp = [biased_t[g * eg:(g + 1) * eg, :] for g in range(N_EXPERT_GROUPS)]
    sco = [scores_t[g * eg:(g + 1) * eg, :] for g in range(N_EXPERT_GROUPS)]
    gid = [sub + g * eg for g in range(N_EXPERT_GROUPS)]

    gscore = []
    for g in range(N_EXPERT_GROUPS):
        m1 = jnp.max(grp[g], axis=0, keepdims=True)
        first = jnp.min(jnp.where(grp[g] == m1, sub, eg), axis=0, keepdims=True)
        m2 = jnp.max(jnp.where(sub == first, neg, grp[g]), axis=0, keepdims=True)
        gscore.append(m1 + m2)
    masked = []
    for g in range(N_EXPERT_GROUPS):
        beaten = jnp.zeros((1, rows), jnp.int32)
        for j in range(N_EXPERT_GROUPS):
            if j == g:
                continue
            wins = (gscore[j] > gscore[g]) if j > g else (gscore[j] >= gscore[g])
            beaten = beaten + wins.astype(jnp.int32)
        masked.append(jnp.where(beaten < TOPK_GROUPS, grp[g], neg))

    sel = [jnp.zeros((eg, rows), F32) for _ in range(N_EXPERT_GROUPS)]
    e_rows, w_rows = [], []
    for _ in range(TOP_K):
        m = masked[0]
        for g in range(1, N_EXPERT_GROUPS):
            m = jnp.maximum(m, masked[g])
        mk = jnp.max(m, axis=0, keepdims=True)
        cand = jnp.where(masked[0] == mk, gid[0], n_experts)
        for g in range(1, N_EXPERT_GROUPS):
            cand = jnp.minimum(cand, jnp.where(masked[g] == mk, gid[g], n_experts))
        ik = jnp.min(cand, axis=0, keepdims=True)
        wk = jnp.zeros((eg, rows), F32)
        for g in range(N_EXPERT_GROUPS):
            hit = gid[g] == ik
            masked[g] = jnp.where(hit, neg, masked[g])
            sel[g] = jnp.where(hit, 1.0, sel[g])
            wk = wk + jnp.where(hit, sco[g], 0.0)
        e_rows.append(ik)
        w_rows.append(jnp.sum(wk, axis=0, keepdims=True))
    wsum = w_rows[0]
    for k in range(1, TOP_K):
        wsum = wsum + w_rows[k]
    gates = [w_rows[k] / wsum * ROUTED_SCALE for k in range(TOP_K)]

    sel_t = jnp.concatenate(sel, axis=0)
    new_carry = carry_scr[...] + jnp.sum(sel_t, axis=1, keepdims=True)
    carry_scr[...] = new_carry
    cnt_ref[0] = new_carry.astype(jnp.int32)
    eidx_ref[...] = jnp.concatenate(e_rows, axis=0)
    gate_ref[...] = jnp.concatenate(gates, axis=0)


def _post(x2, an, bn, woa, wob, g2, wrt, rb, wsgu, wsd):
    t, d = x2.shape
    rows = FRONT_ROWS
    n_experts = wrt.shape[0]
    d_shared = wsd.shape[0]
    n_slab = d // LANES
    steps_per_group = GROUP_TOKENS // rows
    n_groups = t // GROUP_TOKENS
    const = lambda shape: pl.BlockSpec(shape, lambda i: (0,) * len(shape))
    row_blk = lambda w: pl.BlockSpec((rows, w), lambda i: (i, 0))
    slab_blk = pl.BlockSpec((rows * n_slab, LANES), lambda i: (i, 0))
    tok_blk = pl.BlockSpec((TOP_K, rows), lambda i: (0, i))
    return pl.pallas_call(
        functools.partial(_post_kernel, n_experts=n_experts, d_shared=d_shared,
                          steps_per_group=steps_per_group),
        grid=(t // rows,),
        in_specs=[row_blk(d), row_blk(an.shape[1]), row_blk(bn.shape[1]),
                  const(woa.shape), const(wob.shape), const((1, d)), const(wrt.shape), const(rb.shape),
                  const(wsgu.shape), const(wsd.shape)],
        out_specs=[slab_blk, slab_blk, tok_blk, tok_blk,
                   pl.BlockSpec((1, n_experts, LANES), lambda i: (i // steps_per_group, 0, 0))],
        out_shape=[jax.ShapeDtypeStruct((t * n_slab, LANES), F32), jax.ShapeDtypeStruct((t * n_slab, LANES), F32),
                   jax.ShapeDtypeStruct((TOP_K, t), jnp.int32), jax.ShapeDtypeStruct((TOP_K, t), F32),
                   jax.ShapeDtypeStruct((n_groups, n_experts, LANES), jnp.int32)],
        scratch_shapes=[pltpu.VMEM((n_experts, LANES), F32)],
        compiler_params=pltpu.CompilerParams(dimension_semantics=("arbitrary",),
                                             vmem_limit_bytes=VMEM_LIMIT),
        name="post",
    )(x2, an, bn, woa, wob, g2, wrt, rb, wsgu, wsd)


def _moe_kernel(cnt_ref, off_ref, next_ref, rows_ref, gates_ref, hn_ref, hs_hbm, wg_ref, wu_ref, wd_ref, gf_ref,
                out_ref, acc_scr, x_scr, y_scr, state_scr, sem, *, n_slab):
    g = pl.program_id(0)
    e = pl.program_id(1)
    n_e = pl.num_programs(1)
    tokens = GROUP_TOKENS
    blk = EXPERT_ROWS
    batch = 16

    def slab(ref, row0):
        return ref.at[pl.ds(pl.multiple_of(row0, n_slab), n_slab), :]

    def acc_init(group):
        return pltpu.make_async_copy(
            hs_hbm.at[pl.ds(pl.multiple_of(group * tokens * n_slab, n_slab), tokens * n_slab), :],
            acc_scr.at[pl.ds(0, tokens * n_slab), :], sem)

    def gather(base):
        for r in range(blk):
            slab(x_scr, r * n_slab)[...] = slab(hn_ref, rows_ref[base + r])[...]

    def scatter(base):
        for j in range(blk // batch):
            rows, vals = [], []
            for u in range(batch):
                r = j * batch + u
                row0 = rows_ref[base + r]
                rows.append(row0)
                vals.append(slab(acc_scr, row0)[...] + slab(y_scr, r * n_slab)[...])
            for u in reversed(range(batch)):
                slab(acc_scr, rows[u])[...] = vals[u]

    n = cnt_ref[g * n_e + e]
    off = off_ref[g * n_e + e]

    @pl.when(e == 0)
    def _():
        @pl.when(g == 0)
        def _():
            acc_init(g).start()

        y_scr[...] = jnp.zeros_like(y_scr)
        state_scr[0] = 0
        gather(jnp.where(n > 0, off, next_ref[g * n_e + e]))
        acc_init(g).wait()

    n_blocks = (n + blk - 1) // blk
    nxt = next_ref[g * n_e + e]
    row_id = lax.broadcasted_iota(jnp.int32, (blk, 1), 0)
    diag_row = lax.broadcasted_iota(jnp.int32, (LANES, 2 * LANES), 0)
    diag_lane = lax.broadcasted_iota(jnp.int32, (LANES, 2 * LANES), 1)

    def block(i, carry):
        base = off + i * blk
        xb = jnp.concatenate([x_scr[pl.ds(s, blk, stride=n_slab), :] for s in range(n_slab)], axis=1).astype(BF16)
        scatter(state_scr[0])
        gather(jnp.where(i + 1 < n_blocks, base + blk, nxt))
        act = (_silu(jnp.dot(xb, wg_ref[0], preferred_element_type=F32))
               * jnp.dot(xb, wu_ref[0], preferred_element_type=F32)).astype(BF16)
        y = jnp.dot(act, wd_ref[0], preferred_element_type=F32)
        first_row = lax.shift_right_logical(base, LANES.bit_length() - 1)
        shift = base & (LANES - 1)
        cols = []
        for j in range(blk // LANES):
            pair = jnp.concatenate([gates_ref[pl.ds(first_row + j, 1), :],
                                    gates_ref[pl.ds(first_row + j + 1, 1), :]], axis=1)
            cols.append(jnp.sum(jnp.where(diag_lane == diag_row + shift, pair, 0.0), axis=1, keepdims=True))
        gate_col = cols[0] if len(cols) == 1 else jnp.concatenate(cols, axis=0)
        y = jnp.where(row_id < n - i * blk, y * gate_col, 0.0)
        for s in range(n_slab):
            y_scr[pl.ds(s, blk, stride=n_slab), :] = y[:, s * LANES:(s + 1) * LANES]
        state_scr[0] = base
        return carry

    lax.fori_loop(0, n_blocks, block, 0)

    @pl.when(e == n_e - 1)
    def _():
        scatter(state_scr[0])
        rows = 256
        for c in range(tokens // rows):
            acc = jnp.concatenate([acc_scr[pl.ds(c * rows * n_slab + s, rows, stride=n_slab), :]
                                   for s in range(n_slab)], axis=1)
            out_ref[c * rows:(c + 1) * rows, :] = _rms(acc, gf_ref[...])

        @pl.when(g + 1 < pl.num_programs(0))
        def _():
            acc_init(g + 1).start()


def _moe(cnt_flat, off_flat, next_flat, rows_flat, gates2d, hn3, hs3, wg, wu, wd, gf):
    n_slab = gf.shape[1] // LANES
    t = hn3.shape[0] // n_slab
    d = gf.shape[1]
    n_experts, d_expert = wd.shape[0], wd.shape[1]
    tokens = GROUP_TOKENS
    n_groups = t // tokens
    list_len = tokens * TOP_K + LIST_PAD
    assert EXPERT_ROWS % LANES == 0 and EXPERT_ROWS + LANES <= LIST_PAD and list_len % (8 * LANES) == 0
    grid_spec = pltpu.PrefetchScalarGridSpec(
        num_scalar_prefetch=3,
        grid=(n_groups, n_experts),
        in_specs=[pl.BlockSpec((list_len,), lambda g, e, *_: (g,), memory_space=pltpu.SMEM),
                  pl.BlockSpec((list_len // LANES, LANES), lambda g, e, *_: (g, 0)),
                  pl.BlockSpec((tokens * n_slab, LANES), lambda g, e, *_: (g, 0)),
                  pl.BlockSpec(memory_space=pl.ANY),
                  pl.BlockSpec((1, d, d_expert), lambda g, e, *_: (e, 0, 0)),
                  pl.BlockSpec((1, d, d_expert), lambda g, e, *_: (e, 0, 0)),
                  pl.BlockSpec((1, d_expert, d), lambda g, e, *_: (e, 0, 0)),
                  pl.BlockSpec((1, d), lambda g, e, *_: (0, 0))],
        out_specs=pl.BlockSpec((tokens, d), lambda g, e, *_: (g, 0)),
        scratch_shapes=[pltpu.VMEM((tokens * n_slab, LANES), F32),
                        pltpu.VMEM((EXPERT_ROWS * n_slab, LANES), F32),
                        pltpu.VMEM((EXPERT_ROWS * n_slab, LANES), F32),
                        pltpu.SMEM((1,), jnp.int32),
                        pltpu.SemaphoreType.DMA],
    )
    return pl.pallas_call(
        functools.partial(_moe_kernel, n_slab=n_slab),
        grid_spec=grid_spec,
        out_shape=jax.ShapeDtypeStruct((t, d), F32),
        compiler_params=pltpu.CompilerParams(dimension_semantics=("arbitrary", "arbitrary"),
                                             vmem_limit_bytes=VMEM_LIMIT),
        name="moe",
    )(cnt_flat, off_flat, next_flat, rows_flat, gates2d, hn3, hs3, wg, wu, wd, gf)


def kernel(x, norm1_g, w_in, sgu_ln_g, sgu_ln_b, w_spatial, b_spatial, out_norm_a_g, out_norm_b_g, w_out,
           norm2_g, w_router, router_bias, w_gate, w_up, w_down, ws_gate, ws_up, ws_down, final_norm_g):
    bsz, s, d = x.shape
    depth = w_in.shape[0]
    assert depth == 1, "the fused final norm assumes a single trunk layer"
    l = 0
    t = bsz * s
    d_a = sgu_ln_g.shape[-1]
    d_b = out_norm_b_g.shape[-1]
    n_heads, chunk = w_spatial.shape[1], w_spatial.shape[2]
    n_experts = w_router.shape[-1]
    assert d_a == n_heads * A_HEAD_DIM and chunk == LANES and 2 * A_HEAD_DIM == LANES
    assert t % FRONT_ROWS == 0 and FRONT_ROWS % chunk == 0 and s % FRONT_ROWS == 0
    assert t % GROUP_TOKENS == 0 and GROUP_TOKENS % FRONT_ROWS == 0
    assert (GROUP_TOKENS * TOP_K) & (GROUP_TOKENS * TOP_K - 1) == 0, "sort keys pack the assignment index in low bits"
    row = lambda v: v.reshape(1, -1).astype(F32)

    x2 = x.reshape(t, d)

    win = w_in[l].astype(BF16)
    ws = w_spatial[l]
    ws_pair = jnp.concatenate([ws[0::2], ws[1::2]], axis=2).astype(BF16)
    bs_full = jnp.repeat(b_spatial[l].T, A_HEAD_DIM, axis=1).astype(F32)
    woa = w_out[l][:d_a].astype(BF16)
    wob = w_out[l][d_a:].astype(BF16)
    wrt = w_router[l].T.astype(F32)
    rb = jnp.broadcast_to(router_bias[l].astype(F32)[:, None], (n_experts, 1))
    wsgu = jnp.concatenate([ws_gate[l], ws_up[l]], axis=1).astype(BF16)
    wsd = ws_down[l].astype(BF16)
    wg = w_gate[l].astype(BF16)
    wu = w_up[l].astype(BF16)
    wd = w_down[l].astype(BF16)

    an, zb = _front(x2, row(norm1_g[l]), win, row(sgu_ln_g[l]), row(sgu_ln_b[l]), ws_pair, bs_full,
                    row(out_norm_a_g[l]), d_a=d_a, d_b=d_b, chunk=chunk)
    bn = _fnet(zb, row(out_norm_b_g[l]), bsz).reshape(t, d_b)

    hs3, hn3, eidx, gate, cnt = _post(x2, an, bn, woa, wob, row(norm2_g[l]), wrt, rb, wsgu, wsd)

    n_groups = t // GROUP_TOKENS
    n_assign = GROUP_TOKENS * TOP_K
    counts = cnt[:, :, 0]
    offs = jnp.cumsum(counts, axis=1) - counts
    tok_bits = (n_assign - 1).bit_length()
    local = jnp.arange(n_assign, dtype=jnp.int32).reshape(1, GROUP_TOKENS, TOP_K)
    eidx_g = eidx.reshape(TOP_K, n_groups, GROUP_TOKENS).transpose(1, 2, 0)
    gate_g = gate.reshape(TOP_K, n_groups, GROUP_TOKENS).transpose(1, 2, 0).reshape(n_groups, n_assign)
    keys = ((eidx_g << tok_bits) | local).reshape(n_groups, n_assign)
    keys_sorted, gates_sorted = lax.sort((keys, gate_g), dimension=1, num_keys=1)
    rows_sorted = (keys_sorted & (n_assign - 1)) // TOP_K * (d // LANES)
    fill = ((0, 0), (0, LIST_PAD))
    rows_flat = jnp.pad(rows_sorted, fill).reshape(-1)
    gates2d = jnp.pad(gates_sorted, fill).reshape(-1, LANES)
    e_row = jnp.arange(n_experts, dtype=jnp.int32)
    later = (e_row[None, None, :] > e_row[None, :, None]) & (counts[:, None, :] > 0)
    first_later = jnp.min(jnp.where(later, e_row[None, None, :], n_experts), axis=2)
    next_off = jnp.sum(jnp.where(first_later[:, :, None] == e_row[None, None, :], offs[:, None, :], 0), axis=2)
    flat = lambda a: a.reshape(-1).astype(jnp.int32)

    out = _moe(flat(counts), flat(offs), flat(next_off), rows_flat, gates2d,
               hn3, hs3, wg, wu, wd, row(final_norm_g))
    return out.reshape(bsz, s, d)
```

```python
import functools
import math

import numpy as np
import jax
import jax.numpy as jnp
from jax import lax
from jax.experimental import pallas as pl
from jax.experimental.pallas import tpu as pltpu

A_HEAD_DIM = 64
B_GROUP_DIM = 64
TOP_K = 8
N_EXPERT_GROUPS = 8
TOPK_GROUPS = 4
ROUTED_SCALE = 2.5
EPS = 1e-6

LANES = 128
FRONT_ROWS = 1024
EXPERT_ROWS = 128
LIST_PAD = 1024
GROUP_TOKENS = 2048
FFT_RADIX = 64
FFT_PAD = 8
VMEM_LIMIT = 56 * 1024 * 1024

F32 = jnp.float32
BF16 = jnp.bfloat16


def _rms(x, g):
    return x * lax.rsqrt(jnp.mean(x * x, axis=-1, keepdims=True) + EPS) * g


def _gelu(x):
    return 0.5 * x * (1.0 + lax.erf(x * (1.0 / math.sqrt(2.0))))


def _silu(x):
    return x / (1.0 + jnp.exp(-x))


def _front_kernel(x_ref, g1_ref, win_ref, lng_ref, lnb_ref, ws_ref, bs_ref, ga_ref,
                  an_ref, zb_ref, ya_scr, *, d_a, chunk):
    x = x_ref[...]
    xn = _rms(x, g1_ref[...]).astype(BF16)
    z = jnp.dot(xn, win_ref[...], preferred_element_type=F32)
    for q in range(zb_ref.shape[0]):
        zb_ref[q] = z[:, 2 * d_a + q * LANES:2 * d_a + (q + 1) * LANES].astype(BF16)
    u = _gelu(z[:, :d_a])
    v = _gelu(z[:, d_a:2 * d_a])
    mu = jnp.mean(v, axis=-1, keepdims=True)
    vc = v - mu
    v = vc * lax.rsqrt(jnp.mean(vc * vc, axis=-1, keepdims=True) + EPS) * lng_ref[...] + lnb_ref[...]
    rows = x.shape[0]
    lane = lax.broadcasted_iota(jnp.int32, (chunk, LANES), 1)
    low = lane < A_HEAD_DIM
    for c in range(rows // chunk):
        r0 = c * chunk
        for hp in range(d_a // LANES):
            c0 = hp * LANES
            vp = v[r0:r0 + chunk, c0:c0 + LANES]
            rhs = jnp.concatenate([jnp.where(low, vp, 0.0), jnp.where(low, 0.0, vp)], axis=0).astype(BF16)
            mixed = jnp.dot(ws_ref[hp], rhs, preferred_element_type=F32) + bs_ref[:, c0:c0 + LANES]
            ya_scr[r0:r0 + chunk, c0:c0 + LANES] = u[r0:r0 + chunk, c0:c0 + LANES] * mixed
    an_ref[...] = _rms(ya_scr[...], ga_ref[...]).astype(BF16)


def _front(x2, g1, win, lng, lnb, ws_pair, bs_full, ga, *, d_a, d_b, chunk):
    t, d = x2.shape
    d_in = win.shape[1]
    rows = FRONT_ROWS
    const = lambda shape: pl.BlockSpec(shape, lambda i: (0,) * len(shape))
    return pl.pallas_call(
        functools.partial(_front_kernel, d_a=d_a, chunk=chunk),
        grid=(t // rows,),
        in_specs=[
            pl.BlockSpec((rows, d), lambda i: (i, 0)),
            const((1, d)), const((d, d_in)), const((1, d_a)), const((1, d_a)),
            const(ws_pair.shape), const(bs_full.shape), const((1, d_a)),
        ],
        out_specs=[pl.BlockSpec((rows, d_a), lambda i: (i, 0)),
                   pl.BlockSpec((d_b // LANES, rows, LANES), lambda i: (0, i, 0))],
        out_shape=[jax.ShapeDtypeStruct((t, d_a), BF16), jax.ShapeDtypeStruct((d_b // LANES, t, LANES), BF16)],
        scratch_shapes=[pltpu.VMEM((rows, d_a), F32)],
        compiler_params=pltpu.CompilerParams(dimension_semantics=("parallel",),
                                             vmem_limit_bytes=VMEM_LIMIT),
        name="front",
    )(x2, g1, win, lng, lnb, ws_pair, bs_full, ga)


def _fnet_tables(seq, half_w):
    r = FFT_RADIX
    assert seq == r * r
    j = np.arange(B_GROUP_DIM)
    ang = 2.0 * np.pi * np.outer(j, j) / B_GROUP_DIM
    ng = half_w // B_GROUP_DIM
    eye = np.eye(ng)
    m_ch = np.concatenate([np.kron(eye, np.cos(ang)), np.kron(eye, np.sin(ang))], axis=1)
    k1 = np.arange(r)[:, None]
    n1 = np.arange(r)[None, :]
    d_tw = np.zeros((r, 2 * r, 2 * r))
    for n2 in range(r):
        th = 2.0 * np.pi * (k1 * n1 / r + k1 * n2 / (r * r))
        c, s = np.cos(th), np.sin(th)
        d_tw[n2] = np.block([[c, -s], [s, c]])
    ph = 2.0 * np.pi * np.outer(np.arange(r), np.arange(r)) / r
    scale = 1.0 / math.sqrt(seq * B_GROUP_DIM)
    d2 = np.concatenate([np.cos(ph), -np.sin(ph)], axis=1) * scale
    return (jnp.asarray(m_ch, F32), jnp.asarray(d_tw, F32), jnp.asarray(d2, F32))


def _fnet_kernel(z_ref, mch32_ref, dtw_ref, d232_ref, gb_ref, out_ref, w_scr, a2_scr, y_scr, mch_ref, d2_ref,
                 *, n_col, unroll):
    r = FFT_RADIX
    rp = r + FFT_PAD
    ap = 2 * r + FFT_PAD
    hf = pl.program_id(1)
    half_w = n_col * LANES
    mch_ref[...] = mch32_ref[...].astype(BF16)
    d2_ref[...] = d232_ref[...].astype(BF16)

    chunk = 8 * r
    for c in range(r * r // chunk):
        zc = jnp.concatenate([z_ref[q, c * chunk:(c + 1) * chunk, :] for q in range(n_col)], axis=1)
        w = jnp.dot(zc, mch_ref[...], preferred_element_type=F32)
        for j in range(chunk // r):
            n1 = c * (chunk // r) + j
            for p in range(2 * n_col):
                w_scr[p, n1 * rp:n1 * rp + r, :] = w[j * r:(j + 1) * r, p * LANES:(p + 1) * LANES]

    def stage1(it, carry):
        for u in range(unroll):
            n2 = it * unroll + u
            rhs = jnp.concatenate(
                [jnp.concatenate([w_scr[ri * n_col + q, pl.ds(n2, r, stride=rp), :] for q in range(n_col)], axis=1)
                 for ri in range(2)], axis=0).astype(BF16)
            a = jnp.dot(dtw_ref[n2].astype(BF16), rhs, preferred_element_type=F32)
            for q in range(n_col):
                a2_scr[q, pl.ds(n2, r, stride=ap), :] = a[:r, q * LANES:(q + 1) * LANES]
                a2_scr[q, pl.ds(r + n2, r, stride=ap), :] = a[r:, q * LANES:(q + 1) * LANES]
        return carry

    lax.fori_loop(0, r // unroll, stage1, 0)

    def stage2(it, carry):
        for u in range(unroll):
            k1 = it * unroll + u
            row0 = pl.multiple_of(k1 * ap, 8)
            blk = jnp.concatenate([a2_scr[q, pl.ds(row0, 2 * r), :] for q in range(n_col)],
                                  axis=1).astype(BF16)
            y = jnp.dot(d2_ref[...], blk, preferred_element_type=F32)
            for q in range(n_col):
                y_scr[hf * n_col + q, pl.ds(k1, r, stride=rp), :] = y[:, q * LANES:(q + 1) * LANES]
        return carry

    lax.fori_loop(0, r // unroll, stage2, 0)

    @pl.when(hf == pl.num_programs(1) - 1)
    def _():
        n_all = y_scr.shape[0]
        for k2 in range(r):
            ys = [y_scr[q, k2 * rp:k2 * rp + r, :] for q in range(n_all)]
            ss = sum(jnp.sum(y * y, axis=-1, keepdims=True) for y in ys)
            inv = lax.rsqrt(ss * (1.0 / (n_all * LANES)) + EPS)
            for q in range(n_all):
                out_ref[0, k2 * r:(k2 + 1) * r, q * LANES:(q + 1) * LANES] = (
                    ys[q] * inv * gb_ref[:, q * LANES:(q + 1) * LANES]).astype(BF16)


def _fnet(zb_cols, gb, bsz):
    n_all, t, _ = zb_cols.shape
    s = t // bsz
    d_b = n_all * LANES
    n_col = 2
    r = FFT_RADIX
    rp, ap = r + FFT_PAD, 2 * r + FFT_PAD
    m_ch, d_tw, d2 = _fnet_tables(s, n_col * LANES)
    return pl.pallas_call(
        functools.partial(_fnet_kernel, n_col=n_col, unroll=8),
        grid=(bsz, n_all // n_col),
        in_specs=[
            pl.BlockSpec((n_col, s, LANES), lambda i, h: (h, i, 0)),
            pl.BlockSpec(m_ch.shape, lambda i, h: (0, 0)),
            pl.BlockSpec(d_tw.shape, lambda i, h: (0, 0, 0)),
            pl.BlockSpec(d2.shape, lambda i, h: (0, 0)),
            pl.BlockSpec((1, d_b), lambda i, h: (0, 0)),
        ],
        out_specs=pl.BlockSpec((1, s, d_b), lambda i, h: (i, 0, 0)),
        out_shape=jax.ShapeDtypeStruct((bsz, s, d_b), BF16),
        scratch_shapes=[pltpu.VMEM((2 * n_col, r * rp, LANES), F32), pltpu.VMEM((n_col, r * ap, LANES), F32),
                        pltpu.VMEM((n_all, r * rp, LANES), F32),
                        pltpu.VMEM(m_ch.shape, BF16), pltpu.VMEM(d2.shape, BF16)],
        compiler_params=pltpu.CompilerParams(dimension_semantics=("parallel", "arbitrary"),
                                             vmem_limit_bytes=VMEM_LIMIT),
        name="fnet",
    )(zb_cols, m_ch, d_tw, d2, gb)


def _post_kernel(x_ref, an_ref, bn_ref, woa_ref, wob_ref, g2_ref, wrt_ref, rb_ref, wsgu_ref, wsd_ref,
                 hs_ref, hn_ref, eidx_ref, gate_ref, cnt_ref, carry_scr,
                 *, n_experts, d_shared, steps_per_group):
    i = pl.program_id(0)

    @pl.when(i % steps_per_group == 0)
    def _():
        carry_scr[...] = jnp.zeros_like(carry_scr)

    h = (x_ref[...]
         + jnp.dot(an_ref[...], woa_ref[...], preferred_element_type=F32)
         + jnp.dot(bn_ref[...], wob_ref[...], preferred_element_type=F32))
    hn = _rms(h, g2_ref[...])
    hnb = hn.astype(BF16)
    gu = jnp.dot(hnb, wsgu_ref[...], preferred_element_type=F32)
    act = (_silu(gu[:, :d_shared]) * gu[:, d_shared:]).astype(BF16)
    hs = h + jnp.dot(act, wsd_ref[...], preferred_element_type=F32)
    n_slab = hn.shape[1] // LANES
    for s in range(n_slab):
        hn_ref[pl.ds(s, hn.shape[0], stride=n_slab), :] = hn[:, s * LANES:(s + 1) * LANES]
        hs_ref[pl.ds(s, hn.shape[0], stride=n_slab), :] = hs[:, s * LANES:(s + 1) * LANES]

    rows = hn.shape[0]
    eg = n_experts // N_EXPERT_GROUPS
    logits_t = lax.dot_general(wrt_ref[...], hn, (((1,), (1,)), ((), ())),
                               precision=lax.Precision.HIGHEST, preferred_element_type=F32)
    scores_t = 1.0 / (1.0 + jnp.exp(-logits_t))
    biased_t = scores_t + rb_ref[...]
    sub = lax.broadcasted_iota(jnp.int32, (eg, rows), 0)
    neg = -jnp.inf
    grp = [biased_t[g * eg:(g + 1) * eg, :] for g in range(N_EXPERT_GROUPS)]
    sco = [scores_t[g * eg:(g + 1) * eg, :] for g in range(N_EXPERT_GROUPS)]
    gid = [sub + g * eg for g in range(N_EXPERT_GROUPS)]

    gscore = []
    for g in range(N_EXPERT_GROUPS):
        m1 = jnp.max(grp[g], axis=0, keepdims=True)
        first = jnp.min(jnp.where(grp[g] == m1, sub, eg), axis=0, keepdims=True)
        m2 = jnp.max(jnp.where(sub == first, neg, grp[g]), axis=0, keepdims=True)
        gscore.append(m1 + m2)
    masked = []
    for g in range(N_EXPERT_GROUPS):
        beaten = jnp.zeros((1, rows), jnp.int32)
        for j in range(N_EXPERT_GROUPS):
            if j == g:
                continue
            wins = (gscore[j] > gscore[g]) if j > g else (gscore[j] >= gscore[g])
            beaten = beaten + wins.astype(jnp.int32)
        masked.append(jnp.where(beaten < TOPK_GROUPS, grp[g], neg))

    sel = [jnp.zeros((eg, rows), F32) for _ in range(N_EXPERT_GROUPS)]
    e_rows, w_rows = [], []
    for _ in range(TOP_K):
        m = masked[0]
        for g in range(1, N_EXPERT_GROUPS):
            m = jnp.maximum(m, masked[g])
        mk = jnp.max(m, axis=0, keepdims=True)
        cand = jnp.where(masked[0] == mk, gid[0], n_experts)
        for g in range(1, N_EXPERT_GROUPS):
            cand = jnp.minimum(cand, jnp.where(masked[g] == mk, gid[g], n_experts))
        ik = jnp.min(cand, axis=0, keepdims=True)
        wk = jnp.zeros((eg, rows), F32)
        for g in range(N_EXPERT_GROUPS):
            hit = gid[g] == ik
            masked[g] = jnp.where(hit, neg, masked[g])
            sel[g] = jnp.where(hit, 1.0, sel[g])
            wk = wk + jnp.where(hit, sco[g], 0.0)
        e_rows.append(ik)
        w_rows.append(jnp.sum(wk, axis=0, keepdims=True))
    wsum = w_rows[0]
    for k in range(1, TOP_K):
        wsum = wsum + w_rows[k]
    gates = [w_rows[k] / wsum * ROUTED_SCALE for k in range(TOP_K)]

    sel_t = jnp.concatenate(sel, axis=0)
    new_carry = carry_scr[...] + jnp.sum(sel_t, axis=1, keepdims=True)
    carry_scr[...] = new_carry
    cnt_ref[0] = new_carry.astype(jnp.int32)
    eidx_ref[...] = jnp.concatenate(e_rows, axis=0)
    gate_ref[...] = jnp.concatenate(gates, axis=0)


def _post(x2, an, bn, woa, wob, g2, wrt, rb, wsgu, wsd):
    t, d = x2.shape
    rows = FRONT_ROWS
    n_experts = wrt.shape[0]
    d_shared = wsd.shape[0]
    n_slab = d // LANES
    steps_per_group = GROUP_TOKENS // rows
    n_groups = t // GROUP_TOKENS
    const = lambda shape: pl.BlockSpec(shape, lambda i: (0,) * len(shape))
    row_blk = lambda w: pl.BlockSpec((rows, w), lambda i: (i, 0))
    slab_blk = pl.BlockSpec((rows * n_slab, LANES), lambda i: (i, 0))
    tok_blk = pl.BlockSpec((TOP_K, rows), lambda i: (0, i))
    return pl.pallas_call(
        functools.partial(_post_kernel, n_experts=n_experts, d_shared=d_shared,
                          steps_per_group=steps_per_group),
        grid=(t // rows,),
        in_specs=[row_blk(d), row_blk(an.shape[1]), row_blk(bn.shape[1]),
                  const(woa.shape), const(wob.shape), const((1, d)), const(wrt.shape), const(rb.shape),
                  const(wsgu.shape), const(wsd.shape)],
        out_specs=[slab_blk, slab_blk, tok_blk, tok_blk,
                   pl.BlockSpec((1, n_experts, LANES), lambda i: (i // steps_per_group, 0, 0))],
        out_shape=[jax.ShapeDtypeStruct((t * n_slab, LANES), F32), jax.ShapeDtypeStruct((t * n_slab, LANES), F32),
                   jax.ShapeDtypeStruct((TOP_K, t), jnp.int32), jax.ShapeDtypeStruct((TOP_K, t), F32),
                   jax.ShapeDtypeStruct((n_groups, n_experts, LANES), jnp.int32)],
        scratch_shapes=[pltpu.VMEM((n_experts, LANES), F32)],
        compiler_params=pltpu.CompilerParams(dimension_semantics=("arbitrary",),
                                             vmem_limit_bytes=VMEM_LIMIT),
        name="post",
    )(x2, an, bn, woa, wob, g2, wrt, rb, wsgu, wsd)


def _moe_kernel(cnt_ref, off_ref, next_ref, rows_ref, gates_ref, hn_ref, hs_hbm, wg_ref, wu_ref, wd_ref, gf_ref,
                out_ref, acc_scr, x_scr, y_scr, state_scr, sem, *, n_slab):
    g = pl.program_id(0)
    e = pl.program_id(1)
    n_e = pl.num_programs(1)
    tokens = GROUP_TOKENS
    blk = EXPERT_ROWS
    batch = 16

    def slab(ref, row0):
        return ref.at[pl.ds(pl.multiple_of(row0, n_slab), n_slab), :]

    def acc_init(group):
        return pltpu.make_async_copy(
            hs_hbm.at[pl.ds(pl.multiple_of(group * tokens * n_slab, n_slab), tokens * n_slab), :],
            acc_scr.at[pl.ds(0, tokens * n_slab), :], sem)

    def gather(base):
        for r in range(blk):
            slab(x_scr, r * n_slab)[...] = slab(hn_ref, rows_ref[base + r])[...]

    def scatter(base):
        for j in range(blk // batch):
            rows, vals = [], []
            for u in range(batch):
                r = j * batch + u
                row0 = rows_ref[base + r]
                rows.append(row0)
                vals.append(slab(acc_scr, row0)[...] + slab(y_scr, r * n_slab)[...])
            for u in reversed(range(batch)):
                slab(acc_scr, rows[u])[...] = vals[u]

    n = cnt_ref[g * n_e + e]
    off = off_ref[g * n_e + e]

    @pl.when(e == 0)
    def _():
        @pl.when(g == 0)
        def _():
            acc_init(g).start()

        y_scr[...] = jnp.zeros_like(y_scr)
        state_scr[0] = 0
        gather(jnp.where(n > 0, off, next_ref[g * n_e + e]))
        acc_init(g).wait()

    n_blocks = (n + blk - 1) // blk
    nxt = next_ref[g * n_e + e]
    row_id = lax.broadcasted_iota(jnp.int32, (blk, 1), 0)
    diag_row = lax.broadcasted_iota(jnp.int32, (LANES, 2 * LANES), 0)
    diag_lane = lax.broadcasted_iota(jnp.int32, (LANES, 2 * LANES), 1)

    def block(i, carry):
        base = off + i * blk
        xb = jnp.concatenate([x_scr[pl.ds(s, blk, stride=n_slab), :] for s in range(n_slab)], axis=1).astype(BF16)
        scatter(state_scr[0])
        gather(jnp.where(i + 1 < n_blocks, base + blk, nxt))
        act = (_silu(jnp.dot(xb, wg_ref[0], preferred_element_type=F32))
               * jnp.dot(xb, wu_ref[0], preferred_element_type=F32)).astype(BF16)
        y = jnp.dot(act, wd_ref[0], preferred_element_type=F32)
        first_row = lax.shift_right_logical(base, LANES.bit_length() - 1)
        shift = base & (LANES - 1)
        cols = []
        for j in range(blk // LANES):
            pair = jnp.concatenate([gates_ref[pl.ds(first_row + j, 1), :],
                                    gates_ref[pl.ds(first_row + j + 1, 1), :]], axis=1)
            cols.append(jnp.sum(jnp.where(diag_lane == diag_row + shift, pair, 0.0), axis=1, keepdims=True))
        gate_col = cols[0] if len(cols) == 1 else jnp.concatenate(cols, axis=0)
        y = jnp.where(row_id < n - i * blk, y * gate_col, 0.0)
        for s in range(n_slab):
            y_scr[pl.ds(s, blk, stride=n_slab), :] = y[:, s * LANES:(s + 1) * LANES]
        state_scr[0] = base
        return carry

    lax.fori_loop(0, n_blocks, block, 0)

    @pl.when(e == n_e - 1)
    def _():
        scatter(state_scr[0])
        rows = 256
        for c in range(tokens // rows):
            acc = jnp.concatenate([acc_scr[pl.ds(c * rows * n_slab + s, rows, stride=n_slab), :]
                                   for s in range(n_slab)], axis=1)
            out_ref[c * rows:(c + 1) * rows, :] = _rms(acc, gf_ref[...])

        @pl.when(g + 1 < pl.num_programs(0))
        def _():
            acc_init(g + 1).start()


def _moe(cnt_flat, off_flat, next_flat, rows_flat, gates2d, hn3, hs3, wg, wu, wd, gf):
    n_slab = gf.shape[1] // LANES
    t = hn3.shape[0] // n_slab
    d = gf.shape[1]
    n_experts, d_expert = wd.shape[0], wd.shape[1]
    tokens = GROUP_TOKENS
    n_groups = t // tokens
    list_len = tokens * TOP_K + LIST_PAD
    assert EXPERT_ROWS % LANES == 0 and EXPERT_ROWS + LANES <= LIST_PAD and list_len % (8 * LANES) == 0
    grid_spec = pltpu.PrefetchScalarGridSpec(
        num_scalar_prefetch=3,
        grid=(n_groups, n_experts),
        in_specs=[pl.BlockSpec((list_len,), lambda g, e, *_: (g,), memory_space=pltpu.SMEM),
                  pl.BlockSpec((list_len // LANES, LANES), lambda g, e, *_: (g, 0)),
                  pl.BlockSpec((tokens * n_slab, LANES), lambda g, e, *_: (g, 0)),
                  pl.BlockSpec(memory_space=pl.ANY),
                  pl.BlockSpec((1, d, d_expert), lambda g, e, *_: (e, 0, 0)),
                  pl.BlockSpec((1, d, d_expert), lambda g, e, *_: (e, 0, 0)),
                  pl.BlockSpec((1, d_expert, d), lambda g, e, *_: (e, 0, 0)),
                  pl.BlockSpec((1, d), lambda g, e, *_: (0, 0))],
        out_specs=pl.BlockSpec((tokens, d), lambda g, e, *_: (g, 0)),
        scratch_shapes=[pltpu.VMEM((tokens * n_slab, LANES), F32),
                        pltpu.VMEM((EXPERT_ROWS * n_slab, LANES), F32),
                        pltpu.VMEM((EXPERT_ROWS * n_slab, LANES), F32),
                        pltpu.SMEM((1,), jnp.int32),
                        pltpu.SemaphoreType.DMA],
    )
    return pl.pallas_call(
        functools.partial(_moe_kernel, n_slab=n_slab),
        grid_spec=grid_spec,
        out_shape=jax.ShapeDtypeStruct((t, d), F32),
        compiler_params=pltpu.CompilerParams(dimension_semantics=("arbitrary", "arbitrary"),
                                             vmem_limit_bytes=VMEM_LIMIT),
        name="moe",
    )(cnt_flat, off_flat, next_flat, rows_flat, gates2d, hn3, hs3, wg, wu, wd, gf)


def kernel(x, norm1_g, w_in, sgu_ln_g, sgu_ln_b, w_spatial, b_spatial, out_norm_a_g, out_norm_b_g, w_out,
           norm2_g, w_router, router_bias, w_gate, w_up, w_down, ws_gate, ws_up, ws_down, final_norm_g):
    bsz, s, d = x.shape
    depth = w_in.shape[0]
    assert depth == 1, "the fused final norm assumes a single trunk layer"
    l = 0
    t = bsz * s
    d_a = sgu_ln_g.shape[-1]
    d_b = out_norm_b_g.shape[-1]
    n_heads, chunk = w_spatial.shape[1], w_spatial.shape[2]
    n_experts = w_router.shape[-1]
    assert d_a == n_heads * A_HEAD_DIM and chunk == LANES and 2 * A_HEAD_DIM == LANES
    assert t % FRONT_ROWS == 0 and FRONT_ROWS % chunk == 0 and s % FRONT_ROWS == 0
    assert t % GROUP_TOKENS == 0 and GROUP_TOKENS % FRONT_ROWS == 0
    assert (GROUP_TOKENS * TOP_K) & (GROUP_TOKENS * TOP_K - 1) == 0, "sort keys pack the assignment index in low bits"
    row = lambda v: v.reshape(1, -1).astype(F32)

    x2 = x.reshape(t, d)

    win = w_in[l].astype(BF16)
    ws = w_spatial[l]
    ws_pair = jnp.concatenate([ws[0::2], ws[1::2]], axis=2).astype(BF16)
    bs_full = jnp.repeat(b_spatial[l].T, A_HEAD_DIM, axis=1).astype(F32)
    woa = w_out[l][:d_a].astype(BF16)
    wob = w_out[l][d_a:].astype(BF16)
    wrt = w_router[l].T.astype(F32)
    rb = jnp.broadcast_to(router_bias[l].astype(F32)[:, None], (n_experts, 1))
    wsgu = jnp.concatenate([ws_gate[l], ws_up[l]], axis=1).astype(BF16)
    wsd = ws_down[l].astype(BF16)
    wg = w_gate[l].astype(BF16)
    wu = w_up[l].astype(BF16)
    wd = w_down[l].astype(BF16)

    an, zb = _front(x2, row(norm1_g[l]), win, row(sgu_ln_g[l]), row(sgu_ln_b[l]), ws_pair, bs_full,
                    row(out_norm_a_g[l]), d_a=d_a, d_b=d_b, chunk=chunk)
    bn = _fnet(zb, row(out_norm_b_g[l]), bsz).reshape(t, d_b)

    hs3, hn3, eidx, gate, cnt = _post(x2, an, bn, woa, wob, row(norm2_g[l]), wrt, rb, wsgu, wsd)

    n_groups = t // GROUP_TOKENS
    n_assign = GROUP_TOKENS * TOP_K
    counts = cnt[:, :, 0]
    offs = jnp.cumsum(counts, axis=1) - counts
    tok_bits = (n_assign - 1).bit_length()
    local = jnp.arange(n_assign, dtype=jnp.int32).reshape(1, GROUP_TOKENS, TOP_K)
    eidx_g = eidx.reshape(TOP_K, n_groups, GROUP_TOKENS).transpose(1, 2, 0)
    gate_g = gate.reshape(TOP_K, n_groups, GROUP_TOKENS).transpose(1, 2, 0).reshape(n_groups, n_assign)
    keys = ((eidx_g << tok_bits) | local).reshape(n_groups, n_assign)
    keys_sorted, gates_sorted = lax.sort((keys, gate_g), dimension=1, num_keys=1, is_stable=False)
    rows_sorted = (keys_sorted & (n_assign - 1)) // TOP_K * (d // LANES)
    fill = ((0, 0), (0, LIST_PAD))
    rows_flat = jnp.pad(rows_sorted, fill).reshape(-1)
    gates2d = jnp.pad(gates_sorted, fill).reshape(-1, LANES)
    e_row = jnp.arange(n_experts, dtype=jnp.int32)
    later = (e_row[None, None, :] > e_row[None, :, None]) & (counts[:, None, :] > 0)
    first_later = jnp.min(jnp.where(later, e_row[None, None, :], n_experts), axis=2)
    next_off = jnp.sum(jnp.where(first_later[:, :, None] == e_row[None, None, :], offs[:, None, :], 0), axis=2)
    flat = lambda a: a.reshape(-1).astype(jnp.int32)

    out = _moe(flat(counts), flat(offs), flat(next_off), rows_flat, gates2d,
               hn3, hs3, wg, wu, wd, row(final_norm_g))
    return out.reshape(bsz, s, d)
```

```python
import functools
import math

import numpy as np
import jax
import jax.numpy as jnp
from jax import lax
from jax.experimental import pallas as pl
from jax.experimental.pallas import tpu as pltpu

A_HEAD_DIM = 64
B_GROUP_DIM = 64
TOP_K = 8
N_EXPERT_GROUPS = 8
TOPK_GROUPS = 4
ROUTED_SCALE = 2.5
EPS = 1e-6

LANES = 128
FRONT_ROWS = 1024
EXPERT_ROWS = 128
LIST_PAD = 1024
GROUP_TOKENS = 2048
FFT_RADIX = 64
FFT_PAD = 8
VMEM_LIMIT = 56 * 1024 * 1024

F32 = jnp.float32
BF16 = jnp.bfloat16


def _rms(x, g):
    return x * lax.rsqrt(jnp.mean(x * x, axis=-1, keepdims=True) + EPS) * g


def _gelu(x):
    return 0.5 * x * (1.0 + lax.erf(x * (1.0 / math.sqrt(2.0))))


def _silu(x):
    return x / (1.0 + jnp.exp(-x))


def _front_kernel(x_ref, g1_ref, win_ref, lng_ref, lnb_ref, ws_ref, bs_ref, ga_ref,
                  an_ref, zb_ref, ya_scr, *, d_a, chunk):
    x = x_ref[...]
    xn = _rms(x, g1_ref[...]).astype(BF16)
    z = jnp.dot(xn, win_ref[...], preferred_element_type=F32)
    for q in range(zb_ref.shape[0]):
        zb_ref[q] = z[:, 2 * d_a + q * LANES:2 * d_a + (q + 1) * LANES].astype(BF16)
    u = _gelu(z[:, :d_a])
    v = _gelu(z[:, d_a:2 * d_a])
    mu = jnp.mean(v, axis=-1, keepdims=True)
    vc = v - mu
    v = vc * lax.rsqrt(jnp.mean(vc * vc, axis=-1, keepdims=True) + EPS) * lng_ref[...] + lnb_ref[...]
    rows = x.shape[0]
    lane = lax.broadcasted_iota(jnp.int32, (chunk, LANES), 1)
    low = lane < A_HEAD_DIM
    for c in range(rows // chunk):
        r0 = c * chunk
        for hp in range(d_a // LANES):
            c0 = hp * LANES
            vp = v[r0:r0 + chunk, c0:c0 + LANES]
            rhs = jnp.concatenate([jnp.where(low, vp, 0.0), jnp.where(low, 0.0, vp)], axis=0).astype(BF16)
            mixed = jnp.dot(ws_ref[hp], rhs, preferred_element_type=F32) + bs_ref[:, c0:c0 + LANES]
            ya_scr[r0:r0 + chunk, c0:c0 + LANES] = u[r0:r0 + chunk, c0:c0 + LANES] * mixed
    an_ref[...] = _rms(ya_scr[...], ga_ref[...]).astype(BF16)


def _front(x2, g1, win, lng, lnb, ws_pair, bs_full, ga, *, d_a, d_b, chunk):
    t, d = x2.shape
    d_in = win.shape[1]
    rows = FRONT_ROWS
    const = lambda shape: pl.BlockSpec(shape, lambda i: (0,) * len(shape))
    return pl.pallas_call(
        functools.partial(_front_kernel, d_a=d_a, chunk=chunk),
        grid=(t // rows,),
        in_specs=[
            pl.BlockSpec((rows, d), lambda i: (i, 0)),
            const((1, d)), const((d, d_in)), const((1, d_a)), const((1, d_a)),
            const(ws_pair.shape), const(bs_full.shape), const((1, d_a)),
        ],
        out_specs=[pl.BlockSpec((rows, d_a), lambda i: (i, 0)),
                   pl.BlockSpec((d_b // LANES, rows, LANES), lambda i: (0, i, 0))],
        out_shape=[jax.ShapeDtypeStruct((t, d_a), BF16), jax.ShapeDtypeStruct((d_b // LANES, t, LANES), BF16)],
        scratch_shapes=[pltpu.VMEM((rows, d_a), F32)],
        compiler_params=pltpu.CompilerParams(dimension_semantics=("parallel",),
                                             vmem_limit_bytes=VMEM_LIMIT),
        name="front",
    )(x2, g1, win, lng, lnb, ws_pair, bs_full, ga)


def _fnet_tables(seq, half_w):
    r = FFT_RADIX
    assert seq == r * r
    j = np.arange(B_GROUP_DIM)
    ang = 2.0 * np.pi * np.outer(j, j) / B_GROUP_DIM
    ng = half_w // B_GROUP_DIM
    eye = np.eye(ng)
    m_ch = np.concatenate([np.kron(eye, np.cos(ang)), np.kron(eye, np.sin(ang))], axis=1)
    k1 = np.arange(r)[:, None]
    n1 = np.arange(r)[None, :]
    d_tw = np.zeros((r, 2 * r, 2 * r))
    for n2 in range(r):
        th = 2.0 * np.pi * (k1 * n1 / r + k1 * n2 / (r * r))
        c, s = np.cos(th), np.sin(th)
        d_tw[n2] = np.block([[c, -s], [s, c]])
    ph = 2.0 * np.pi * np.outer(np.arange(r), np.arange(r)) / r
    scale = 1.0 / math.sqrt(seq * B_GROUP_DIM)
    d2 = np.concatenate([np.cos(ph), -np.sin(ph)], axis=1) * scale
    return (jnp.asarray(m_ch, F32), jnp.asarray(d_tw, F32), jnp.asarray(d2, F32))


def _fnet_kernel(z_ref, mch32_ref, dtw_ref, d232_ref, gb_ref, out_ref, w_scr, a2_scr, y_scr, mch_ref, d2_ref,
                 *, n_col, unroll):
    r = FFT_RADIX
    rp = r + FFT_PAD
    ap = 2 * r + FFT_PAD
    hf = pl.program_id(1)
    half_w = n_col * LANES
    mch_ref[...] = mch32_ref[...].astype(BF16)
    d2_ref[...] = d232_ref[...].astype(BF16)

    chunk = 8 * r
    for c in range(r * r // chunk):
        zc = jnp.concatenate([z_ref[q, c * chunk:(c + 1) * chunk, :] for q in range(n_col)], axis=1)
        w = jnp.dot(zc, mch_ref[...], preferred_element_type=F32)
        for j in range(chunk // r):
            n1 = c * (chunk // r) + j
            for p in range(2 * n_col):
                w_scr[p, n1 * rp:n1 * rp + r, :] = w[j * r:(j + 1) * r, p * LANES:(p + 1) * LANES]

    def stage1(it, carry):
        for u in range(unroll):
            n2 = it * unroll + u
            rhs = jnp.concatenate(
                [jnp.concatenate([w_scr[ri * n_col + q, pl.ds(n2, r, stride=rp), :] for q in range(n_col)], axis=1)
                 for ri in range(2)], axis=0).astype(BF16)
            a = jnp.dot(dtw_ref[n2].astype(BF16), rhs, preferred_element_type=F32)
            for q in range(n_col):
                a2_scr[q, pl.ds(n2, r, stride=ap), :] = a[:r, q * LANES:(q + 1) * LANES]
                a2_scr[q, pl.ds(r + n2, r, stride=ap), :] = a[r:, q * LANES:(q + 1) * LANES]
        return carry

    lax.fori_loop(0, r // unroll, stage1, 0)

    def stage2(it, carry):
        for u in range(unroll):
            k1 = it * unroll + u
            row0 = pl.multiple_of(k1 * ap, 8)
            blk = jnp.concatenate([a2_scr[q, pl.ds(row0, 2 * r), :] for q in range(n_col)],
                                  axis=1).astype(BF16)
            y = jnp.dot(d2_ref[...], blk, preferred_element_type=F32)
            for q in range(n_col):
                y_scr[hf * n_col + q, pl.ds(k1, r, stride=rp), :] = y[:, q * LANES:(q + 1) * LANES]
        return carry

    lax.fori_loop(0, r // unroll, stage2, 0)

    @pl.when(hf == pl.num_programs(1) - 1)
    def _():
        n_all = y_scr.shape[0]
        for k2 in range(r):
            ys = [y_scr[q, k2 * rp:k2 * rp + r, :] for q in range(n_all)]
            ss = sum(jnp.sum(y * y, axis=-1, keepdims=True) for y in ys)
            inv = lax.rsqrt(ss * (1.0 / (n_all * LANES)) + EPS)
            for q in range(n_all):
                out_ref[0, k2 * r:(k2 + 1) * r, q * LANES:(q + 1) * LANES] = (
                    ys[q] * inv * gb_ref[:, q * LANES:(q + 1) * LANES]).astype(BF16)


def _fnet(zb_cols, gb, bsz):
    n_all, t, _ = zb_cols.shape
    s = t // bsz
    d_b = n_all * LANES
    n_col = 2
    r = FFT_RADIX
    rp, ap = r + FFT_PAD, 2 * r + FFT_PAD
    m_ch, d_tw, d2 = _fnet_tables(s, n_col * LANES)
    return pl.pallas_call(
        functools.partial(_fnet_kernel, n_col=n_col, unroll=32),
        grid=(bsz, n_all // n_col),
        in_specs=[
            pl.BlockSpec((n_col, s, LANES), lambda i, h: (h, i, 0)),
            pl.BlockSpec(m_ch.shape, lambda i, h: (0, 0)),
            pl.BlockSpec(d_tw.shape, lambda i, h: (0, 0, 0)),
            pl.BlockSpec(d2.shape, lambda i, h: (0, 0)),
            pl.BlockSpec((1, d_b), lambda i, h: (0, 0)),
        ],
        out_specs=pl.BlockSpec((1, s, d_b), lambda i, h: (i, 0, 0)),
        out_shape=jax.ShapeDtypeStruct((bsz, s, d_b), BF16),
        scratch_shapes=[pltpu.VMEM((2 * n_col, r * rp, LANES), F32), pltpu.VMEM((n_col, r * ap, LANES), F32),
                        pltpu.VMEM((n_all, r * rp, LANES), F32),
                        pltpu.VMEM(m_ch.shape, BF16), pltpu.VMEM(d2.shape, BF16)],
        compiler_params=pltpu.CompilerParams(dimension_semantics=("parallel", "arbitrary"),
                                             vmem_limit_bytes=VMEM_LIMIT),
        name="fnet",
    )(zb_cols, m_ch, d_tw, d2, gb)


def _post_kernel(x_ref, an_ref, bn_ref, woa_ref, wob_ref, g2_ref, wrt_ref, rb_ref, wsgu_ref, wsd_ref,
                 hs_ref, hn_ref, eidx_ref, gate_ref, cnt_ref, carry_scr,
                 *, n_experts, d_shared, steps_per_group):
    i = pl.program_id(0)

    @pl.when(i % steps_per_group == 0)
    def _():
        carry_scr[...] = jnp.zeros_like(carry_scr)

    h = (x_ref[...]
         + jnp.dot(an_ref[...], woa_ref[...], preferred_element_type=F32)
         + jnp.dot(bn_ref[...], wob_ref[...], preferred_element_type=F32))
    hn = _rms(h, g2_ref[...])
    hnb = hn.astype(BF16)
    gu = jnp.dot(hnb, wsgu_ref[...], preferred_element_type=F32)
    act = (_silu(gu[:, :d_shared]) * gu[:, d_shared:]).astype(BF16)
    hs = h + jnp.dot(act, wsd_ref[...], preferred_element_type=F32)
    n_slab = hn.shape[1] // LANES
    for s in range(n_slab):
        hn_ref[pl.ds(s, hn.shape[0], stride=n_slab), :] = hn[:, s * LANES:(s + 1) * LANES]
        hs_ref[pl.ds(s, hn.shape[0], stride=n_slab), :] = hs[:, s * LANES:(s + 1) * LANES]

    rows = hn.shape[0]
    eg = n_experts // N_EXPERT_GROUPS
    logits_t = lax.dot_general(wrt_ref[...], hn, (((1,), (1,)), ((), ())),
                               precision=lax.Precision.HIGHEST, preferred_element_type=F32)
    scores_t = 1.0 / (1.0 + jnp.exp(-logits_t))
    biased_t = scores_t + rb_ref[...]
    sub = lax.broadcasted_iota(jnp.int32, (eg, rows), 0)
    neg = -jnp.inf
    grp = [biased_t[g * eg:(g + 1) * eg, :] for g in range(N_EXPERT_GROUPS)]
    sco = [scores_t[g * eg:(g + 1) * eg, :] for g in range(N_EXPERT_GROUPS)]
    gid = [sub + g * eg for g in range(N_EXPERT_GROUPS)]

    gscore = []
    for g in range(N_EXPERT_GROUPS):
        m1 = jnp.max(grp[g], axis=0, keepdims=True)
        first = jnp.min(jnp.where(grp[g] == m1, sub, eg), axis=0, keepdims=True)
        m2 = jnp.max(jnp.where(sub == first, neg, grp[g]), axis=0, keepdims=True)
        gscore.append(m1 + m2)
    masked = []
    for g in range(N_EXPERT_GROUPS):
        beaten = jnp.zeros((1, rows), jnp.int32)
        for j in range(N_EXPERT_GROUPS):
            if j == g:
                continue
            wins = (gscore[j] > gscore[g]) if j > g else (gscore[j] >= gscore[g])
            beaten = beaten + wins.astype(jnp.int32)
        masked.append(jnp.where(beaten < TOPK_GROUPS, grp[g], neg))

    sel = [jnp.zeros((eg, rows), F32) for _ in range(N_EXPERT_GROUPS)]
    e_rows, w_rows = [], []
    for _ in range(TOP_K):
        m = masked[0]
        for g in range(1, N_EXPERT_GROUPS):
            m = jnp.maximum(m, masked[g])
        mk = jnp.max(m, axis=0, keepdims=True)
        cand = jnp.where(masked[0] == mk, gid[0], n_experts)
        for g in range(1, N_EXPERT_GROUPS):
            cand = jnp.minimum(cand, jnp.where(masked[g] == mk, gid[g], n_experts))
        ik = jnp.min(cand, axis=0, keepdims=True)
        wk = jnp.zeros((eg, rows), F32)
        for g in range(N_EXPERT_GROUPS):
            hit = gid[g] == ik
            masked[g] = jnp.where(hit, neg, masked[g])
            sel[g] = jnp.where(hit, 1.0, sel[g])
            wk = wk + jnp.where(hit, sco[g], 0.0)
        e_rows.append(ik)
        w_rows.append(jnp.sum(wk, axis=0, keepdims=True))
    wsum = w_rows[0]
    for k in range(1, TOP_K):
        wsum = wsum + w_rows[k]
    gates = [w_rows[k] / wsum * ROUTED_SCALE for k in range(TOP_K)]

    sel_t = jnp.concatenate(sel, axis=0)
    new_carry = carry_scr[...] + jnp.sum(sel_t, axis=1, keepdims=True)
    carry_scr[...] = new_carry
    cnt_ref[0] = new_carry.astype(jnp.int32)
    eidx_ref[...] = jnp.concatenate(e_rows, axis=0)
    gate_ref[...] = jnp.concatenate(gates, axis=0)


def _post(x2, an, bn, woa, wob, g2, wrt, rb, wsgu, wsd):
    t, d = x2.shape
    rows = FRONT_ROWS
    n_experts = wrt.shape[0]
    d_shared = wsd.shape[0]
    n_slab = d // LANES
    steps_per_group = GROUP_TOKENS // rows
    n_groups = t // GROUP_TOKENS
    const = lambda shape: pl.BlockSpec(shape, lambda i: (0,) * len(shape))
    row_blk = lambda w: pl.BlockSpec((rows, w), lambda i: (i, 0))
    slab_blk = pl.BlockSpec((rows * n_slab, LANES), lambda i: (i, 0))
    tok_blk = pl.BlockSpec((TOP_K, rows), lambda i: (0, i))
    return pl.pallas_call(
        functools.partial(_post_kernel, n_experts=n_experts, d_shared=d_shared,
                          steps_per_group=steps_per_group),
        grid=(t // rows,),
        in_specs=[row_blk(d), row_blk(an.shape[1]), row_blk(bn.shape[1]),
                  const(woa.shape), const(wob.shape), const((1, d)), const(wrt.shape), const(rb.shape),
                  const(wsgu.shape), const(wsd.shape)],
        out_specs=[slab_blk, slab_blk, tok_blk, tok_blk,
                   pl.BlockSpec((1, n_experts, LANES), lambda i: (i // steps_per_group, 0, 0))],
        out_shape=[jax.ShapeDtypeStruct((t * n_slab, LANES), F32), jax.ShapeDtypeStruct((t * n_slab, LANES), F32),
                   jax.ShapeDtypeStruct((TOP_K, t), jnp.int32), jax.ShapeDtypeStruct((TOP_K, t), F32),
                   jax.ShapeDtypeStruct((n_groups, n_experts, LANES), jnp.int32)],
        scratch_shapes=[pltpu.VMEM((n_experts, LANES), F32)],
        compiler_params=pltpu.CompilerParams(dimension_semantics=("arbitrary",),
                                             vmem_limit_bytes=VMEM_LIMIT),
        name="post",
    )(x2, an, bn, woa, wob, g2, wrt, rb, wsgu, wsd)


def _moe_kernel(cnt_ref, off_ref, next_ref, rows_ref, gates_ref, hn_ref, hs_hbm, wg_ref, wu_ref, wd_ref, gf_ref,
                out_ref, acc_scr, x_scr, y_scr, state_scr, sem, *, n_slab):
    g = pl.program_id(0)
    e = pl.program_id(1)
    n_e = pl.num_programs(1)
    tokens = GROUP_TOKENS
    blk = EXPERT_ROWS
    batch = 16

    def slab(ref, row0):
        return ref.at[pl.ds(pl.multiple_of(row0, n_slab), n_slab), :]

    def acc_init(group):
        return pltpu.make_async_copy(
            hs_hbm.at[pl.ds(pl.multiple_of(group * tokens * n_slab, n_slab), tokens * n_slab), :],
            acc_scr.at[pl.ds(0, tokens * n_slab), :], sem)

    def gather(base):
        for r in range(blk):
            slab(x_scr, r * n_slab)[...] = slab(hn_ref, rows_ref[base + r])[...]

    def scatter(base):
        for j in range(blk // batch):
            rows, vals = [], []
            for u in range(batch):
                r = j * batch + u
                row0 = rows_ref[base + r]
                rows.append(row0)
                vals.append(slab(acc_scr, row0)[...] + slab(y_scr, r * n_slab)[...])
            for u in reversed(range(batch)):
                slab(acc_scr, rows[u])[...] = vals[u]

    n = cnt_ref[g * n_e + e]
    off = off_ref[g * n_e + e]

    @pl.when(e == 0)
    def _():
        @pl.when(g == 0)
        def _():
            acc_init(g).start()

        y_scr[...] = jnp.zeros_like(y_scr)
        state_scr[0] = 0
        gather(jnp.where(n > 0, off, next_ref[g * n_e + e]))
        acc_init(g).wait()

    n_blocks = (n + blk - 1) // blk
    nxt = next_ref[g * n_e + e]
    row_id = lax.broadcasted_iota(jnp.int32, (blk, 1), 0)
    diag_row = lax.broadcasted_iota(jnp.int32, (LANES, 2 * LANES), 0)
    diag_lane = lax.broadcasted_iota(jnp.int32, (LANES, 2 * LANES), 1)

    def block(i, carry):
        base = off + i * blk
        xb = jnp.concatenate([x_scr[pl.ds(s, blk, stride=n_slab), :] for s in range(n_slab)], axis=1).astype(BF16)
        scatter(state_scr[0])
        gather(jnp.where(i + 1 < n_blocks, base + blk, nxt))
        act = (_silu(jnp.dot(xb, wg_ref[0], preferred_element_type=F32))
               * jnp.dot(xb, wu_ref[0], preferred_element_type=F32)).astype(BF16)
        y = jnp.dot(act, wd_ref[0], preferred_element_type=F32)
        first_row = lax.shift_right_logical(base, LANES.bit_length() - 1)
        shift = base & (LANES - 1)
        cols = []
        for j in range(blk // LANES):
            pair = jnp.concatenate([gates_ref[pl.ds(first_row + j, 1), :],
                                    gates_ref[pl.ds(first_row + j + 1, 1), :]], axis=1)
            cols.append(jnp.sum(jnp.where(diag_lane == diag_row + shift, pair, 0.0), axis=1, keepdims=True))
        gate_col = cols[0] if len(cols) == 1 else jnp.concatenate(cols, axis=0)
        y = jnp.where(row_id < n - i * blk, y * gate_col, 0.0)
        for s in range(n_slab):
            y_scr[pl.ds(s, blk, stride=n_slab), :] = y[:, s * LANES:(s + 1) * LANES]
        state_scr[0] = base
        return carry

    lax.fori_loop(0, n_blocks, block, 0)

    @pl.when(e == n_e - 1)
    def _():
        scatter(state_scr[0])
        rows = 256
        for c in range(tokens // rows):
            acc = jnp.concatenate([acc_scr[pl.ds(c * rows * n_slab + s, rows, stride=n_slab), :]
                                   for s in range(n_slab)], axis=1)
            out_ref[c * rows:(c + 1) * rows, :] = _rms(acc, gf_ref[...])

        @pl.when(g + 1 < pl.num_programs(0))
        def _():
            acc_init(g + 1).start()


def _moe(cnt_flat, off_flat, next_flat, rows_flat, gates2d, hn3, hs3, wg, wu, wd, gf):
    n_slab = gf.shape[1] // LANES
    t = hn3.shape[0] // n_slab
    d = gf.shape[1]
    n_experts, d_expert = wd.shape[0], wd.shape[1]
    tokens = GROUP_TOKENS
    n_groups = t // tokens
    list_len = tokens * TOP_K + LIST_PAD
    assert EXPERT_ROWS % LANES == 0 and EXPERT_ROWS + LANES <= LIST_PAD and list_len % (8 * LANES) == 0
    grid_spec = pltpu.PrefetchScalarGridSpec(
        num_scalar_prefetch=3,
        grid=(n_groups, n_experts),
        in_specs=[pl.BlockSpec((list_len,), lambda g, e, *_: (g,), memory_space=pltpu.SMEM),
                  pl.BlockSpec((list_len // LANES, LANES), lambda g, e, *_: (g, 0)),
                  pl.BlockSpec((tokens * n_slab, LANES), lambda g, e, *_: (g, 0)),
                  pl.BlockSpec(memory_space=pl.ANY),
                  pl.BlockSpec((1, d, d_expert), lambda g, e, *_: (e, 0, 0)),
                  pl.BlockSpec((1, d, d_expert), lambda g, e, *_: (e, 0, 0)),
                  pl.BlockSpec((1, d_expert, d), lambda g, e, *_: (e, 0, 0)),
                  pl.BlockSpec((1, d), lambda g, e, *_: (0, 0))],
        out_specs=pl.BlockSpec((tokens, d), lambda g, e, *_: (g, 0)),
        scratch_shapes=[pltpu.VMEM((tokens * n_slab, LANES), F32),
                        pltpu.VMEM((EXPERT_ROWS * n_slab, LANES), F32),
                        pltpu.VMEM((EXPERT_ROWS * n_slab, LANES), F32),
                        pltpu.SMEM((1,), jnp.int32),
                        pltpu.SemaphoreType.DMA],
    )
    return pl.pallas_call(
        functools.partial(_moe_kernel, n_slab=n_slab),
        grid_spec=grid_spec,
        out_shape=jax.ShapeDtypeStruct((t, d), F32),
        compiler_params=pltpu.CompilerParams(dimension_semantics=("arbitrary", "arbitrary"),
                                             vmem_limit_bytes=VMEM_LIMIT),
        name="moe",
    )(cnt_flat, off_flat, next_flat, rows_flat, gates2d, hn3, hs3, wg, wu, wd, gf)


def kernel(x, norm1_g, w_in, sgu_ln_g, sgu_ln_b, w_spatial, b_spatial, out_norm_a_g, out_norm_b_g, w_out,
           norm2_g, w_router, router_bias, w_gate, w_up, w_down, ws_gate, ws_up, ws_down, final_norm_g):
    bsz, s, d = x.shape
    depth = w_in.shape[0]
    assert depth == 1, "the fused final norm assumes a single trunk layer"
    l = 0
    t = bsz * s
    d_a = sgu_ln_g.shape[-1]
    d_b = out_norm_b_g.shape[-1]
    n_heads, chunk = w_spatial.shape[1], w_spatial.shape[2]
    n_experts = w_router.shape[-1]
    assert d_a == n_heads * A_HEAD_DIM and chunk == LANES and 2 * A_HEAD_DIM == LANES
    assert t % FRONT_ROWS == 0 and FRONT_ROWS % chunk == 0 and s % FRONT_ROWS == 0
    assert t % GROUP_TOKENS == 0 and GROUP_TOKENS % FRONT_ROWS == 0
    assert (GROUP_TOKENS * TOP_K) & (GROUP_TOKENS * TOP_K - 1) == 0, "sort keys pack the assignment index in low bits"
    row = lambda v: v.reshape(1, -1).astype(F32)

    x2 = x.reshape(t, d)

    win = w_in[l].astype(BF16)
    ws = w_spatial[l]
    ws_pair = jnp.concatenate([ws[0::2], ws[1::2]], axis=2).astype(BF16)
    bs_full = jnp.repeat(b_spatial[l].T, A_HEAD_DIM, axis=1).astype(F32)
    woa = w_out[l][:d_a].astype(BF16)
    wob = w_out[l][d_a:].astype(BF16)
    wrt = w_router[l].T.astype(F32)
    rb = jnp.broadcast_to(router_bias[l].astype(F32)[:, None], (n_experts, 1))
    wsgu = jnp.concatenate([ws_gate[l], ws_up[l]], axis=1).astype(BF16)
    wsd = ws_down[l].astype(BF16)
    wg = w_gate[l].astype(BF16)
    wu = w_up[l].astype(BF16)
    wd = w_down[l].astype(BF16)

    an, zb = _front(x2, row(norm1_g[l]), win, row(sgu_ln_g[l]), row(sgu_ln_b[l]), ws_pair, bs_full,
                    row(out_norm_a_g[l]), d_a=d_a, d_b=d_b, chunk=chunk)
    bn = _fnet(zb, row(out_norm_b_g[l]), bsz).reshape(t, d_b)

    hs3, hn3, eidx, gate, cnt = _post(x2, an, bn, woa, wob, row(norm2_g[l]), wrt, rb, wsgu, wsd)

    n_groups = t // GROUP_TOKENS
    n_assign = GROUP_TOKENS * TOP_K
    counts = cnt[:, :, 0]
    offs = jnp.cumsum(counts, axis=1) - counts
    tok_bits = (n_assign - 1).bit_length()
    local = jnp.arange(n_assign, dtype=jnp.int32).reshape(1, GROUP_TOKENS, TOP_K)
    eidx_g = eidx.reshape(TOP_K, n_groups, GROUP_TOKENS).transpose(1, 2, 0)
    gate_g = gate.reshape(TOP_K, n_groups, GROUP_TOKENS).transpose(1, 2, 0).reshape(n_groups, n_assign)
    keys = ((eidx_g << tok_bits) | local).reshape(n_groups, n_assign)
    keys_sorted, gates_sorted = lax.sort((keys, gate_g), dimension=1, num_keys=1, is_stable=False)
    rows_sorted = (keys_sorted & (n_assign - 1)) // TOP_K * (d // LANES)
    fill = ((0, 0), (0, LIST_PAD))
    rows_flat = jnp.pad(rows_sorted, fill).reshape(-1)
    gates2d = jnp.pad(gates_sorted, fill).reshape(-1, LANES)
    e_row = jnp.arange(n_experts, dtype=jnp.int32)
    later = (e_row[None, None, :] > e_row[None, :, None]) & (counts[:, None, :] > 0)
    first_later = jnp.min(jnp.where(later, e_row[None, None, :], n_experts), axis=2)
    next_off = jnp.sum(jnp.where(first_later[:, :, None] == e_row[None, None, :], offs[:, None, :], 0), axis=2)
    flat = lambda a: a.reshape(-1).astype(jnp.int32)

    out = _moe(flat(counts), flat(offs), flat(next_off), rows_flat, gates2d,
               hn3, hs3, wg, wu, wd, row(final_norm_g))
    return out.reshape(bsz, s, d)
```

```python
import functools
import math

import numpy as np
import jax
import jax.numpy as jnp
from jax import lax
from jax.experimental import pallas as pl
from jax.experimental.pallas import tpu as pltpu

A_HEAD_DIM = 64
B_GROUP_DIM = 64
TOP_K = 8
N_EXPERT_GROUPS = 8
TOPK_GROUPS = 4
ROUTED_SCALE = 2.5
EPS = 1e-6

LANES = 128
FRONT_ROWS = 1024
EXPERT_ROWS = 128
LIST_PAD = 1024
GROUP_TOKENS = 2048
FFT_RADIX = 64
FFT_PAD = 8
VMEM_LIMIT = 56 * 1024 * 1024

F32 = jnp.float32
BF16 = jnp.bfloat16


def _rms(x, g):
    return x * lax.rsqrt(jnp.mean(x * x, axis=-1, keepdims=True) + EPS) * g


def _gelu(x):
    return 0.5 * x * (1.0 + lax.erf(x * (1.0 / math.sqrt(2.0))))


def _silu(x):
    return x / (1.0 + jnp.exp(-x))


def _front_kernel(x_ref, g1_ref, win_ref, lng_ref, lnb_ref, ws_ref, bs_ref, ga_ref,
                  an_ref, zb_ref, ya_scr, *, d_a, chunk):
    x = x_ref[...]
    xn = _rms(x, g1_ref[...]).astype(BF16)
    z = jnp.dot(xn, win_ref[...], preferred_element_type=F32)
    for q in range(zb_ref.shape[0]):
        zb_ref[q] = z[:, 2 * d_a + q * LANES:2 * d_a + (q + 1) * LANES].astype(BF16)
    u = _gelu(z[:, :d_a])
    v = _gelu(z[:, d_a:2 * d_a])
    mu = jnp.mean(v, axis=-1, keepdims=True)
    vc = v - mu
    v = vc * lax.rsqrt(jnp.mean(vc * vc, axis=-1, keepdims=True) + EPS) * lng_ref[...] + lnb_ref[...]
    rows = x.shape[0]
    lane = lax.broadcasted_iota(jnp.int32, (chunk, LANES), 1)
    low = lane < A_HEAD_DIM
    for c in range(rows // chunk):
        r0 = c * chunk
        for hp in range(d_a // LANES):
            c0 = hp * LANES
            vp = v[r0:r0 + chunk, c0:c0 + LANES]
            rhs = jnp.concatenate([jnp.where(low, vp, 0.0), jnp.where(low, 0.0, vp)], axis=0).astype(BF16)
            mixed = jnp.dot(ws_ref[hp], rhs, preferred_element_type=F32) + bs_ref[:, c0:c0 + LANES]
            ya_scr[r0:r0 + chunk, c0:c0 + LANES] = u[r0:r0 + chunk, c0:c0 + LANES] * mixed
    an_ref[...] = _rms(ya_scr[...], ga_ref[...]).astype(BF16)


def _front(x2, g1, win, lng, lnb, ws_pair, bs_full, ga, *, d_a, d_b, chunk):
    t, d = x2.shape
    d_in = win.shape[1]
    rows = FRONT_ROWS
    const = lambda shape: pl.BlockSpec(shape, lambda i: (0,) * len(shape))
    return pl.pallas_call(
        functools.partial(_front_kernel, d_a=d_a, chunk=chunk),
        grid=(t // rows,),
        in_specs=[
            pl.BlockSpec((rows, d), lambda i: (i, 0)),
            const((1, d)), const((d, d_in)), const((1, d_a)), const((1, d_a)),
            const(ws_pair.shape), const(bs_full.shape), const((1, d_a)),
        ],
        out_specs=[pl.BlockSpec((rows, d_a), lambda i: (i, 0)),
                   pl.BlockSpec((d_b // LANES, rows, LANES), lambda i: (0, i, 0))],
        out_shape=[jax.ShapeDtypeStruct((t, d_a), BF16), jax.ShapeDtypeStruct((d_b // LANES, t, LANES), BF16)],
        scratch_shapes=[pltpu.VMEM((rows, d_a), F32)],
        compiler_params=pltpu.CompilerParams(dimension_semantics=("parallel",),
                                             vmem_limit_bytes=VMEM_LIMIT),
        name="front",
    )(x2, g1, win, lng, lnb, ws_pair, bs_full, ga)


def _fnet_tables(seq, half_w):
    r = FFT_RADIX
    assert seq == r * r
    j = np.arange(B_GROUP_DIM)
    ang = 2.0 * np.pi * np.outer(j, j) / B_GROUP_DIM
    ng = half_w // B_GROUP_DIM
    eye = np.eye(ng)
    m_ch = np.concatenate([np.kron(eye, np.cos(ang)), np.kron(eye, np.sin(ang))], axis=1)
    k1 = np.arange(r)[:, None]
    n1 = np.arange(r)[None, :]
    d_tw = np.zeros((r, 2 * r, 2 * r))
    for n2 in range(r):
        th = 2.0 * np.pi * (k1 * n1 / r + k1 * n2 / (r * r))
        c, s = np.cos(th), np.sin(th)
        d_tw[n2] = np.block([[c, -s], [s, c]])
    ph = 2.0 * np.pi * np.outer(np.arange(r), np.arange(r)) / r
    scale = 1.0 / math.sqrt(seq * B_GROUP_DIM)
    d2 = np.concatenate([np.cos(ph), -np.sin(ph)], axis=1) * scale
    return (jnp.asarray(m_ch, F32), jnp.asarray(d_tw, F32), jnp.asarray(d2, F32))


def _fnet_kernel(z_ref, mch32_ref, dtw_ref, d232_ref, gb_ref, out_ref, w_scr, a2_scr, y_scr, mch_ref, d2_ref,
                 *, n_col, unroll):
    r = FFT_RADIX
    rp = r + FFT_PAD
    ap = 2 * r + FFT_PAD
    hf = pl.program_id(1)
    half_w = n_col * LANES
    mch_ref[...] = mch32_ref[...].astype(BF16)
    d2_ref[...] = d232_ref[...].astype(BF16)

    chunk = 8 * r
    for c in range(r * r // chunk):
        zc = jnp.concatenate([z_ref[q, c * chunk:(c + 1) * chunk, :] for q in range(n_col)], axis=1)
        w = jnp.dot(zc, mch_ref[...], preferred_element_type=F32)
        for j in range(chunk // r):
            n1 = c * (chunk // r) + j
            for p in range(2 * n_col):
                w_scr[p, n1 * rp:n1 * rp + r, :] = w[j * r:(j + 1) * r, p * LANES:(p + 1) * LANES]

    def stage1(it, carry):
        for u in range(unroll):
            n2 = it * unroll + u
            rhs = jnp.concatenate(
                [jnp.concatenate([w_scr[ri * n_col + q, pl.ds(n2, r, stride=rp), :] for q in range(n_col)], axis=1)
                 for ri in range(2)], axis=0).astype(BF16)
            a = jnp.dot(dtw_ref[n2].astype(BF16), rhs, preferred_element_type=F32)
            for q in range(n_col):
                a2_scr[q, pl.ds(n2, r, stride=ap), :] = a[:r, q * LANES:(q + 1) * LANES]
                a2_scr[q, pl.ds(r + n2, r, stride=ap), :] = a[r:, q * LANES:(q + 1) * LANES]
        return carry

    lax.fori_loop(0, r // unroll, stage1, 0)

    def stage2(it, carry):
        for u in range(unroll):
            k1 = it * unroll + u
            row0 = pl.multiple_of(k1 * ap, 8)
            blk = jnp.concatenate([a2_scr[q, pl.ds(row0, 2 * r), :] for q in range(n_col)],
                                  axis=1).astype(BF16)
            y = jnp.dot(d2_ref[...], blk, preferred_element_type=F32)
            for q in range(n_col):
                y_scr[hf * n_col + q, pl.ds(k1, r, stride=rp), :] = y[:, q * LANES:(q + 1) * LANES]
        return carry

    lax.fori_loop(0, r // unroll, stage2, 0)

    @pl.when(hf == pl.num_programs(1) - 1)
    def _():
        n_all = y_scr.shape[0]
        for k2 in range(r):
            ys = [y_scr[q, k2 * rp:k2 * rp + r, :] for q in range(n_all)]
            ss = sum(jnp.sum(y * y, axis=-1, keepdims=True) for y in ys)
            inv = lax.rsqrt(ss * (1.0 / (n_all * LANES)) + EPS)
            for q in range(n_all):
                out_ref[0, k2 * r:(k2 + 1) * r, q * LANES:(q + 1) * LANES] = (
                    ys[q] * inv * gb_ref[:, q * LANES:(q + 1) * LANES]).astype(BF16)


def _fnet(zb_cols, gb, bsz):
    n_all, t, _ = zb_cols.shape
    s = t // bsz
    d_b = n_all * LANES
    n_col = 2
    r = FFT_RADIX
    rp, ap = r + FFT_PAD, 2 * r + FFT_PAD
    m_ch, d_tw, d2 = _fnet_tables(s, n_col * LANES)
    return pl.pallas_call(
        functools.partial(_fnet_kernel, n_col=n_col, unroll=32),
        grid=(bsz, n_all // n_col),
        in_specs=[
            pl.BlockSpec((n_col, s, LANES), lambda i, h: (h, i, 0)),
            pl.BlockSpec(m_ch.shape, lambda i, h: (0, 0)),
            pl.BlockSpec(d_tw.shape, lambda i, h: (0, 0, 0)),
            pl.BlockSpec(d2.shape, lambda i, h: (0, 0)),
            pl.BlockSpec((1, d_b), lambda i, h: (0, 0)),
        ],
        out_specs=pl.BlockSpec((1, s, d_b), lambda i, h: (i, 0, 0)),
        out_shape=jax.ShapeDtypeStruct((bsz, s, d_b), BF16),
        scratch_shapes=[pltpu.VMEM((2 * n_col, r * rp, LANES), F32), pltpu.VMEM((n_col, r * ap, LANES), F32),
                        pltpu.VMEM((n_all, r * rp, LANES), F32),
                        pltpu.VMEM(m_ch.shape, BF16), pltpu.VMEM(d2.shape, BF16)],
        compiler_params=pltpu.CompilerParams(dimension_semantics=("parallel", "arbitrary"),
                                             vmem_limit_bytes=VMEM_LIMIT),
        name="fnet",
    )(zb_cols, m_ch, d_tw, d2, gb)


def _post_kernel(x_ref, an_ref, bn_ref, woa_ref, wob_ref, g2_ref, wrt_ref, rb_ref, wsgu_ref, wsd_ref,
                 hs_ref, hn_ref, eidx_ref, gate_ref, cnt_ref, carry_scr,
                 *, n_experts, d_shared, steps_per_group):
    i = pl.program_id(0)

    @pl.when(i % steps_per_group == 0)
    def _():
        carry_scr[...] = jnp.zeros_like(carry_scr)

    h = (x_ref[...]
         + jnp.dot(an_ref[...], woa_ref[...], preferred_element_type=F32)
         + jnp.dot(bn_ref[...], wob_ref[...], preferred_element_type=F32))
    hn = _rms(h, g2_ref[...])
    hnb = hn.astype(BF16)
    gu = jnp.dot(hnb, wsgu_ref[...], preferred_element_type=F32)
    act = (_silu(gu[:, :d_shared]) * gu[:, d_shared:]).astype(BF16)
    hs = h + jnp.dot(act, wsd_ref[...], preferred_element_type=F32)
    n_slab = hn.shape[1] // LANES
    for s in range(n_slab):
        hn_ref[pl.ds(s, hn.shape[0], stride=n_slab), :] = hn[:, s * LANES:(s + 1) * LANES]
        hs_ref[pl.ds(s, hn.shape[0], stride=n_slab), :] = hs[:, s * LANES:(s + 1) * LANES]

    rows = hn.shape[0]
    eg = n_experts // N_EXPERT_GROUPS
    logits_t = lax.dot_general(wrt_ref[...], hn, (((1,), (1,)), ((), ())),
                               precision=lax.Precision.HIGHEST, preferred_element_type=F32)
    scores_t = 1.0 / (1.0 + jnp.exp(-logits_t))
    biased_t = scores_t + rb_ref[...]
    sub = lax.broadcasted_iota(jnp.int32, (eg, rows), 0)
    neg = -jnp.inf
    grp = [biased_t[g * eg:(g + 1) * eg, :] for g in range(N_EXPERT_GROUPS)]
    sco = [scores_t[g * eg:(g + 1) * eg, :] for g in range(N_EXPERT_GROUPS)]
    gid = [sub + g * eg for g in range(N_EXPERT_GROUPS)]

    gscore = []
    for g in range(N_EXPERT_GROUPS):
        m1 = jnp.max(grp[g], axis=0, keepdims=True)
        first = jnp.min(jnp.where(grp[g] == m1, sub, eg), axis=0, keepdims=True)
        m2 = jnp.max(jnp.where(sub == first, neg, grp[g]), axis=0, keepdims=True)
        gscore.append(m1 + m2)
    masked = []
    for g in range(N_EXPERT_GROUPS):
        beaten = jnp.zeros((1, rows), jnp.int32)
        for j in range(N_EXPERT_GROUPS):
            if j == g:
                continue
            wins = (gscore[j] > gscore[g]) if j > g else (gscore[j] >= gscore[g])
            beaten = beaten + wins.astype(jnp.int32)
        masked.append(jnp.where(beaten < TOPK_GROUPS, grp[g], neg))

    sel = [jnp.zeros((eg, rows), F32) for _ in range(N_EXPERT_GROUPS)]
    e_rows, w_rows = [], []
    for _ in range(TOP_K):
        m = masked[0]
        for g in range(1, N_EXPERT_GROUPS):
            m = jnp.maximum(m, masked[g])
        mk = jnp.max(m, axis=0, keepdims=True)
        cand = jnp.where(masked[0] == mk, gid[0], n_experts)
        for g in range(1, N_EXPERT_GROUPS):
            cand = jnp.minimum(cand, jnp.where(masked[g] == mk, gid[g], n_experts))
        ik = jnp.min(cand, axis=0, keepdims=True)
        wk = jnp.zeros((eg, rows), F32)
        for g in range(N_EXPERT_GROUPS):
            hit = gid[g] == ik
            masked[g] = jnp.where(hit, neg, masked[g])
            sel[g] = jnp.where(hit, 1.0, sel[g])
            wk = wk + jnp.where(hit, sco[g], 0.0)
        e_rows.append(ik)
        w_rows.append(jnp.sum(wk, axis=0, keepdims=True))
    wsum = w_rows[0]
    for k in range(1, TOP_K):
        wsum = wsum + w_rows[k]
    gates = [w_rows[k] / wsum * ROUTED_SCALE for k in range(TOP_K)]

    sel_t = jnp.concatenate(sel, axis=0)
    new_carry = carry_scr[...] + jnp.sum(sel_t, axis=1, keepdims=True)
    carry_scr[...] = new_carry
    cnt_ref[0] = new_carry.astype(jnp.int32)
    eidx_ref[...] = jnp.concatenate(e_rows, axis=0)
    gate_ref[...] = jnp.concatenate(gates, axis=0)


def _post(x2, an, bn, woa, wob, g2, wrt, rb, wsgu, wsd):
    t, d = x2.shape
    rows = FRONT_ROWS
    n_experts = wrt.shape[0]
    d_shared = wsd.shape[0]
    n_slab = d // LANES
    steps_per_group = GROUP_TOKENS // rows
    n_groups = t // GROUP_TOKENS
    const = lambda shape: pl.BlockSpec(shape, lambda i: (0,) * len(shape))
    row_blk = lambda w: pl.BlockSpec((rows, w), lambda i: (i, 0))
    slab_blk = pl.BlockSpec((rows * n_slab, LANES), lambda i: (i, 0))
    tok_blk = pl.BlockSpec((TOP_K, rows), lambda i: (0, i))
    return pl.pallas_call(
        functools.partial(_post_kernel, n_experts=n_experts, d_shared=d_shared,
                          steps_per_group=steps_per_group),
        grid=(t // rows,),
        in_specs=[row_blk(d), row_blk(an.shape[1]), row_blk(bn.shape[1]),
                  const(woa.shape), const(wob.shape), const((1, d)), const(wrt.shape), const(rb.shape),
                  const(wsgu.shape), const(wsd.shape)],
        out_specs=[slab_blk, slab_blk, tok_blk, tok_blk,
                   pl.BlockSpec((1, n_experts, LANES), lambda i: (i // steps_per_group, 0, 0))],
        out_shape=[jax.ShapeDtypeStruct((t * n_slab, LANES), F32), jax.ShapeDtypeStruct((t * n_slab, LANES), F32),
                   jax.ShapeDtypeStruct((TOP_K, t), jnp.int32), jax.ShapeDtypeStruct((TOP_K, t), F32),
                   jax.ShapeDtypeStruct((n_groups, n_experts, LANES), jnp.int32)],
        scratch_shapes=[pltpu.VMEM((n_experts, LANES), F32)],
        compiler_params=pltpu.CompilerParams(dimension_semantics=("arbitrary",),
                                             vmem_limit_bytes=VMEM_LIMIT),
        name="post",
    )(x2, an, bn, woa, wob, g2, wrt, rb, wsgu, wsd)


def _moe_kernel(cnt_ref, off_ref, next_ref, rows_ref, gates_ref, hn_ref, hs_hbm, wg_ref, wu_ref, wd_ref, gf_ref,
                out_ref, acc_scr, x_scr, y_scr, state_scr, sem, *, n_slab):
    g = pl.program_id(0)
    e = pl.program_id(1)
    n_e = pl.num_programs(1)
    tokens = GROUP_TOKENS
    blk = EXPERT_ROWS
    batch = 16

    def slab(ref, row0):
        return ref.at[pl.ds(pl.multiple_of(row0, n_slab), n_slab), :]

    def acc_init(group):
        return pltpu.make_async_copy(
            hs_hbm.at[pl.ds(pl.multiple_of(group * tokens * n_slab, n_slab), tokens * n_slab), :],
            acc_scr.at[pl.ds(0, tokens * n_slab), :], sem)

    def gather(base):
        for r in range(blk):
            slab(x_scr, r * n_slab)[...] = slab(hn_ref, rows_ref[base + r])[...]

    def scatter(base):
        for j in range(blk // batch):
            rows, vals = [], []
            for u in range(batch):
                r = j * batch + u
                row0 = rows_ref[base + r]
                rows.append(row0)
                vals.append(slab(acc_scr, row0)[...] + slab(y_scr, r * n_slab)[...])
            for u in reversed(range(batch)):
                slab(acc_scr, rows[u])[...] = vals[u]

    n = cnt_ref[g * n_e + e]
    off = off_ref[g * n_e + e]

    @pl.when(e == 0)
    def _():
        @pl.when(g == 0)
        def _():
            acc_init(g).start()

        y_scr[...] = jnp.zeros_like(y_scr)
        state_scr[0] = 0
        gather(jnp.where(n > 0, off, next_ref[g * n_e + e]))
        acc_init(g).wait()

    n_blocks = (n + blk - 1) // blk
    nxt = next_ref[g * n_e + e]
    row_id = lax.broadcasted_iota(jnp.int32, (blk, 1), 0)
    diag_row = lax.broadcasted_iota(jnp.int32, (LANES, 2 * LANES), 0)
    diag_lane = lax.broadcasted_iota(jnp.int32, (LANES, 2 * LANES), 1)

    def block(i, carry):
        base = off + i * blk
        xb = jnp.concatenate([x_scr[pl.ds(s, blk, stride=n_slab), :] for s in range(n_slab)], axis=1).astype(BF16)
        scatter(state_scr[0])
        gather(jnp.where(i + 1 < n_blocks, base + blk, nxt))
        act = (_silu(jnp.dot(xb, wg_ref[0], preferred_element_type=F32))
               * jnp.dot(xb, wu_ref[0], preferred_element_type=F32)).astype(BF16)
        y = jnp.dot(act, wd_ref[0], preferred_element_type=F32)
        first_row = lax.shift_right_logical(base, LANES.bit_length() - 1)
        shift = base & (LANES - 1)
        cols = []
        for j in range(blk // LANES):
            pair = jnp.concatenate([gates_ref[pl.ds(first_row + j, 1), :],
                                    gates_ref[pl.ds(first_row + j + 1, 1), :]], axis=1)
            cols.append(jnp.sum(jnp.where(diag_lane == diag_row + shift, pair, 0.0), axis=1, keepdims=True))
        gate_col = cols[0] if len(cols) == 1 else jnp.concatenate(cols, axis=0)
        y = y * jnp.where(row_id < n - i * blk, gate_col, 0.0)
        for s in range(n_slab):
            y_scr[pl.ds(s, blk, stride=n_slab), :] = y[:, s * LANES:(s + 1) * LANES]
        state_scr[0] = base
        return carry

    lax.fori_loop(0, n_blocks, block, 0)

    @pl.when(e == n_e - 1)
    def _():
        scatter(state_scr[0])
        rows = 256
        for c in range(tokens // rows):
            acc = jnp.concatenate([acc_scr[pl.ds(c * rows * n_slab + s, rows, stride=n_slab), :]
                                   for s in range(n_slab)], axis=1)
            out_ref[c * rows:(c + 1) * rows, :] = _rms(acc, gf_ref[...])

        @pl.when(g + 1 < pl.num_programs(0))
        def _():
            acc_init(g + 1).start()


def _moe(cnt_flat, off_flat, next_flat, rows_flat, gates2d, hn3, hs3, wg, wu, wd, gf):
    n_slab = gf.shape[1] // LANES
    t = hn3.shape[0] // n_slab
    d = gf.shape[1]
    n_experts, d_expert = wd.shape[0], wd.shape[1]
    tokens = GROUP_TOKENS
    n_groups = t // tokens
    list_len = tokens * TOP_K + LIST_PAD
    assert EXPERT_ROWS % LANES == 0 and EXPERT_ROWS + LANES <= LIST_PAD and list_len % (8 * LANES) == 0
    grid_spec = pltpu.PrefetchScalarGridSpec(
        num_scalar_prefetch=3,
        grid=(n_groups, n_experts),
        in_specs=[pl.BlockSpec((list_len,), lambda g, e, *_: (g,), memory_space=pltpu.SMEM),
                  pl.BlockSpec((list_len // LANES, LANES), lambda g, e, *_: (g, 0)),
                  pl.BlockSpec((tokens * n_slab, LANES), lambda g, e, *_: (g, 0)),
                  pl.BlockSpec(memory_space=pl.ANY),
                  pl.BlockSpec((1, d, d_expert), lambda g, e, *_: (e, 0, 0)),
                  pl.BlockSpec((1, d, d_expert), lambda g, e, *_: (e, 0, 0)),
                  pl.BlockSpec((1, d_expert, d), lambda g, e, *_: (e, 0, 0)),
                  pl.BlockSpec((1, d), lambda g, e, *_: (0, 0))],
        out_specs=pl.BlockSpec((tokens, d), lambda g, e, *_: (g, 0)),
        scratch_shapes=[pltpu.VMEM((tokens * n_slab, LANES), F32),
                        pltpu.VMEM((EXPERT_ROWS * n_slab, LANES), F32),
                        pltpu.VMEM((EXPERT_ROWS * n_slab, LANES), F32),
                        pltpu.SMEM((1,), jnp.int32),
                        pltpu.SemaphoreType.DMA],
    )
    return pl.pallas_call(
        functools.partial(_moe_kernel, n_slab=n_slab),
        grid_spec=grid_spec,
        out_shape=jax.ShapeDtypeStruct((t, d), F32),
        compiler_params=pltpu.CompilerParams(dimension_semantics=("arbitrary", "arbitrary"),
                                             vmem_limit_bytes=VMEM_LIMIT),
        name="moe",
    )(cnt_flat, off_flat, next_flat, rows_flat, gates2d, hn3, hs3, wg, wu, wd, gf)


def kernel(x, norm1_g, w_in, sgu_ln_g, sgu_ln_b, w_spatial, b_spatial, out_norm_a_g, out_norm_b_g, w_out,
           norm2_g, w_router, router_bias, w_gate, w_up, w_down, ws_gate, ws_up, ws_down, final_norm_g):
    bsz, s, d = x.shape
    depth = w_in.shape[0]
    assert depth == 1, "the fused final norm assumes a single trunk layer"
    l = 0
    t = bsz * s
    d_a = sgu_ln_g.shape[-1]
    d_b = out_norm_b_g.shape[-1]
    n_heads, chunk = w_spatial.shape[1], w_spatial.shape[2]
    n_experts = w_router.shape[-1]
    assert d_a == n_heads * A_HEAD_DIM and chunk == LANES and 2 * A_HEAD_DIM == LANES
    assert t % FRONT_ROWS == 0 and FRONT_ROWS % chunk == 0 and s % FRONT_ROWS == 0
    assert t % GROUP_TOKENS == 0 and GROUP_TOKENS % FRONT_ROWS == 0
    assert (GROUP_TOKENS * TOP_K) & (GROUP_TOKENS * TOP_K - 1) == 0, "sort keys pack the assignment index in low bits"
    row = lambda v: v.reshape(1, -1).astype(F32)

    x2 = x.reshape(t, d)

    win = w_in[l].astype(BF16)
    ws = w_spatial[l]
    ws_pair = jnp.concatenate([ws[0::2], ws[1::2]], axis=2).astype(BF16)
    bs_full = jnp.repeat(b_spatial[l].T, A_HEAD_DIM, axis=1).astype(F32)
    woa = w_out[l][:d_a].astype(BF16)
    wob = w_out[l][d_a:].astype(BF16)
    wrt = w_router[l].T.astype(F32)
    rb = jnp.broadcast_to(router_bias[l].astype(F32)[:, None], (n_experts, 1))
    wsgu = jnp.concatenate([ws_gate[l], ws_up[l]], axis=1).astype(BF16)
    wsd = ws_down[l].astype(BF16)
    wg = w_gate[l].astype(BF16)
    wu = w_up[l].astype(BF16)
    wd = w_down[l].astype(BF16)

    an, zb = _front(x2, row(norm1_g[l]), win, row(sgu_ln_g[l]), row(sgu_ln_b[l]), ws_pair, bs_full,
                    row(out_norm_a_g[l]), d_a=d_a, d_b=d_b, chunk=chunk)
    bn = _fnet(zb, row(out_norm_b_g[l]), bsz).reshape(t, d_b)

    hs3, hn3, eidx, gate, cnt = _post(x2, an, bn, woa, wob, row(norm2_g[l]), wrt, rb, wsgu, wsd)

    n_groups = t // GROUP_TOKENS
    n_assign = GROUP_TOKENS * TOP_K
    counts = cnt[:, :, 0]
    offs = jnp.cumsum(counts, axis=1) - counts
    tok_bits = (n_assign - 1).bit_length()
    local = jnp.arange(n_assign, dtype=jnp.int32).reshape(1, GROUP_TOKENS, TOP_K)
    eidx_g = eidx.reshape(TOP_K, n_groups, GROUP_TOKENS).transpose(1, 2, 0)
    gate_g = gate.reshape(TOP_K, n_groups, GROUP_TOKENS).transpose(1, 2, 0).reshape(n_groups, n_assign)
    keys = ((eidx_g << tok_bits) | local).reshape(n_groups, n_assign)
    keys_sorted, gates_sorted = lax.sort((keys, gate_g), dimension=1, num_keys=1, is_stable=False)
    rows_sorted = (keys_sorted & (n_assign - 1)) // TOP_K * (d // LANES)
    fill = ((0, 0), (0, LIST_PAD))
    rows_flat = jnp.pad(rows_sorted, fill).reshape(-1)
    gates2d = jnp.pad(gates_sorted, fill).reshape(-1, LANES)
    e_row = jnp.arange(n_experts, dtype=jnp.int32)
    later = (e_row[None, None, :] > e_row[None, :, None]) & (counts[:, None, :] > 0)
    first_later = jnp.min(jnp.where(later, e_row[None, None, :], n_experts), axis=2)
    next_off = jnp.sum(jnp.where(first_later[:, :, None] == e_row[None, None, :], offs[:, None, :], 0), axis=2)
    flat = lambda a: a.reshape(-1).astype(jnp.int32)

    out = _moe(flat(counts), flat(offs), flat(next_off), rows_flat, gates2d,
               hn3, hs3, wg, wu, wd, row(final_norm_g))
    return out.reshape(bsz, s, d)
```

```python
import functools
import math

import numpy as np
import jax
import jax.numpy as jnp
from jax import lax
from jax.experimental import pallas as pl
from jax.experimental.pallas import tpu as pltpu

A_HEAD_DIM = 64
B_GROUP_DIM = 64
TOP_K = 8
N_EXPERT_GROUPS = 8
TOPK_GROUPS = 4
ROUTED_SCALE = 2.5
EPS = 1e-6

LANES = 128
FRONT_ROWS = 1024
EXPERT_ROWS = 128
DOWN_PART_LANES = 256
LIST_PAD = 1024
GROUP_TOKENS = 2048
FFT_RADIX = 64
FFT_PAD = 8
VMEM_LIMIT = 56 * 1024 * 1024

F32 = jnp.float32
BF16 = jnp.bfloat16


def _rms(x, g):
    return x * lax.rsqrt(jnp.mean(x * x, axis=-1, keepdims=True) + EPS) * g


def _gelu(x):
    return 0.5 * x * (1.0 + lax.erf(x * (1.0 / math.sqrt(2.0))))


def _silu(x):
    return x / (1.0 + jnp.exp(-x))


def _front_kernel(x_ref, g1_ref, win_ref, lng_ref, lnb_ref, ws_ref, bs_ref, ga_ref,
                  an_ref, zb_ref, ya_scr, *, d_a, chunk):
    x = x_ref[...]
    xn = _rms(x, g1_ref[...]).astype(BF16)
    z = jnp.dot(xn, win_ref[...], preferred_element_type=F32)
    for q in range(zb_ref.shape[0]):
        zb_ref[q] = z[:, 2 * d_a + q * LANES:2 * d_a + (q + 1) * LANES].astype(BF16)
    u = _gelu(z[:, :d_a])
    v = _gelu(z[:, d_a:2 * d_a])
    mu = jnp.mean(v, axis=-1, keepdims=True)
    vc = v - mu
    v = vc * lax.rsqrt(jnp.mean(vc * vc, axis=-1, keepdims=True) + EPS) * lng_ref[...] + lnb_ref[...]
    rows = x.shape[0]
    lane = lax.broadcasted_iota(jnp.int32, (chunk, LANES), 1)
    low = lane < A_HEAD_DIM
    for c in range(rows // chunk):
        r0 = c * chunk
        for hp in range(d_a // LANES):
            c0 = hp * LANES
            vp = v[r0:r0 + chunk, c0:c0 + LANES]
            rhs = jnp.concatenate([jnp.where(low, vp, 0.0), jnp.where(low, 0.0, vp)], axis=0).astype(BF16)
            mixed = jnp.dot(ws_ref[hp], rhs, preferred_element_type=F32) + bs_ref[:, c0:c0 + LANES]
            ya_scr[r0:r0 + chunk, c0:c0 + LANES] = u[r0:r0 + chunk, c0:c0 + LANES] * mixed
    an_ref[...] = _rms(ya_scr[...], ga_ref[...]).astype(BF16)


def _front(x2, g1, win, lng, lnb, ws_pair, bs_full, ga, *, d_a, d_b, chunk):
    t, d = x2.shape
    d_in = win.shape[1]
    rows = FRONT_ROWS
    const = lambda shape: pl.BlockSpec(shape, lambda i: (0,) * len(shape))
    return pl.pallas_call(
        functools.partial(_front_kernel, d_a=d_a, chunk=chunk),
        grid=(t // rows,),
        in_specs=[
            pl.BlockSpec((rows, d), lambda i: (i, 0)),
            const((1, d)), const((d, d_in)), const((1, d_a)), const((1, d_a)),
            const(ws_pair.shape), const(bs_full.shape), const((1, d_a)),
        ],
        out_specs=[pl.BlockSpec((rows, d_a), lambda i: (i, 0)),
                   pl.BlockSpec((d_b // LANES, rows, LANES), lambda i: (0, i, 0))],
        out_shape=[jax.ShapeDtypeStruct((t, d_a), BF16), jax.ShapeDtypeStruct((d_b // LANES, t, LANES), BF16)],
        scratch_shapes=[pltpu.VMEM((rows, d_a), F32)],
        compiler_params=pltpu.CompilerParams(dimension_semantics=("parallel",),
                                             vmem_limit_bytes=VMEM_LIMIT),
        name="front",
    )(x2, g1, win, lng, lnb, ws_pair, bs_full, ga)


def _fnet_tables(seq, half_w):
    r = FFT_RADIX
    assert seq == r * r
    j = np.arange(B_GROUP_DIM)
    ang = 2.0 * np.pi * np.outer(j, j) / B_GROUP_DIM
    ng = half_w // B_GROUP_DIM
    eye = np.eye(ng)
    m_ch = np.concatenate([np.kron(eye, np.cos(ang)), np.kron(eye, np.sin(ang))], axis=1)
    k1 = np.arange(r)[:, None]
    n1 = np.arange(r)[None, :]
    d_tw = np.zeros((r, 2 * r, 2 * r))
    for n2 in range(r):
        th = 2.0 * np.pi * (k1 * n1 / r + k1 * n2 / (r * r))
        c, s = np.cos(th), np.sin(th)
        d_tw[n2] = np.block([[c, -s], [s, c]])
    ph = 2.0 * np.pi * np.outer(np.arange(r), np.arange(r)) / r
    scale = 1.0 / math.sqrt(seq * B_GROUP_DIM)
    d2 = np.concatenate([np.cos(ph), -np.sin(ph)], axis=1) * scale
    return (jnp.asarray(m_ch, F32), jnp.asarray(d_tw, F32), jnp.asarray(d2, F32))


def _fnet_kernel(z_ref, mch32_ref, dtw_ref, d232_ref, gb_ref, out_ref, w_scr, a2_scr, y_scr, mch_ref, d2_ref,
                 *, n_col, unroll):
    r = FFT_RADIX
    rp = r + FFT_PAD
    ap = 2 * r + FFT_PAD
    hf = pl.program_id(1)
    half_w = n_col * LANES
    mch_ref[...] = mch32_ref[...].astype(BF16)
    d2_ref[...] = d232_ref[...].astype(BF16)

    chunk = 8 * r
    for c in range(r * r // chunk):
        zc = jnp.concatenate([z_ref[q, c * chunk:(c + 1) * chunk, :] for q in range(n_col)], axis=1)
        w = jnp.dot(zc, mch_ref[...], preferred_element_type=F32)
        for j in range(chunk // r):
            n1 = c * (chunk // r) + j
            for p in range(2 * n_col):
                w_scr[p, n1 * rp:n1 * rp + r, :] = w[j * r:(j + 1) * r, p * LANES:(p + 1) * LANES]

    def stage1(it, carry):
        for u in range(unroll):
            n2 = it * unroll + u
            rhs = jnp.concatenate(
                [jnp.concatenate([w_scr[ri * n_col + q, pl.ds(n2, r, stride=rp), :] for q in range(n_col)], axis=1)
                 for ri in range(2)], axis=0).astype(BF16)
            a = jnp.dot(dtw_ref[n2].astype(BF16), rhs, preferred_element_type=F32)
            for q in range(n_col):
                a2_scr[q, pl.ds(n2, r, stride=ap), :] = a[:r, q * LANES:(q + 1) * LANES]
                a2_scr[q, pl.ds(r + n2, r, stride=ap), :] = a[r:, q * LANES:(q + 1) * LANES]
        return carry

    lax.fori_loop(0, r // unroll, stage1, 0)

    def stage2(it, carry):
        for u in range(unroll):
            k1 = it * unroll + u
            row0 = pl.multiple_of(k1 * ap, 8)
            blk = jnp.concatenate([a2_scr[q, pl.ds(row0, 2 * r), :] for q in range(n_col)],
                                  axis=1).astype(BF16)
            y = jnp.dot(d2_ref[...], blk, preferred_element_type=F32)
            for q in range(n_col):
                y_scr[hf * n_col + q, pl.ds(k1, r, stride=rp), :] = y[:, q * LANES:(q + 1) * LANES]
        return carry

    lax.fori_loop(0, r // unroll, stage2, 0)

    @pl.when(hf == pl.num_programs(1) - 1)
    def _():
        n_all = y_scr.shape[0]
        for k2 in range(r):
            ys = [y_scr[q, k2 * rp:k2 * rp + r, :] for q in range(n_all)]
            ss = sum(jnp.sum(y * y, axis=-1, keepdims=True) for y in ys)
            inv = lax.rsqrt(ss * (1.0 / (n_all * LANES)) + EPS)
            for q in range(n_all):
                out_ref[0, k2 * r:(k2 + 1) * r, q * LANES:(q + 1) * LANES] = (
                    ys[q] * inv * gb_ref[:, q * LANES:(q + 1) * LANES]).astype(BF16)


def _fnet(zb_cols, gb, bsz):
    n_all, t, _ = zb_cols.shape
    s = t // bsz
    d_b = n_all * LANES
    n_col = 2
    r = FFT_RADIX
    rp, ap = r + FFT_PAD, 2 * r + FFT_PAD
    m_ch, d_tw, d2 = _fnet_tables(s, n_col * LANES)
    return pl.pallas_call(
        functools.partial(_fnet_kernel, n_col=n_col, unroll=32),
        grid=(bsz, n_all // n_col),
        in_specs=[
            pl.BlockSpec((n_col, s, LANES), lambda i, h: (h, i, 0)),
            pl.BlockSpec(m_ch.shape, lambda i, h: (0, 0)),
            pl.BlockSpec(d_tw.shape, lambda i, h: (0, 0, 0)),
            pl.BlockSpec(d2.shape, lambda i, h: (0, 0)),
            pl.BlockSpec((1, d_b), lambda i, h: (0, 0)),
        ],
        out_specs=pl.BlockSpec((1, s, d_b), lambda i, h: (i, 0, 0)),
        out_shape=jax.ShapeDtypeStruct((bsz, s, d_b), BF16),
        scratch_shapes=[pltpu.VMEM((2 * n_col, r * rp, LANES), F32), pltpu.VMEM((n_col, r * ap, LANES), F32),
                        pltpu.VMEM((n_all, r * rp, LANES), F32),
                        pltpu.VMEM(m_ch.shape, BF16), pltpu.VMEM(d2.shape, BF16)],
        compiler_params=pltpu.CompilerParams(dimension_semantics=("parallel", "arbitrary"),
                                             vmem_limit_bytes=VMEM_LIMIT),
        name="fnet",
    )(zb_cols, m_ch, d_tw, d2, gb)


def _post_kernel(x_ref, an_ref, bn_ref, woa_ref, wob_ref, g2_ref, wrt_ref, rb_ref, wsgu_ref, wsd_ref,
                 hs_ref, hn_ref, eidx_ref, gate_ref, cnt_ref, carry_scr,
                 *, n_experts, d_shared, steps_per_group):
    i = pl.program_id(0)

    @pl.when(i % steps_per_group == 0)
    def _():
        carry_scr[...] = jnp.zeros_like(carry_scr)

    h = (x_ref[...]
         + jnp.dot(an_ref[...], woa_ref[...], preferred_element_type=F32)
         + jnp.dot(bn_ref[...], wob_ref[...], preferred_element_type=F32))
    hn = _rms(h, g2_ref[...])
    hnb = hn.astype(BF16)
    gu = jnp.dot(hnb, wsgu_ref[...], preferred_element_type=F32)
    act = (_silu(gu[:, :d_shared]) * gu[:, d_shared:]).astype(BF16)
    hs = h + jnp.dot(act, wsd_ref[...], preferred_element_type=F32)
    n_slab = hn.shape[1] // LANES
    for s in range(n_slab):
        hn_ref[pl.ds(s, hn.shape[0], stride=n_slab), :] = hn[:, s * LANES:(s + 1) * LANES]
        hs_ref[pl.ds(s, hn.shape[0], stride=n_slab), :] = hs[:, s * LANES:(s + 1) * LANES]

    rows = hn.shape[0]
    eg = n_experts // N_EXPERT_GROUPS
    logits_t = lax.dot_general(wrt_ref[...], hn, (((1,), (1,)), ((), ())),
                               precision=lax.Precision.HIGHEST, preferred_element_type=F32)
    scores_t = 1.0 / (1.0 + jnp.exp(-logits_t))
    biased_t = scores_t + rb_ref[...]
    sub = lax.broadcasted_iota(jnp.int32, (eg, rows), 0)
    neg = -jnp.inf
    grp = [biased_t[g * eg:(g + 1) * eg, :] for g in range(N_EXPERT_GROUPS)]
    sco = [scores_t[g * eg:(g + 1) * eg, :] for g in range(N_EXPERT_GROUPS)]
    gid = [sub + g * eg for g in range(N_EXPERT_GROUPS)]

    gscore = []
    for g in range(N_EXPERT_GROUPS):
        m1 = jnp.max(grp[g], axis=0, keepdims=True)
        first = jnp.min(jnp.where(grp[g] == m1, sub, eg), axis=0, keepdims=True)
        m2 = jnp.max(jnp.where(sub == first, neg, grp[g]), axis=0, keepdims=True)
        gscore.append(m1 + m2)
    masked = []
    for g in range(N_EXPERT_GROUPS):
        beaten = jnp.zeros((1, rows), jnp.int32)
        for j in range(N_EXPERT_GROUPS):
            if j == g:
                continue
            wins = (gscore[j] > gscore[g]) if j > g else (gscore[j] >= gscore[g])
            beaten = beaten + wins.astype(jnp.int32)
        masked.append(jnp.where(beaten < TOPK_GROUPS, grp[g], neg))

    sel = [jnp.zeros((eg, rows), F32) for _ in range(N_EXPERT_GROUPS)]
    e_rows, w_rows = [], []
    for _ in range(TOP_K):
        m = masked[0]
        for g in range(1, N_EXPERT_GROUPS):
            m = jnp.maximum(m, masked[g])
        mk = jnp.max(m, axis=0, keepdims=True)
        cand = jnp.where(masked[0] == mk, gid[0], n_experts)
        for g in range(1, N_EXPERT_GROUPS):
            cand = jnp.minimum(cand, jnp.where(masked[g] == mk, gid[g], n_experts))
        ik = jnp.min(cand, axis=0, keepdims=True)
        wk = jnp.zeros((eg, rows), F32)
        for g in range(N_EXPERT_GROUPS):
            hit = gid[g] == ik
            masked[g] = jnp.where(hit, neg, masked[g])
            sel[g] = jnp.where(hit, 1.0, sel[g])
            wk = wk + jnp.where(hit, sco[g], 0.0)
        e_rows.append(ik)
        w_rows.append(jnp.sum(wk, axis=0, keepdims=True))
    wsum = w_rows[0]
    for k in range(1, TOP_K):
        wsum = wsum + w_rows[k]
    gates = [w_rows[k] / wsum * ROUTED_SCALE for k in range(TOP_K)]

    sel_t = jnp.concatenate(sel, axis=0)
    new_carry = carry_scr[...] + jnp.sum(sel_t, axis=1, keepdims=True)
    carry_scr[...] = new_carry
    cnt_ref[0] = new_carry.astype(jnp.int32)
    eidx_ref[...] = jnp.concatenate(e_rows, axis=0)
    gate_ref[...] = jnp.concatenate(gates, axis=0)


def _post(x2, an, bn, woa, wob, g2, wrt, rb, wsgu, wsd):
    t, d = x2.shape
    rows = FRONT_ROWS
    n_experts = wrt.shape[0]
    d_shared = wsd.shape[0]
    n_slab = d // LANES
    steps_per_group = GROUP_TOKENS // rows
    n_groups = t // GROUP_TOKENS
    const = lambda shape: pl.BlockSpec(shape, lambda i: (0,) * len(shape))
    row_blk = lambda w: pl.BlockSpec((rows, w), lambda i: (i, 0))
    slab_blk = pl.BlockSpec((rows * n_slab, LANES), lambda i: (i, 0))
    tok_blk = pl.BlockSpec((TOP_K, rows), lambda i: (0, i))
    return pl.pallas_call(
        functools.partial(_post_kernel, n_experts=n_experts, d_shared=d_shared,
                          steps_per_group=steps_per_group),
        grid=(t // rows,),
        in_specs=[row_blk(d), row_blk(an.shape[1]), row_blk(bn.shape[1]),
                  const(woa.shape), const(wob.shape), const((1, d)), const(wrt.shape), const(rb.shape),
                  const(wsgu.shape), const(wsd.shape)],
        out_specs=[slab_blk, slab_blk, tok_blk, tok_blk,
                   pl.BlockSpec((1, n_experts, LANES), lambda i: (i // steps_per_group, 0, 0))],
        out_shape=[jax.ShapeDtypeStruct((t * n_slab, LANES), F32), jax.ShapeDtypeStruct((t * n_slab, LANES), F32),
                   jax.ShapeDtypeStruct((TOP_K, t), jnp.int32), jax.ShapeDtypeStruct((TOP_K, t), F32),
                   jax.ShapeDtypeStruct((n_groups, n_experts, LANES), jnp.int32)],
        scratch_shapes=[pltpu.VMEM((n_experts, LANES), F32)],
        compiler_params=pltpu.CompilerParams(dimension_semantics=("arbitrary",),
                                             vmem_limit_bytes=VMEM_LIMIT),
        name="post",
    )(x2, an, bn, woa, wob, g2, wrt, rb, wsgu, wsd)


def _moe_kernel(cnt_ref, off_ref, next_ref, rows_ref, gates_ref, hn_ref, hs_hbm, wg_ref, wu_ref, wd_ref, gf_ref,
                out_ref, acc_scr, x_scr, y_scr, state_scr, sem, *, n_slab):
    g = pl.program_id(0)
    e = pl.program_id(1)
    n_e = pl.num_programs(1)
    tokens = GROUP_TOKENS
    blk = EXPERT_ROWS
    batch = 16

    def slab(ref, row0):
        return ref.at[pl.ds(pl.multiple_of(row0, n_slab), n_slab), :]

    def acc_init(group):
        return pltpu.make_async_copy(
            hs_hbm.at[pl.ds(pl.multiple_of(group * tokens * n_slab, n_slab), tokens * n_slab), :],
            acc_scr.at[pl.ds(0, tokens * n_slab), :], sem)

    def gather(base):
        for r in range(blk):
            slab(x_scr, r * n_slab)[...] = slab(hn_ref, rows_ref[base + r])[...]

    def scatter(base):
        for j in range(blk // batch):
            rows, vals = [], []
            for u in range(batch):
                r = j * batch + u
                row0 = rows_ref[base + r]
                rows.append(row0)
                vals.append(slab(acc_scr, row0)[...] + slab(y_scr, r * n_slab)[...])
            for u in reversed(range(batch)):
                slab(acc_scr, rows[u])[...] = vals[u]

    n = cnt_ref[g * n_e + e]
    off = off_ref[g * n_e + e]

    @pl.when(e == 0)
    def _():
        @pl.when(g == 0)
        def _():
            acc_init(g).start()

        y_scr[...] = jnp.zeros_like(y_scr)
        state_scr[0] = 0
        gather(jnp.where(n > 0, off, next_ref[g * n_e + e]))
        acc_init(g).wait()

    n_blocks = (n + blk - 1) // blk
    nxt = next_ref[g * n_e + e]
    row_id = lax.broadcasted_iota(jnp.int32, (blk, 1), 0)
    diag_row = lax.broadcasted_iota(jnp.int32, (LANES, 2 * LANES), 0)
    diag_lane = lax.broadcasted_iota(jnp.int32, (LANES, 2 * LANES), 1)

    def block(i, carry):
        base = off + i * blk
        xb = jnp.concatenate([x_scr[pl.ds(s, blk, stride=n_slab), :] for s in range(n_slab)], axis=1).astype(BF16)
        scatter(state_scr[0])
        gather(jnp.where(i + 1 < n_blocks, base + blk, nxt))
        act = (_silu(jnp.dot(xb, wg_ref[0], preferred_element_type=F32))
               * jnp.dot(xb, wu_ref[0], preferred_element_type=F32)).astype(BF16)
        first_row = lax.shift_right_logical(base, LANES.bit_length() - 1)
        shift = base & (LANES - 1)
        cols = []
        for j in range(blk // LANES):
            pair = jnp.concatenate([gates_ref[pl.ds(first_row + j, 1), :],
                                    gates_ref[pl.ds(first_row + j + 1, 1), :]], axis=1)
            cols.append(jnp.sum(jnp.where(diag_lane == diag_row + shift, pair, 0.0), axis=1, keepdims=True))
        gate_col = cols[0] if len(cols) == 1 else jnp.concatenate(cols, axis=0)
        gate_col = jnp.where(row_id < n - i * blk, gate_col, 0.0)
        part = DOWN_PART_LANES
        for c in range(wd_ref.shape[2] // part):
            y = jnp.dot(act, wd_ref[0, :, c * part:(c + 1) * part], preferred_element_type=F32) * gate_col
            for s in range(part // LANES):
                y_scr[pl.ds(c * (part // LANES) + s, blk, stride=n_slab), :] = y[:, s * LANES:(s + 1) * LANES]
        state_scr[0] = base
        return carry

    lax.fori_loop(0, n_blocks, block, 0)

    @pl.when(e == n_e - 1)
    def _():
        scatter(state_scr[0])
        rows = 256
        for c in range(tokens // rows):
            acc = jnp.concatenate([acc_scr[pl.ds(c * rows * n_slab + s, rows, stride=n_slab), :]
                                   for s in range(n_slab)], axis=1)
            out_ref[c * rows:(c + 1) * rows, :] = _rms(acc, gf_ref[...])

        @pl.when(g + 1 < pl.num_programs(0))
        def _():
            acc_init(g + 1).start()


def _moe(cnt_flat, off_flat, next_flat, rows_flat, gates2d, hn3, hs3, wg, wu, wd, gf):
    n_slab = gf.shape[1] // LANES
    t = hn3.shape[0] // n_slab
    d = gf.shape[1]
    n_experts, d_expert = wd.shape[0], wd.shape[1]
    tokens = GROUP_TOKENS
    n_groups = t // tokens
    list_len = tokens * TOP_K + LIST_PAD
    assert EXPERT_ROWS % LANES == 0 and EXPERT_ROWS + LANES <= LIST_PAD and list_len % (8 * LANES) == 0
    grid_spec = pltpu.PrefetchScalarGridSpec(
        num_scalar_prefetch=3,
        grid=(n_groups, n_experts),
        in_specs=[pl.BlockSpec((list_len,), lambda g, e, *_: (g,), memory_space=pltpu.SMEM),
                  pl.BlockSpec((list_len // LANES, LANES), lambda g, e, *_: (g, 0)),
                  pl.BlockSpec((tokens * n_slab, LANES), lambda g, e, *_: (g, 0)),
                  pl.BlockSpec(memory_space=pl.ANY),
                  pl.BlockSpec((1, d, d_expert), lambda g, e, *_: (e, 0, 0)),
                  pl.BlockSpec((1, d, d_expert), lambda g, e, *_: (e, 0, 0)),
                  pl.BlockSpec((1, d_expert, d), lambda g, e, *_: (e, 0, 0)),
                  pl.BlockSpec((1, d), lambda g, e, *_: (0, 0))],
        out_specs=pl.BlockSpec((tokens, d), lambda g, e, *_: (g, 0)),
        scratch_shapes=[pltpu.VMEM((tokens * n_slab, LANES), F32),
                        pltpu.VMEM((EXPERT_ROWS * n_slab, LANES), F32),
                        pltpu.VMEM((EXPERT_ROWS * n_slab, LANES), F32),
                        pltpu.SMEM((1,), jnp.int32),
                        pltpu.SemaphoreType.DMA],
    )
    return pl.pallas_call(
        functools.partial(_moe_kernel, n_slab=n_slab),
        grid_spec=grid_spec,
        out_shape=jax.ShapeDtypeStruct((t, d), F32),
        compiler_params=pltpu.CompilerParams(dimension_semantics=("arbitrary", "arbitrary"),
                                             vmem_limit_bytes=VMEM_LIMIT),
        name="moe",
    )(cnt_flat, off_flat, next_flat, rows_flat, gates2d, hn3, hs3, wg, wu, wd, gf)


def kernel(x, norm1_g, w_in, sgu_ln_g, sgu_ln_b, w_spatial, b_spatial, out_norm_a_g, out_norm_b_g, w_out,
           norm2_g, w_router, router_bias, w_gate, w_up, w_down, ws_gate, ws_up, ws_down, final_norm_g):
    bsz, s, d = x.shape
    depth = w_in.shape[0]
    assert depth == 1, "the fused final norm assumes a single trunk layer"
    l = 0
    t = bsz * s
    d_a = sgu_ln_g.shape[-1]
    d_b = out_norm_b_g.shape[-1]
    n_heads, chunk = w_spatial.shape[1], w_spatial.shape[2]
    n_experts = w_router.shape[-1]
    assert d_a == n_heads * A_HEAD_DIM and chunk == LANES and 2 * A_HEAD_DIM == LANES
    assert t % FRONT_ROWS == 0 and FRONT_ROWS % chunk == 0 and s % FRONT_ROWS == 0
    assert t % GROUP_TOKENS == 0 and GROUP_TOKENS % FRONT_ROWS == 0
    assert (GROUP_TOKENS * TOP_K) & (GROUP_TOKENS * TOP_K - 1) == 0, "sort keys pack the assignment index in low bits"
    row = lambda v: v.reshape(1, -1).astype(F32)

    x2 = x.reshape(t, d)

    win = w_in[l].astype(BF16)
    ws = w_spatial[l]
    ws_pair = jnp.concatenate([ws[0::2], ws[1::2]], axis=2).astype(BF16)
    bs_full = jnp.repeat(b_spatial[l].T, A_HEAD_DIM, axis=1).astype(F32)
    woa = w_out[l][:d_a].astype(BF16)
    wob = w_out[l][d_a:].astype(BF16)
    wrt = w_router[l].T.astype(F32)
    rb = jnp.broadcast_to(router_bias[l].astype(F32)[:, None], (n_experts, 1))
    wsgu = jnp.concatenate([ws_gate[l], ws_up[l]], axis=1).astype(BF16)
    wsd = ws_down[l].astype(BF16)
    wg = w_gate[l].astype(BF16)
    wu = w_up[l].astype(BF16)
    wd = w_down[l].astype(BF16)

    an, zb = _front(x2, row(norm1_g[l]), win, row(sgu_ln_g[l]), row(sgu_ln_b[l]), ws_pair, bs_full,
                    row(out_norm_a_g[l]), d_a=d_a, d_b=d_b, chunk=chunk)
    bn = _fnet(zb, row(out_norm_b_g[l]), bsz).reshape(t, d_b)

    hs3, hn3, eidx, gate, cnt = _post(x2, an, bn, woa, wob, row(norm2_g[l]), wrt, rb, wsgu, wsd)

    n_groups = t // GROUP_TOKENS
    n_assign = GROUP_TOKENS * TOP_K
    counts = cnt[:, :, 0]
    offs = jnp.cumsum(counts, axis=1) - counts
    tok_bits = (n_assign - 1).bit_length()
    local = jnp.arange(n_assign, dtype=jnp.int32).reshape(1, GROUP_TOKENS, TOP_K)
    eidx_g = eidx.reshape(TOP_K, n_groups, GROUP_TOKENS).transpose(1, 2, 0)
    gate_g = gate.reshape(TOP_K, n_groups, GROUP_TOKENS).transpose(1, 2, 0).reshape(n_groups, n_assign)
    keys = ((eidx_g << tok_bits) | local).reshape(n_groups, n_assign)
    keys_sorted, gates_sorted = lax.sort((keys, gate_g), dimension=1, num_keys=1, is_stable=False)
    rows_sorted = (keys_sorted & (n_assign - 1)) // TOP_K * (d // LANES)
    fill = ((0, 0), (0, LIST_PAD))
    rows_flat = jnp.pad(rows_sorted, fill).reshape(-1)
    gates2d = jnp.pad(gates_sorted, fill).reshape(-1, LANES)
    e_row = jnp.arange(n_experts, dtype=jnp.int32)
    later = (e_row[None, None, :] > e_row[None, :, None]) & (counts[:, None, :] > 0)
    first_later = jnp.min(jnp.where(later, e_row[None, None, :], n_experts), axis=2)
    next_off = jnp.sum(jnp.where(first_later[:, :, None] == e_row[None, None, :], offs[:, None, :], 0), axis=2)
    flat = lambda a: a.reshape(-1).astype(jnp.int32)

    out = _moe(flat(counts), flat(offs), flat(next_off), rows_flat, gates2d,
               hn3, hs3, wg, wu, wd, row(final_norm_g))
    return out.reshape(bsz, s, d)
```
